```python
import jax, jax.numpy as jnp
from jax import lax
import numpy as np

D_MODEL = 1024
BATCH = 8
SEQ = 4096
DEPTH = 4

HEAD_DIM = 64
N_HEADS_TOTAL = D_MODEL // HEAD_DIM
A_Q_HEADS = N_HEADS_TOTAL // 4
A_KV_HEADS = max(1, A_Q_HEADS // 2)
B_HEADS = (N_HEADS_TOTAL - A_Q_HEADS) // 2
C_HEADS = N_HEADS_TOTAL - A_Q_HEADS - B_HEADS
A_Q_W = A_Q_HEADS * HEAD_DIM
A_KV_W = A_KV_HEADS * HEAD_DIM
B_W = B_HEADS * HEAD_DIM
C_W = C_HEADS * HEAD_DIM
MIX_W = A_Q_W + B_W + C_W
IN_COLS = A_Q_W + 2 * A_KV_W + 3 * B_W + 3 * C_W

GRID_W = 64
AXIAL_THETA = 10000.0
NA_ROWS_MAX = 8
NA_COLS = 16
C_BRANCHES = ((128, 1), (512, 4), (2048, 16))
ROPE_THETA = 500000.0
ROPE_DIMS = HEAD_DIM // 4
Q_BLOCK = 128
N_EXPERTS = 16
EC_CAPACITY = 2
D_FF_EXPERT = 1024
D_PLE = 256
EPS = 1e-6
NEG_INF = -1e30

kernel_name = "hybrid_parallel_heads_ec_moe_encoder"


def rmsnorm(x, g):
    xf = x.astype(jnp.float32)
    y = xf * lax.rsqrt(jnp.mean(xf * xf, axis=-1, keepdims=True) + EPS)
    return (y * g.astype(jnp.float32)).astype(x.dtype)


def rope_angles(pos_f, n, theta):
    inv = theta ** (-jnp.arange(0, n, 2, dtype=jnp.float32) / n)
    return pos_f[:, None] * inv[None, :]


def apply_rope(x, ang):
    half = x.shape[-1] // 2
    xf = x.astype(jnp.float32)
    c = jnp.cos(ang)[None, :, None, :]
    s = jnp.sin(ang)[None, :, None, :]
    x1, x2 = xf[..., :half], xf[..., half:]
    return jnp.concatenate([x1 * c - x2 * s, x1 * s + x2 * c], axis=-1).astype(x.dtype)


def axial_rope(x, ang_row, ang_col):
    half = x.shape[-1] // 2
    return jnp.concatenate([apply_rope(x[..., :half], ang_row),
                            apply_rope(x[..., half:], ang_col)], axis=-1)


def partial_rope(x, ang):
    return jnp.concatenate([apply_rope(x[..., :ROPE_DIMS], ang), x[..., ROPE_DIMS:]], axis=-1)


def axial_gqa(q, k, v):
    bsz, seq, hq, dh = q.shape
    hkv = k.shape[2]
    grp = hq // hkv
    nblk = seq // Q_BLOCK
    scale = dh ** -0.5
    qb = q.reshape(bsz, nblk, Q_BLOCK, hkv, grp, dh).transpose(1, 0, 3, 4, 2, 5)

    def block(qblk):
        s = jnp.einsum("bkgqd,bskd->bkgqs", qblk, k).astype(jnp.float32) * scale
        pr = jax.nn.softmax(s, axis=-1).astype(v.dtype)
        return jnp.einsum("bkgqs,bskd->bkgqd", pr, v)

    o = lax.map(block, qb)
    return o.transpose(1, 0, 4, 2, 3, 5).reshape(bsz, seq, hq * dh)


def neighbourhood_attn(q, k, v, rpb):
    bsz, seq, nh, dh = q.shape
    rows = seq // GRID_W
    wr = min(NA_ROWS_MAX, rows)
    scale = dh ** -0.5
    qg = q.reshape(bsz, rows, GRID_W, nh, dh).transpose(1, 0, 3, 2, 4)
    kg = k.reshape(bsz, rows, GRID_W, nh, dh).transpose(0, 3, 1, 2, 4)
    vg = v.reshape(bsz, rows, GRID_W, nh, dh).transpose(0, 3, 1, 2, 4)
    cols = jnp.arange(GRID_W)
    cstart = jnp.clip(cols - NA_COLS // 2, 0, GRID_W - NA_COLS)
    colidx = cstart[:, None] + jnp.arange(NA_COLS)[None, :]
    coloff = colidx - cols[:, None] + (NA_COLS - 1)

    def row_block(args):
        r, qrow = args
        rs = jnp.clip(r - wr // 2, 0, rows - wr)
        krows = lax.dynamic_slice_in_dim(kg, rs, wr, axis=2)
        vrows = lax.dynamic_slice_in_dim(vg, rs, wr, axis=2)
        kwin = krows[:, :, :, colidx, :]
        vwin = vrows[:, :, :, colidx, :]
        rowoff = rs + jnp.arange(wr) - r + (NA_ROWS_MAX - 1)
        bias = rpb[:, rowoff[:, None, None], coloff[None, :, :]]
        s = jnp.einsum("bhcd,bhrcjd->bhcrj", qrow, kwin).astype(jnp.float32) * scale
        s = s + bias.transpose(0, 2, 1, 3)[None].astype(jnp.float32)
        pr = jax.nn.softmax(s.reshape(s.shape[:3] + (wr * NA_COLS,)), axis=-1)
        pr = pr.reshape(s.shape).astype(v.dtype)
        return jnp.einsum("bhcrj,bhrcjd->bhcd", pr, vwin)

    o = lax.map(row_block, (jnp.arange(rows), qg))
    return o.transpose(1, 0, 3, 2, 4).reshape(bsz, seq, nh * dh)


def banded_attn(q, k, v, half):
    n, nh, length, dh = q.shape
    scale = dh ** -0.5
    nblk = -(-length // Q_BLOCK)
    lpad = nblk * Q_BLOCK
    span = Q_BLOCK + 2 * half
    qp = jnp.pad(q, ((0, 0), (0, 0), (0, lpad - length), (0, 0)))
    kp = jnp.pad(k, ((0, 0), (0, 0), (half, lpad - length + half), (0, 0)))
    vp = jnp.pad(v, ((0, 0), (0, 0), (half, lpad - length + half), (0, 0)))
    kidx = jnp.arange(nblk)[:, None] * Q_BLOCK + jnp.arange(span)[None, :]
    kb = kp[:, :, kidx]
    vb = vp[:, :, kidx]
    qpos = jnp.arange(nblk)[:, None] * Q_BLOCK + jnp.arange(Q_BLOCK)[None, :]
    kpos = kidx - half
    mask = ((jnp.abs(qpos[:, :, None] - kpos[:, None, :]) <= half)
            & ((kpos >= 0) & (kpos < length))[:, None, :])
    qr = qp.reshape(n, nh, nblk, Q_BLOCK, dh)
    s = jnp.einsum("nhbqd,nhbkd->nhbqk", qr, kb).astype(jnp.float32) * scale
    s = jnp.where(mask[None, None], s, NEG_INF)
    m = jnp.max(s, axis=-1, keepdims=True)
    e = jnp.exp(s - m)
    l = jnp.sum(e, axis=-1, keepdims=True)
    o = jnp.einsum("nhbqk,nhbkd->nhbqd", (e / l).astype(v.dtype), vb)
    lse = (m + jnp.log(l))[..., 0]
    o = o.reshape(n, nh, lpad, dh)[:, :, :length]
    lse = lse.reshape(n, nh, lpad)[:, :, :length]
    return o, lse


def dilated_mixture(q, k, v):
    bsz, seq, nh, dh = q.shape
    outs, lses = [], []
    for window, dil in C_BRANCHES:
        half = window // 2 // dil
        length = seq // dil

        def to_classes(t):
            return t.reshape(bsz, length, dil, nh, dh).transpose(0, 2, 3, 1, 4).reshape(bsz * dil, nh, length, dh)

        o, lse = banded_attn(to_classes(q), to_classes(k), to_classes(v), half)
        outs.append(o.reshape(bsz, dil, nh, length, dh).transpose(0, 3, 1, 2, 4).reshape(bsz, seq, nh, dh))
        lses.append(lse.reshape(bsz, dil, nh, length).transpose(0, 3, 1, 2).reshape(bsz, seq, nh))
    w = jax.nn.softmax(jnp.stack(lses, axis=0), axis=0)
    o = jnp.sum(w[..., None] * jnp.stack(outs, axis=0).astype(jnp.float32), axis=0)
    return o.astype(q.dtype).reshape(bsz, seq, nh * dh)


def expert_choice_ffn(h, w_router, w_gate, w_up, w_down):
    bsz, seq, d = h.shape
    cap = max(1, EC_CAPACITY * seq // N_EXPERTS)
    logits = jnp.einsum("bsd,de->bse", h, w_router).astype(jnp.float32)
    aff = jax.nn.softmax(logits, axis=-1)
    gates, idx = lax.top_k(aff.transpose(0, 2, 1), cap)
    xg = jax.vmap(lambda hb, ib: hb[ib])(h, idx)
    hid = jax.nn.silu(jnp.einsum("becd,edf->becf", xg, w_gate)) * jnp.einsum("becd,edf->becf", xg, w_up)
    y = jnp.einsum("becf,efd->becd", hid, w_down) * gates[..., None].astype(h.dtype)
    return jax.vmap(lambda yb, ib: jax.ops.segment_sum(yb.reshape(-1, d), ib.reshape(-1), num_segments=seq))(y, idx)


def split_heads(t, n):
    return t.reshape(t.shape[0], t.shape[1], n, HEAD_DIM)


def setup_inputs(seed: int = 0) -> dict:
    key = jax.random.key(seed)
    ks = jax.random.split(key, 20)
    f32 = jnp.float32
    nrm = lambda k, shape, sc: jax.random.normal(k, shape, f32) * sc
    gain = lambda k, shape: 1.0 + 0.05 * jax.random.normal(k, shape, f32)
    return {
        "x": nrm(ks[0], (BATCH, SEQ, D_MODEL), 1.0),
        "p": nrm(ks[1], (DEPTH, BATCH, SEQ, D_PLE), 1.0),
        "g_mix": gain(ks[2], (DEPTH, D_MODEL)),
        "w_in": nrm(ks[3], (DEPTH, D_MODEL, IN_COLS), D_MODEL ** -0.5),
        "qk_gain": gain(ks[4], (DEPTH, 3, 2, HEAD_DIM)),
        "na_bias": nrm(ks[5], (DEPTH, B_HEADS, 2 * NA_ROWS_MAX - 1, 2 * NA_COLS - 1), 0.1),
        "g_out": gain(ks[6], (DEPTH, MIX_W)),
        "w_out": nrm(ks[7], (DEPTH, MIX_W, D_MODEL), MIX_W ** -0.5),
        "g_ffn": gain(ks[8], (DEPTH, D_MODEL)),
        "w_router": nrm(ks[9], (DEPTH, D_MODEL, N_EXPERTS), D_MODEL ** -0.5),
        "w_gate": nrm(ks[10], (DEPTH, N_EXPERTS, D_MODEL, D_FF_EXPERT), D_MODEL ** -0.5),
        "w_up": nrm(ks[11], (DEPTH, N_EXPERTS, D_MODEL, D_FF_EXPERT), D_MODEL ** -0.5),
        "w_down": nrm(ks[12], (DEPTH, N_EXPERTS, D_FF_EXPERT, D_MODEL), D_FF_EXPERT ** -0.5),
        "g_ple": gain(ks[13], (DEPTH, D_MODEL)),
        "w_ple_gate": nrm(ks[14], (DEPTH, D_MODEL, D_MODEL), D_MODEL ** -0.5),
        "w_ple_proj": nrm(ks[15], (DEPTH, D_PLE, D_MODEL), D_PLE ** -0.5),
    }


def reference(x, p, g_mix, w_in, qk_gain, na_bias, g_out, w_out, g_ffn, w_router,
              w_gate, w_up, w_down, g_ple, w_ple_gate, w_ple_proj):
    bsz, seq, _ = x.shape
    pos = jnp.arange(seq)
    ang_row = rope_angles((pos // GRID_W).astype(jnp.float32), HEAD_DIM // 2, AXIAL_THETA)
    ang_col = rope_angles((pos % GRID_W).astype(jnp.float32), HEAD_DIM // 2, AXIAL_THETA)
    ang_1d = rope_angles(pos.astype(jnp.float32), ROPE_DIMS, ROPE_THETA)
    sizes = [A_Q_W, A_KV_W, A_KV_W, B_W, B_W, B_W, C_W, C_W, C_W]
    cuts = np.cumsum(sizes)[:-1].tolist()
    out_cuts = [A_Q_W, A_Q_W + B_W]
    h = x
    for i in range(DEPTH):
        a = rmsnorm(h, g_mix[i])
        proj = jnp.einsum("bsd,dc->bsc", a, w_in[i])
        qa, ka, va, qb, kb, vb, qc, kc, vc = jnp.split(proj, cuts, axis=-1)
        qg = qk_gain[i]
        qa = axial_rope(rmsnorm(split_heads(qa, A_Q_HEADS), qg[0, 0]), ang_row, ang_col)
        ka = axial_rope(rmsnorm(split_heads(ka, A_KV_HEADS), qg[0, 1]), ang_row, ang_col)
        oa = axial_gqa(qa, ka, split_heads(va, A_KV_HEADS))
        ob = neighbourhood_attn(rmsnorm(split_heads(qb, B_HEADS), qg[1, 0]),
                                rmsnorm(split_heads(kb, B_HEADS), qg[1, 1]),
                                split_heads(vb, B_HEADS), na_bias[i])
        qc = partial_rope(rmsnorm(split_heads(qc, C_HEADS), qg[2, 0]), ang_1d)
        kc = partial_rope(rmsnorm(split_heads(kc, C_HEADS), qg[2, 1]), ang_1d)
        oc = dilated_mixture(qc, kc, split_heads(vc, C_HEADS))
        ga, gb, gc = jnp.split(g_out[i], out_cuts)
        mixed = jnp.concatenate([rmsnorm(oa, ga), rmsnorm(ob, gb), rmsnorm(oc, gc)], axis=-1)
        h = h + jnp.einsum("bsc,cd->bsd", mixed, w_out[i])
        h = h + expert_choice_ffn(rmsnorm(h, g_ffn[i]), w_router[i], w_gate[i], w_up[i], w_down[i])
        gate = jax.nn.sigmoid(jnp.einsum("bsd,de->bse", rmsnorm(h, g_ple[i]), w_ple_gate[i]).astype(jnp.float32))
        h = h + (gate * jnp.einsum("bsk,kd->bsd", p[i], w_ple_proj[i]).astype(jnp.float32)).astype(h.dtype)
    return h
```

```python
import functools

import numpy as np
import jax
import jax.numpy as jnp
from jax import lax
from jax.experimental import pallas as pl
from jax.experimental.pallas import tpu as pltpu

F32 = jnp.float32
BF16 = jnp.bfloat16
I32 = jnp.int32

D_MODEL = 1024
HEAD_DIM = 64
A_Q_HEADS = 4
A_KV_HEADS = 2
B_HEADS = 6
C_HEADS = 6
A_Q_W = A_Q_HEADS * HEAD_DIM
A_KV_W = A_KV_HEADS * HEAD_DIM
B_W = B_HEADS * HEAD_DIM
C_W = C_HEADS * HEAD_DIM
IN_COLS = A_Q_W + 2 * A_KV_W + 3 * B_W + 3 * C_W
GRID_W = 64
AXIAL_THETA = 10000.0
NA_ROWS = 8
NA_COLS = 16
C_BRANCHES = ((128, 1), (512, 4), (2048, 16))
ROPE_THETA = 500000.0
ROPE_DIMS = HEAD_DIM // 4
N_EXPERTS = 16
EC_CAPACITY = 2
EPS = 1e-6
NEG_INF = -1e30

LANES = 128
MXU_N = 256
VMEM_LIMIT = 56 * 1024 * 1024

N_CB = IN_COLS // LANES
CB_QA, CB_KA, CB_VA = 0, 2, 3
CB_QB, CB_KB, CB_VB = 4, 7, 10
CB_QC, CB_KC, CB_VC = 13, 16, 19
CB_KIND = ("A", "A", "A", "V") + ("N",) * 6 + ("V",) * 3 + ("C",) * 6 + ("V",) * 3

TM = 512
TQ_A = 256
NA_QROWS = 4
NA_KROWS = 12
TQ_C = 256
TOK_BLK = 256
N_OUT_CHUNKS = 8


def _rms(x, g):
    return x * lax.rsqrt(jnp.mean(x * x, axis=-1, keepdims=True) + EPS) * g


def _proj_kernel(h_ref, g_ref, w_ref, gain_ref, gsum_ref, ca_ref, sma_ref, spa_ref,
                 cc_ref, smc_ref, spc_ref, o_ref):
    x = h_ref[...]
    a = _rms(x, g_ref[...]).astype(BF16)
    gsum = gsum_ref[...]
    for c in range(IN_COLS // MXU_N):
        acc = jnp.dot(a, w_ref[:, c * MXU_N:(c + 1) * MXU_N], preferred_element_type=F32)
        for hf in range(MXU_N // LANES):
            cb = c * (MXU_N // LANES) + hf
            t = acc[:, hf * LANES:(hf + 1) * LANES]
            kind = CB_KIND[cb]
            if kind != "V":
                sq = t * t
                hi = sq.astype(BF16)
                lo = (sq - hi.astype(F32)).astype(BF16)
                ss = (jnp.dot(hi, gsum, preferred_element_type=F32)
                      + jnp.dot(lo, gsum, preferred_element_type=F32))
                t = t * lax.rsqrt(ss * (1.0 / HEAD_DIM) + EPS) * gain_ref[:, cb * LANES:(cb + 1) * LANES]
                if kind == "A":
                    sh = HEAD_DIM // 4
                    t = (t * ca_ref[...] + pltpu.roll(t, LANES - sh, 1) * sma_ref[...]
                         + pltpu.roll(t, sh, 1) * spa_ref[...])
                elif kind == "C":
                    sh = ROPE_DIMS // 2
                    t = (t * cc_ref[...] + pltpu.roll(t, LANES - sh, 1) * smc_ref[...]
                         + pltpu.roll(t, sh, 1) * spc_ref[...])
            o_ref[:, cb * LANES:(cb + 1) * LANES] = t.astype(BF16)


def _proj_call(h2d, g, w, gain, gsum, tabs, seq):
    t_tot = h2d.shape[0]
    nseq = seq // TM
    row = lambda i: (i, 0)
    fixed = lambda i: (0, 0)
    tab = lambda i: (i % nseq, 0)
    return pl.pallas_call(
        _proj_kernel,
        grid=(t_tot // TM,),
        in_specs=[pl.BlockSpec((TM, D_MODEL), row),
                  pl.BlockSpec((1, D_MODEL), fixed),
                  pl.BlockSpec((D_MODEL, IN_COLS), fixed),
                  pl.BlockSpec((1, IN_COLS), fixed),
                  pl.BlockSpec((LANES, LANES), fixed)]
                 + [pl.BlockSpec((TM, LANES), tab)] * 6,
        out_specs=pl.BlockSpec((TM, IN_COLS), row),
        out_shape=jax.ShapeDtypeStruct((t_tot, IN_COLS), BF16),
        compiler_params=pltpu.CompilerParams(dimension_semantics=("parallel",),
                                             vmem_limit_bytes=VMEM_LIMIT),
    )(h2d, g, w, gain, gsum, *tabs)


def _stack_heads(q):
    qf = q.astype(F32)
    lo = lax.broadcasted_iota(I32, qf.shape, 1) < HEAD_DIM
    return jnp.concatenate([jnp.where(lo, qf, 0.0), jnp.where(lo, 0.0, qf)], axis=0).astype(BF16)


def _merge_heads(o):
    n = o.shape[0] // 2
    lo = lax.broadcasted_iota(I32, (n, LANES), 1) < HEAD_DIM
    return jnp.where(lo, o[:n], o[n:])


def _scores(qs, k):
    return lax.dot_general(qs, k, (((1,), (1,)), ((), ())), preferred_element_type=F32)


def _attn_a_kernel(q_ref, k_ref, v_ref, o_ref):
    k = k_ref[0]
    v = v_ref[0]
    for blk in range(A_Q_W // LANES):
        qs = _stack_heads(q_ref[0, :, blk * LANES:(blk + 1) * LANES])
        s = _scores(qs, k)
        m = jnp.max(s, axis=-1, keepdims=True)
        p = jnp.exp(s - m)
        l = jnp.sum(p, axis=-1, keepdims=True)
        o = jnp.dot(p.astype(BF16), v, preferred_element_type=F32) / l
        o_ref[0, :, blk * LANES:(blk + 1) * LANES] = _merge_heads(o)


def _attn_a_call(qkv):
    bsz, seq, _ = qkv.shape
    return pl.pallas_call(
        _attn_a_kernel,
        grid=(bsz, seq // TQ_A),
        in_specs=[pl.BlockSpec((1, TQ_A, A_Q_W), lambda b, i: (b, i, 0)),
                  pl.BlockSpec((1, seq, LANES), lambda b, i: (b, 0, CB_KA)),
                  pl.BlockSpec((1, seq, LANES), lambda b, i: (b, 0, CB_VA))],
        out_specs=pl.BlockSpec((1, TQ_A, A_Q_W), lambda b, i: (b, i, 0)),
        out_shape=jax.ShapeDtypeStruct((bsz, seq, A_Q_W), F32),
        compiler_params=pltpu.CompilerParams(dimension_semantics=("parallel", "parallel"),
                                             vmem_limit_bytes=VMEM_LIMIT),
    )(qkv, qkv, qkv)


def _attn_b_kernel(q_ref, k_ref, v_ref, bias_ref, o_ref, *, n_rows):
    g = pl.program_id(2)
    krow0 = jnp.clip(g * NA_QROWS - NA_ROWS // 2, 0, n_rows - NA_KROWS)
    t0 = pl.multiple_of(krow0 * GRID_W, GRID_W)
    kw = k_ref[0, pl.ds(t0, NA_KROWS * GRID_W), :]
    vw = v_ref[0, pl.ds(t0, NA_KROWS * GRID_W), :]
    qs = _stack_heads(q_ref[0])
    s = _scores(qs, kw) + bias_ref[0, 0]
    m = jnp.max(s, axis=-1, keepdims=True)
    p = jnp.exp(s - m)
    l = jnp.sum(p, axis=-1, keepdims=True)
    o = jnp.dot(p.astype(BF16), vw, preferred_element_type=F32) / l
    o_ref[0] = _merge_heads(o)


def _attn_b_call(qkv, bias):
    bsz, seq, _ = qkv.shape
    n_rows = seq // GRID_W
    tq = NA_QROWS * GRID_W
    n_g = seq // tq
    cfg = lambda g: jnp.where(g == 0, 0, jnp.where(g == n_g - 1, 2, 1))
    return pl.pallas_call(
        functools.partial(_attn_b_kernel, n_rows=n_rows),
        grid=(B_W // LANES, bsz, n_g),
        in_specs=[pl.BlockSpec((1, tq, LANES), lambda p, b, g: (b, g, CB_QB + p)),
                  pl.BlockSpec((1, seq, LANES), lambda p, b, g: (b, 0, CB_KB + p)),
                  pl.BlockSpec((1, seq, LANES), lambda p, b, g: (b, 0, CB_VB + p)),
                  pl.BlockSpec((1, 1, 2 * tq, NA_KROWS * GRID_W), lambda p, b, g: (p, cfg(g), 0, 0))],
        out_specs=pl.BlockSpec((1, tq, LANES), lambda p, b, g: (b, g, p)),
        out_shape=jax.ShapeDtypeStruct((bsz, seq, B_W), F32),
        compiler_params=pltpu.CompilerParams(dimension_semantics=("parallel",) * 3,
                                             vmem_limit_bytes=VMEM_LIMIT),
    )(qkv, qkv, qkv, bias)


def _na_bias_tiles(rpb, n_rows):
    wr = NA_ROWS
    tiles = []
    for r0 in (0, NA_QROWS, n_rows - NA_QROWS):
        start = int(np.clip(r0 - wr // 2, 0, n_rows - NA_KROWS))
        r = r0 + np.arange(NA_QROWS)
        rs = np.clip(r - wr // 2, 0, n_rows - wr)
        krow = start + np.arange(NA_KROWS)
        row_ok = (krow[None, :] >= rs[:, None]) & (krow[None, :] < rs[:, None] + wr)
        rowoff = krow[None, :] - r[:, None] + (NA_ROWS - 1)
        c = np.arange(GRID_W)
        cstart = np.clip(c - NA_COLS // 2, 0, GRID_W - NA_COLS)
        kc = np.arange(GRID_W)
        col_ok = (kc[None, :] >= cstart[:, None]) & (kc[None, :] < cstart[:, None] + NA_COLS)
        coloff = kc[None, :] - c[:, None] + (NA_COLS - 1)
        ok = row_ok[:, None, :, None] & col_ok[None, :, None, :]
        ro = np.broadcast_to(np.clip(rowoff, 0, 2 * NA_ROWS - 2)[:, None, :, None], ok.shape)
        co = np.broadcast_to(np.clip(coloff, 0, 2 * NA_COLS - 2)[None, :, None, :], ok.shape)
        vals = rpb[:, ro, co]
        vals = jnp.where(ok[None], vals.astype(F32), NEG_INF)
        tiles.append(vals.reshape(B_HEADS, NA_QROWS * GRID_W, NA_KROWS * GRID_W))
    t = jnp.stack(tiles, axis=1)
    t = t.reshape(B_HEADS // 2, 2, 3, NA_QROWS * GRID_W, NA_KROWS * GRID_W).transpose(0, 2, 1, 3, 4)
    return t.reshape(B_HEADS // 2, 3, 2 * NA_QROWS * GRID_W, NA_KROWS * GRID_W)


def _attn_c_kernel(q_ref, k_ref, v_ref, o_ref, lse_ref, *, length, half):
    win = min(TQ_C + 2 * half, length)
    n_q = length // TQ_C

    def body(qi, carry):
        q0 = pl.multiple_of(qi * TQ_C, TQ_C)
        ks = pl.multiple_of(jnp.clip(q0 - half, 0, length - win), half)
        qs = _stack_heads(q_ref[0, pl.ds(q0, TQ_C), :])
        kw = k_ref[0, pl.ds(ks, win), :]
        vw = v_ref[0, pl.ds(ks, win), :]
        s = _scores(qs, kw)
        row = lax.broadcasted_iota(I32, (2 * TQ_C, win), 0)
        qpos = q0 + jnp.where(row >= TQ_C, row - TQ_C, row)
        kpos = ks + lax.broadcasted_iota(I32, (2 * TQ_C, win), 1)
        s = jnp.where(jnp.abs(qpos - kpos) <= half, s, NEG_INF)
        m = jnp.max(s, axis=-1, keepdims=True)
        p = jnp.exp(s - m)
        l = jnp.sum(p, axis=-1, keepdims=True)
        o = jnp.dot(p.astype(BF16), vw, preferred_element_type=F32) / l
        lse = jnp.broadcast_to(m + jnp.log(l), (2 * TQ_C, LANES))
        o_ref[0, pl.ds(q0, TQ_C), :] = _merge_heads(o)
        lse_ref[0, pl.ds(q0, TQ_C), :] = _merge_heads(lse)
        return carry

    lax.fori_loop(0, n_q, body, 0)


def _attn_c_call(qkv, window, dil):
    bsz, seq, _ = qkv.shape
    length = seq // dil
    half = window // 2 // dil
    view = qkv.reshape(bsz, length, dil * IN_COLS)
    n_p = C_W // LANES
    spec = lambda cb: pl.BlockSpec((1, length, LANES), lambda b, r, p: (b, 0, r * N_CB + cb + p))
    ospec = pl.BlockSpec((1, length, LANES), lambda b, r, p: (b, 0, r * n_p + p))
    oshape = jax.ShapeDtypeStruct((bsz, length, dil * C_W), F32)
    o, lse = pl.pallas_call(
        functools.partial(_attn_c_kernel, length=length, half=half),
        grid=(bsz, dil, n_p),
        in_specs=[spec(CB_QC), spec(CB_KC), spec(CB_VC)],
        out_specs=[ospec, ospec],
        out_shape=[oshape, oshape],
        compiler_params=pltpu.CompilerParams(dimension_semantics=("parallel",) * 3,
                                             vmem_limit_bytes=VMEM_LIMIT),
    )(view, view, view)
    return o.reshape(bsz * seq, C_W), lse.reshape(bsz * seq, C_W)


def _out_kernel(oa_ref, ob_ref, o1_ref, o2_ref, o3_ref, l1_ref, l2_ref, l3_ref, h_ref,
                ga_ref, gb_ref, gc_ref, w_ref, gf_ref, wrh_ref, wrl_ref,
                h1_ref, hn_ref, lt_ref):
    l1, l2, l3 = l1_ref[...], l2_ref[...], l3_ref[...]
    lm = jnp.maximum(jnp.maximum(l1, l2), l3)
    e1, e2, e3 = jnp.exp(l1 - lm), jnp.exp(l2 - lm), jnp.exp(l3 - lm)
    oc = (e1 * o1_ref[...] + e2 * o2_ref[...] + e3 * o3_ref[...]) / (e1 + e2 + e3)
    mixed = jnp.concatenate([_rms(oa_ref[...], ga_ref[...]),
                             _rms(ob_ref[...], gb_ref[...]),
                             _rms(oc, gc_ref[...])], axis=-1).astype(BF16)
    h1 = h_ref[...] + jnp.dot(mixed, w_ref[...], preferred_element_type=F32)
    h1_ref[...] = h1
    hn = _rms(h1, gf_ref[...])
    hi = hn.astype(BF16)
    hn_ref[...] = hi
    lo = (hn - hi.astype(F32)).astype(BF16)
    logits = (jnp.dot(hi, wrh_ref[...], preferred_element_type=F32)
              + jnp.dot(lo, wrh_ref[...], preferred_element_type=F32)
              + jnp.dot(hi, wrl_ref[...], preferred_element_type=F32))
    lt_ref[...] = logits.T[:N_EXPERTS, :]


def _out_call(oa, ob, ocs, lses, h2d, ga, gb, gc, w, gf, wrh, wrl):
    t_tot = h2d.shape[0]
    row = lambda i: (i, 0)
    fixed = lambda i: (0, 0)
    rs = lambda w_: pl.BlockSpec((TM, w_), row)
    fs = lambda a: pl.BlockSpec(a.shape, fixed)
    return pl.pallas_call(
        _out_kernel,
        grid=(t_tot // TM,),
        in_specs=[rs(A_Q_W), rs(B_W)] + [rs(C_W)] * 6 + [rs(D_MODEL)]
                 + [fs(ga), fs(gb), fs(gc), fs(w), fs(gf), fs(wrh), fs(wrl)],
        out_specs=[rs(D_MODEL), rs(D_MODEL), pl.BlockSpec((N_EXPERTS, TM), lambda i: (0, i))],
        out_shape=[jax.ShapeDtypeStruct((t_tot, D_MODEL), F32),
                   jax.ShapeDtypeStruct((t_tot, D_MODEL), BF16),
                   jax.ShapeDtypeStruct((N_EXPERTS, t_tot), F32)],
        compiler_params=pltpu.CompilerParams(dimension_semantics=("parallel",),
                                             vmem_limit_bytes=VMEM_LIMIT),
    )(oa, ob, *ocs, *lses, h2d, ga, gb, gc, w, gf, wrh, wrl)


def _route_kernel(lt_ref, tri_ref, blk_ref, pos_ref, gate_ref, ilo_ref, cross_ref, *, cap):
    l = lt_ref[...]
    seq = l.shape[1]
    m = jnp.max(l, axis=0, keepdims=True)
    ex = jnp.exp(l - m)
    aff = ex / jnp.sum(ex, axis=0, keepdims=True)
    gate_ref[0] = aff
    bits = lax.bitcast_convert_type(aff, I32)

    def count(mask):
        return jnp.sum(mask.astype(F32), axis=1, keepdims=True)

    def search(i, t):
        cand = t | (1 << (30 - i))
        return jnp.where(count(bits >= cand) >= cap, cand, t)

    thr = lax.fori_loop(0, 31, search, jnp.zeros((N_EXPERTS, 1), I32))
    gt = bits > thr
    eq = bits == thr
    need = cap - count(gt)

    tri = tri_ref[...]

    def excl_prefix(mask):
        mb = jnp.where(mask, 1.0, 0.0).astype(BF16)
        parts = []
        run = jnp.zeros((N_EXPERTS, 1), F32)
        for c in range(seq // LANES):
            ch = mb[:, c * LANES:(c + 1) * LANES]
            parts.append(jnp.dot(ch, tri, preferred_element_type=F32) + run)
            run = run + jnp.sum(ch.astype(F32), axis=1, keepdims=True)
        return jnp.concatenate(parts, axis=1)

    sel = gt | (eq & (excl_prefix(eq) < need))
    pos = excl_prefix(sel)
    pos_ref[0] = jnp.where(sel, pos, -1.0).astype(I32)
    sb = jnp.where(sel, 1.0, 0.0).astype(BF16)
    sc = jnp.dot(sb, blk_ref[...], preferred_element_type=F32)
    start = sc[:, :LANES]
    cnt = sc[:, LANES:]
    ilo = jnp.minimum(jnp.floor(start * (1.0 / TOK_BLK)), cap // TOK_BLK - 1.0)
    ilo_ref[0] = ilo.astype(I32)
    cross_ref[0] = jnp.where(start + cnt > (ilo + 1.0) * TOK_BLK, 1, 0).astype(I32)


def _route_call(lt, tri, blk, bsz, seq, cap):
    e = N_EXPERTS
    big = lambda dt: jax.ShapeDtypeStruct((bsz, e, seq), dt)
    small = jax.ShapeDtypeStruct((bsz, e, LANES), I32)
    bspec = pl.BlockSpec((1, e, seq), lambda b: (b, 0, 0))
    sspec = pl.BlockSpec((1, e, LANES), lambda b: (b, 0, 0))
    return pl.pallas_call(
        functools.partial(_route_kernel, cap=cap),
        grid=(bsz,),
        in_specs=[pl.BlockSpec((e, seq), lambda b: (0, b)),
                  pl.BlockSpec(tri.shape, lambda b: (0, 0)),
                  pl.BlockSpec(blk.shape, lambda b: (0, 0))],
        out_specs=[bspec, bspec, sspec, sspec],
        out_shape=[big(I32), big(F32), small, small],
        compiler_params=pltpu.CompilerParams(dimension_semantics=("parallel",),
                                             vmem_limit_bytes=VMEM_LIMIT),
    )(lt, tri, blk)


def _moe_kernel(ilo_ref, cross_ref, hn_ref, pos_ref, gate_ref, wg_ref, wu_ref, wd_ref, o_ref,
                acc_ref, xg_ref, y_ref, *, cap, n_tb):
    b = pl.program_id(0)
    k = pl.program_id(1)
    out_rows = o_ref.shape[1]

    @pl.when(k == 0)
    def _():
        acc_ref[...] = jnp.zeros_like(acc_ref)

    @pl.when(k < N_EXPERTS)
    def _():
        xg_ref[...] = jnp.zeros_like(xg_ref)
        slot = lax.broadcasted_iota(I32, (TOK_BLK, TOK_BLK), 0)

        def gather(j, carry):
            base = (b * N_EXPERTS + k) * n_tb + j
            i0 = ilo_ref[base]
            lp = pos_ref[0, 0, j]
            hb = hn_ref[0, pl.ds(pl.multiple_of(j * TOK_BLK, TOK_BLK), TOK_BLK), :]

            def add(i):
                oh = jnp.where(slot + i * TOK_BLK == lp, 1.0, 0.0).astype(BF16)
                r0 = pl.multiple_of(i * TOK_BLK, TOK_BLK)
                xg_ref[pl.ds(r0, TOK_BLK), :] += jnp.dot(oh, hb, preferred_element_type=F32)

            add(i0)

            @pl.when(cross_ref[base] == 1)
            def _():
                add(i0 + 1)

            return carry

        lax.fori_loop(0, n_tb, gather, 0)

        x = xg_ref[...].astype(BF16)
        y = jnp.zeros((cap, D_MODEL), F32)
        d_ff = wg_ref.shape[2]
        for fc in range(d_ff // MXU_N):
            cs = slice(fc * MXU_N, (fc + 1) * MXU_N)
            g = jnp.dot(x, wg_ref[0, :, cs], preferred_element_type=F32)
            u = jnp.dot(x, wu_ref[0, :, cs], preferred_element_type=F32)
            hid = (g * jax.nn.sigmoid(g) * u).astype(BF16)
            y = y + jnp.dot(hid, wd_ref[0, cs, :], preferred_element_type=F32)
        y_ref[...] = y.astype(BF16)

        def scatter(j, carry):
            base = (b * N_EXPERTS + k) * n_tb + j
            i0 = ilo_ref[base]
            lp = pos_ref[0, 0, j]
            gr = gate_ref[0, 0, j]
            t0 = pl.multiple_of(j * TOK_BLK, TOK_BLK)

            def add(i):
                oh = jnp.where(slot + i * TOK_BLK == lp, gr, 0.0).astype(BF16)
                r0 = pl.multiple_of(i * TOK_BLK, TOK_BLK)
                acc_ref[pl.ds(t0, TOK_BLK), :] += lax.dot_general(
                    oh, y_ref[pl.ds(r0, TOK_BLK), :], (((0,), (0,)), ((), ())),
                    preferred_element_type=F32)

            add(i0)

            @pl.when(cross_ref[base] == 1)
            def _():
                add(i0 + 1)

            return carry

        lax.fori_loop(0, n_tb, scatter, 0)

    @pl.when(k >= N_EXPERTS)
    def _():
        r0 = pl.multiple_of((k - N_EXPERTS) * out_rows, out_rows)
        o_ref[0] = acc_ref[pl.ds(r0, out_rows), :]


def _moe_call(ilo, cross, hn, pos, gate, wg, wu, wd, cap):
    bsz, seq, d = hn.shape
    n_tb = seq // TOK_BLK
    e_idx = lambda k: jnp.minimum(k, N_EXPERTS - 1)
    tok_spec = pl.BlockSpec((1, 1, n_tb, 1, TOK_BLK), lambda b, k, *_: (b, e_idx(k), 0, 0, 0))
    w_spec = lambda w: pl.BlockSpec((1,) + w.shape[1:], lambda b, k, *_: (e_idx(k), 0, 0))
    out_rows = seq // N_OUT_CHUNKS
    grid_spec = pltpu.PrefetchScalarGridSpec(
        num_scalar_prefetch=2,
        grid=(bsz, N_EXPERTS + N_OUT_CHUNKS),
        in_specs=[pl.BlockSpec((1, seq, d), lambda b, k, *_: (b, 0, 0), pipeline_mode=pl.Buffered(1)),
                  tok_spec, tok_spec, w_spec(wg), w_spec(wu), w_spec(wd)],
        out_specs=pl.BlockSpec((1, out_rows, d),
                               lambda b, k, *_: (b, jnp.maximum(k - N_EXPERTS, 0), 0)),
        scratch_shapes=[pltpu.VMEM((seq, d), F32), pltpu.VMEM((cap, d), F32), pltpu.VMEM((cap, d), BF16)],
    )
    return pl.pallas_call(
        functools.partial(_moe_kernel, cap=cap, n_tb=n_tb),
        grid_spec=grid_spec,
        out_shape=jax.ShapeDtypeStruct((bsz, seq, d), F32),
        compiler_params=pltpu.CompilerParams(dimension_semantics=("parallel", "arbitrary"),
                                             vmem_limit_bytes=VMEM_LIMIT),
    )(ilo, cross, hn, pos, gate, wg, wu, wd)


def _ple_kernel(h_ref, moe_ref, p_ref, g_ref, wg_ref, wp_ref, o_ref):
    h2 = h_ref[...] + moe_ref[...]
    a = _rms(h2, g_ref[...]).astype(BF16)
    gate = jax.nn.sigmoid(jnp.dot(a, wg_ref[...], preferred_element_type=F32))
    proj = jnp.dot(p_ref[...].astype(BF16), wp_ref[...], preferred_element_type=F32)
    o_ref[...] = h2 + gate * proj


def _ple_call(h1, moe, p2d, g, wg, wp):
    t_tot = h1.shape[0]
    row = lambda i: (i, 0)
    fixed = lambda i: (0, 0)
    return pl.pallas_call(
        _ple_kernel,
        grid=(t_tot // TM,),
        in_specs=[pl.BlockSpec((TM, D_MODEL), row), pl.BlockSpec((TM, D_MODEL), row),
                  pl.BlockSpec((TM, p2d.shape[1]), row), pl.BlockSpec((1, D_MODEL), fixed),
                  pl.BlockSpec(wg.shape, fixed), pl.BlockSpec(wp.shape, fixed)],
        out_specs=pl.BlockSpec((TM, D_MODEL), row),
        out_shape=jax.ShapeDtypeStruct((t_tot, D_MODEL), F32),
        compiler_params=pltpu.CompilerParams(dimension_semantics=("parallel",),
                                             vmem_limit_bytes=VMEM_LIMIT),
    )(h1, moe, p2d, g, wg, wp)


def _rope_tables(seq):
    pos = jnp.arange(seq)

    def angles(pos_f, n, theta):
        inv = theta ** (-jnp.arange(0, n, 2, dtype=F32) / n)
        return pos_f[:, None] * inv[None, :]

    ar = angles((pos // GRID_W).astype(F32), HEAD_DIM // 2, AXIAL_THETA)
    ac = angles((pos % GRID_W).astype(F32), HEAD_DIM // 2, AXIAL_THETA)
    a1 = angles(pos.astype(F32), ROPE_DIMS, ROPE_THETA)
    z16 = jnp.zeros_like(ar)
    rep = LANES // HEAD_DIM
    cos_a = jnp.tile(jnp.concatenate([jnp.cos(ar), jnp.cos(ar), jnp.cos(ac), jnp.cos(ac)], -1), (1, rep))
    sm_a = jnp.tile(jnp.concatenate([-jnp.sin(ar), z16, -jnp.sin(ac), z16], -1), (1, rep))
    sp_a = jnp.tile(jnp.concatenate([z16, jnp.sin(ar), z16, jnp.sin(ac)], -1), (1, rep))
    z8 = jnp.zeros_like(a1)
    rest = HEAD_DIM - ROPE_DIMS
    ones = jnp.ones((seq, rest), F32)
    zr = jnp.zeros((seq, rest), F32)
    cos_c = jnp.tile(jnp.concatenate([jnp.cos(a1), jnp.cos(a1), ones], -1), (1, rep))
    sm_c = jnp.tile(jnp.concatenate([-jnp.sin(a1), z8, zr], -1), (1, rep))
    sp_c = jnp.tile(jnp.concatenate([z8, jnp.sin(a1), zr], -1), (1, rep))
    return cos_a, sm_a, sp_a, cos_c, sm_c, sp_c


_A_HEAD_ORDER = (0, 2, 1, 3)
_A_PERM = np.concatenate([np.arange(h * HEAD_DIM, (h + 1) * HEAD_DIM) for h in _A_HEAD_ORDER])


def _gain_row(qg):
    scale = HEAD_DIM ** -0.5
    one = lambda n: jnp.ones((n,), F32)
    return jnp.concatenate([
        jnp.tile(qg[0, 0] * scale, A_Q_HEADS), jnp.tile(qg[0, 1], A_KV_HEADS), one(A_KV_W),
        jnp.tile(qg[1, 0] * scale, B_HEADS), jnp.tile(qg[1, 1], B_HEADS), one(B_W),
        jnp.tile(qg[2, 0] * scale, C_HEADS), jnp.tile(qg[2, 1], C_HEADS), one(C_W)])[None, :]


def kernel(x, p, g_mix, w_in, qk_gain, na_bias, g_out, w_out, g_ffn, w_router,
           w_gate, w_up, w_down, g_ple, w_ple_gate, w_ple_proj):
    bsz, seq, d = x.shape
    depth = w_in.shape[0]
    t_tot = bsz * seq
    cap = max(1, EC_CAPACITY * seq // N_EXPERTS)
    n_tb = seq // TOK_BLK

    tabs = _rope_tables(seq)
    lane = np.arange(LANES)
    gsum = jnp.asarray((lane[:, None] // HEAD_DIM) == (lane[None, :] // HEAD_DIM), BF16)
    tri = jnp.asarray(lane[:, None] < lane[None, :], BF16)
    tok = np.arange(seq)[:, None]
    j = np.arange(LANES)[None, :]
    blk_start = (tok < j * TOK_BLK) & (j < n_tb)
    blk_count = (tok // TOK_BLK == j) & (j < n_tb)
    blk = jnp.asarray(np.concatenate([blk_start, blk_count], axis=1), BF16)

    h = x.reshape(t_tot, d)
    for i in range(depth):
        w_i = jnp.concatenate([w_in[i][:, _A_PERM], w_in[i][:, A_Q_W:]], axis=1).astype(BF16)
        qkv = _proj_call(h, g_mix[i][None], w_i, _gain_row(qk_gain[i]), gsum, tabs, seq)
        qkv = qkv.reshape(bsz, seq, IN_COLS)
        oa = _attn_a_call(qkv).reshape(t_tot, A_Q_W)
        ob = _attn_b_call(qkv, _na_bias_tiles(na_bias[i], seq // GRID_W)).reshape(t_tot, B_W)
        ocs, lses = zip(*[_attn_c_call(qkv, window, dil) for window, dil in C_BRANCHES])
        go = g_out[i]
        w_o = jnp.concatenate([w_out[i][_A_PERM], w_out[i][A_Q_W:]], axis=0).astype(BF16)
        wr = jnp.pad(w_router[i], ((0, 0), (0, LANES - N_EXPERTS)))
        wr_hi = wr.astype(BF16)
        wr_lo = (wr - wr_hi.astype(F32)).astype(BF16)
        h1, hn, lt = _out_call(oa, ob, ocs, lses, h, go[:A_Q_W][_A_PERM][None], go[None, A_Q_W:A_Q_W + B_W],
                               go[None, A_Q_W + B_W:], w_o, g_ffn[i][None], wr_hi, wr_lo)
        pos, gate, ilo, cross = _route_call(lt, tri, blk, bsz, seq, cap)
        moe = _moe_call(ilo[:, :, :n_tb].reshape(-1), cross[:, :, :n_tb].reshape(-1),
                        hn.reshape(bsz, seq, d),
                        pos.reshape(bsz, N_EXPERTS, n_tb, 1, TOK_BLK),
                        gate.reshape(bsz, N_EXPERTS, n_tb, 1, TOK_BLK),
                        w_gate[i].astype(BF16), w_up[i].astype(BF16), w_down[i].astype(BF16), cap)
        h = _ple_call(h1, moe.reshape(t_tot, d), p[i].reshape(t_tot, -1), g_ple[i][None],
                      w_ple_gate[i].astype(BF16), w_ple_proj[i].astype(BF16))
    return h.reshape(bsz, seq, d)
```

```python
import functools

import numpy as np
import jax
import jax.numpy as jnp
from jax import lax
from jax.experimental import pallas as pl
from jax.experimental.pallas import tpu as pltpu

F32 = jnp.float32
BF16 = jnp.bfloat16
I32 = jnp.int32

D_MODEL = 1024
HEAD_DIM = 64
A_Q_HEADS = 4
A_KV_HEADS = 2
B_HEADS = 6
C_HEADS = 6
A_Q_W = A_Q_HEADS * HEAD_DIM
A_KV_W = A_KV_HEADS * HEAD_DIM
B_W = B_HEADS * HEAD_DIM
C_W = C_HEADS * HEAD_DIM
IN_COLS = A_Q_W + 2 * A_KV_W + 3 * B_W + 3 * C_W
GRID_W = 64
AXIAL_THETA = 10000.0
NA_ROWS = 8
NA_COLS = 16
C_BRANCHES = ((128, 1), (512, 4), (2048, 16))
ROPE_THETA = 500000.0
ROPE_DIMS = HEAD_DIM // 4
N_EXPERTS = 16
EC_CAPACITY = 2
EPS = 1e-6
NEG_INF = -1e30

LANES = 128
MXU_N = 256
VMEM_LIMIT = 56 * 1024 * 1024

N_CB = IN_COLS // LANES
CB_QA, CB_KA, CB_VA = 0, 2, 3
CB_QB, CB_KB, CB_VB = 4, 7, 10
CB_QC, CB_KC, CB_VC = 13, 16, 19
CB_KIND = ("A", "A", "A", "V") + ("N",) * 6 + ("V",) * 3 + ("C",) * 6 + ("V",) * 3
AB_COLS = CB_QC * LANES
C_COLS = 3 * C_W
N_PAIRS_C = C_W // LANES
DILS = tuple(d for _, d in C_BRANCHES)

TM = 512
TQ_A = 256
NA_QROWS = 4
NA_KROWS = 12
TQ_C = 256
TOK_BLK = 256
N_OUT_CHUNKS = 8
META_W = 32


def _rms(x, g):
    return x * lax.rsqrt(jnp.mean(x * x, axis=-1, keepdims=True) + EPS) * g


def _proj_kernel(h_ref, g_ref, w_ref, gain_ref, gsum_ref, ca_ref, sma_ref, spa_ref,
                 cc_ref, smc_ref, spc_ref, ab_ref, c1_ref, c4_ref, c16_ref, cs_ref):
    x = h_ref[...]
    a = _rms(x, g_ref[...]).astype(BF16)
    gsum = gsum_ref[...]
    for c in range(IN_COLS // MXU_N):
        acc = jnp.dot(a, w_ref[:, c * MXU_N:(c + 1) * MXU_N], preferred_element_type=F32)
        for hf in range(MXU_N // LANES):
            cb = c * (MXU_N // LANES) + hf
            t = acc[:, hf * LANES:(hf + 1) * LANES]
            kind = CB_KIND[cb]
            if kind != "V":
                sq = t * t
                hi = sq.astype(BF16)
                lo = (sq - hi.astype(F32)).astype(BF16)
                ss = (jnp.dot(hi, gsum, preferred_element_type=F32)
                      + jnp.dot(lo, gsum, preferred_element_type=F32))
                t = t * lax.rsqrt(ss * (1.0 / HEAD_DIM) + EPS) * gain_ref[:, cb * LANES:(cb + 1) * LANES]
                if kind == "A":
                    sh = HEAD_DIM // 4
                    t = (t * ca_ref[...] + pltpu.roll(t, LANES - sh, 1) * sma_ref[...]
                         + pltpu.roll(t, sh, 1) * spa_ref[...])
                elif kind == "C":
                    sh = ROPE_DIMS // 2
                    t = (t * cc_ref[...] + pltpu.roll(t, LANES - sh, 1) * smc_ref[...]
                         + pltpu.roll(t, sh, 1) * spc_ref[...])
            if cb < CB_QC:
                ab_ref[:, cb * LANES:(cb + 1) * LANES] = t.astype(BF16)
            else:
                cs_ref[cb - CB_QC] = t
    for p in range(N_PAIRS_C):
        for which in range(3):
            src = which * N_PAIRS_C + p
            for dil, ref in zip(DILS, (c1_ref, c4_ref, c16_ref)):
                for r in range(dil):
                    dst = ((p * dil + r) * 3 + which) * LANES
                    rows = pl.ds(r, TM // dil, stride=dil) if dil > 1 else slice(None)
                    ref[:, dst:dst + LANES] = cs_ref[src, rows, :].astype(BF16)


def _proj_call(h2d, g, w, gain, gsum, tabs, seq):
    t_tot = h2d.shape[0]
    nseq = seq // TM
    row = lambda i: (i, 0)
    fixed = lambda i: (0, 0)
    tab = lambda i: (i % nseq, 0)
    return pl.pallas_call(
        _proj_kernel,
        grid=(t_tot // TM,),
        in_specs=[pl.BlockSpec((TM, D_MODEL), row),
                  pl.BlockSpec((1, D_MODEL), fixed),
                  pl.BlockSpec((D_MODEL, IN_COLS), fixed),
                  pl.BlockSpec((1, IN_COLS), fixed),
                  pl.BlockSpec((LANES, LANES), fixed)]
                 + [pl.BlockSpec((TM, LANES), tab)] * 6,
        out_specs=[pl.BlockSpec((TM, AB_COLS), row)]
                  + [pl.BlockSpec((TM // dil, dil * C_COLS), row) for dil in DILS],
        out_shape=[jax.ShapeDtypeStruct((t_tot, AB_COLS), BF16)]
                  + [jax.ShapeDtypeStruct((t_tot // dil, dil * C_COLS), BF16) for dil in DILS],
        scratch_shapes=[pltpu.VMEM((C_COLS // LANES, TM, LANES), F32)],
        compiler_params=pltpu.CompilerParams(dimension_semantics=("parallel",),
                                             vmem_limit_bytes=VMEM_LIMIT),
    )(h2d, g, w, gain, gsum, *tabs)


def _stack_heads(q):
    qf = q.astype(F32)
    lo = lax.broadcasted_iota(I32, qf.shape, 1) < HEAD_DIM
    return jnp.concatenate([jnp.where(lo, qf, 0.0), jnp.where(lo, 0.0, qf)], axis=0).astype(BF16)


def _merge_heads(o):
    n = o.shape[0] // 2
    lo = lax.broadcasted_iota(I32, (n, LANES), 1) < HEAD_DIM
    return jnp.where(lo, o[:n], o[n:])


def _scores(qs, k):
    return lax.dot_general(qs, k, (((1,), (1,)), ((), ())), preferred_element_type=F32)


def _attn_a_kernel(q_ref, k_ref, v_ref, o_ref):
    k = k_ref[0]
    v = v_ref[0]
    for blk in range(A_Q_W // LANES):
        qs = _stack_heads(q_ref[0, :, blk * LANES:(blk + 1) * LANES])
        s = _scores(qs, k)
        m = jnp.max(s, axis=-1, keepdims=True)
        p = jnp.exp(s - m)
        l = jnp.sum(p, axis=-1, keepdims=True)
        o = jnp.dot(p.astype(BF16), v, preferred_element_type=F32) / l
        o_ref[0, :, blk * LANES:(blk + 1) * LANES] = _merge_heads(o)


def _attn_a_call(qkv):
    bsz, seq, _ = qkv.shape
    return pl.pallas_call(
        _attn_a_kernel,
        grid=(bsz, seq // TQ_A),
        in_specs=[pl.BlockSpec((1, TQ_A, A_Q_W), lambda b, i: (b, i, 0)),
                  pl.BlockSpec((1, seq, LANES), lambda b, i: (b, 0, CB_KA)),
                  pl.BlockSpec((1, seq, LANES), lambda b, i: (b, 0, CB_VA))],
        out_specs=pl.BlockSpec((1, TQ_A, A_Q_W), lambda b, i: (b, i, 0)),
        out_shape=jax.ShapeDtypeStruct((bsz, seq, A_Q_W), F32),
        compiler_params=pltpu.CompilerParams(dimension_semantics=("parallel", "parallel"),
                                             vmem_limit_bytes=VMEM_LIMIT),
    )(qkv, qkv, qkv)


def _attn_b_kernel(q_ref, k_ref, v_ref, bias_ref, o_ref, *, n_rows):
    g = pl.program_id(2)
    krow0 = jnp.clip(g * NA_QROWS - NA_ROWS // 2, 0, n_rows - NA_KROWS)
    t0 = pl.multiple_of(krow0 * GRID_W, GRID_W)
    kw = k_ref[0, pl.ds(t0, NA_KROWS * GRID_W), :]
    vw = v_ref[0, pl.ds(t0, NA_KROWS * GRID_W), :]
    qs = _stack_heads(q_ref[0])
    s = _scores(qs, kw) + bias_ref[0, 0]
    m = jnp.max(s, axis=-1, keepdims=True)
    p = jnp.exp(s - m)
    l = jnp.sum(p, axis=-1, keepdims=True)
    o = jnp.dot(p.astype(BF16), vw, preferred_element_type=F32) / l
    o_ref[0] = _merge_heads(o)


def _attn_b_call(qkv, bias):
    bsz, seq, _ = qkv.shape
    n_rows = seq // GRID_W
    tq = NA_QROWS * GRID_W
    n_g = seq // tq
    cfg = lambda g: jnp.where(g == 0, 0, jnp.where(g == n_g - 1, 2, 1))
    return pl.pallas_call(
        functools.partial(_attn_b_kernel, n_rows=n_rows),
        grid=(B_W // LANES, bsz, n_g),
        in_specs=[pl.BlockSpec((1, tq, LANES), lambda p, b, g: (b, g, CB_QB + p)),
                  pl.BlockSpec((1, seq, LANES), lambda p, b, g: (b, 0, CB_KB + p)),
                  pl.BlockSpec((1, seq, LANES), lambda p, b, g: (b, 0, CB_VB + p)),
                  pl.BlockSpec((1, 1, 2 * tq, NA_KROWS * GRID_W), lambda p, b, g: (p, cfg(g), 0, 0))],
        out_specs=pl.BlockSpec((1, tq, LANES), lambda p, b, g: (b, g, p)),
        out_shape=jax.ShapeDtypeStruct((bsz, seq, B_W), F32),
        compiler_params=pltpu.CompilerParams(dimension_semantics=("parallel",) * 3,
                                             vmem_limit_bytes=VMEM_LIMIT),
    )(qkv, qkv, qkv, bias)


def _na_bias_tiles(rpb, n_rows):
    wr = NA_ROWS
    n_ro, n_co = 2 * NA_ROWS - 1, 2 * NA_COLS - 1
    row_sel = np.zeros((3, NA_QROWS, NA_KROWS, n_ro), np.float32)
    for z, r0 in enumerate((0, NA_QROWS, n_rows - NA_QROWS)):
        start = int(np.clip(r0 - wr // 2, 0, n_rows - NA_KROWS))
        for qi in range(NA_QROWS):
            r = r0 + qi
            rs = int(np.clip(r - wr // 2, 0, n_rows - wr))
            for kj in range(NA_KROWS):
                krow = start + kj
                if rs <= krow < rs + wr:
                    row_sel[z, qi, kj, krow - r + NA_ROWS - 1] = 1.0
    col_sel = np.zeros((GRID_W, GRID_W, n_co), np.float32)
    for c in range(GRID_W):
        cstart = int(np.clip(c - NA_COLS // 2, 0, GRID_W - NA_COLS))
        for kc in range(cstart, cstart + NA_COLS):
            col_sel[c, kc, kc - c + NA_COLS - 1] = 1.0
    ok = (row_sel.sum(-1)[:, :, None, :, None] * col_sel.sum(-1)[None, None, :, None, :]) > 0
    vals = jnp.einsum("zqka,lhab,cjb->lhzqckj", row_sel, rpb.astype(F32), col_sel,
                      precision=lax.Precision.HIGHEST)
    vals = jnp.where(ok[None, None], vals, NEG_INF)
    nq, nk = NA_QROWS * GRID_W, NA_KROWS * GRID_W
    t = vals.reshape(rpb.shape[0], B_HEADS // 2, 2, 3, nq, nk).transpose(0, 1, 3, 2, 4, 5)
    return t.reshape(rpb.shape[0], B_HEADS // 2, 3, 2 * nq, nk)


def _band_tile(q, kw, vw, q0, ks, half):
    win = kw.shape[0]
    s = _scores(_stack_heads(q), kw)
    row = lax.broadcasted_iota(I32, (2 * TQ_C, win), 0)
    qpos = q0 + jnp.where(row >= TQ_C, row - TQ_C, row)
    kpos = ks + lax.broadcasted_iota(I32, (2 * TQ_C, win), 1)
    s = jnp.where(jnp.abs(qpos - kpos) <= half, s, NEG_INF)
    m = jnp.max(s, axis=-1, keepdims=True)
    p = jnp.exp(s - m)
    l = jnp.sum(p, axis=-1, keepdims=True)
    o = jnp.dot(p.astype(BF16), vw, preferred_element_type=F32) / l
    lse = jnp.broadcast_to(m + jnp.log(l), (2 * TQ_C, LANES))
    return _merge_heads(o), _merge_heads(lse)


def _attn_c_kernel(c1_ref, c4_ref, c16_ref, o_ref, o2_ref, l2_ref, o3_ref, l3_ref, *, seq):
    def branch(ref, window, dil, r, emit):
        length = seq // dil
        half = window // 2 // dil
        win = min(TQ_C + 2 * half, length)
        base = r * 3 * LANES

        def body(qi, carry):
            q0 = pl.multiple_of(qi * TQ_C, TQ_C)
            ks = pl.multiple_of(jnp.clip(q0 - half, 0, length - win), half)
            o, lse = _band_tile(ref[0, pl.ds(q0, TQ_C), base:base + LANES],
                                ref[0, pl.ds(ks, win), base + LANES:base + 2 * LANES],
                                ref[0, pl.ds(ks, win), base + 2 * LANES:base + 3 * LANES],
                                q0, ks, half)
            emit(q0, o, lse)
            return carry

        lax.fori_loop(0, length // TQ_C, body, 0)

    for ref, (window, dil), o_scr, l_scr in ((c4_ref, C_BRANCHES[1], o2_ref, l2_ref),
                                            (c16_ref, C_BRANCHES[2], o3_ref, l3_ref)):
        for r in range(dil):
            def emit(q0, o, lse, r=r, dil=dil, o_scr=o_scr, l_scr=l_scr):
                rows = pl.ds(r + dil * q0, TQ_C, stride=dil)
                o_scr[rows, :] = o
                l_scr[rows, :] = lse
            branch(ref, window, dil, r, emit)

    def emit1(q0, o1, l1):
        rows = pl.ds(q0, TQ_C)
        l2, l3 = l2_ref[rows, :], l3_ref[rows, :]
        lm = jnp.maximum(jnp.maximum(l1, l2), l3)
        e1, e2, e3 = jnp.exp(l1 - lm), jnp.exp(l2 - lm), jnp.exp(l3 - lm)
        o_ref[0, rows, :] = (e1 * o1 + e2 * o2_ref[rows, :] + e3 * o3_ref[rows, :]) / (e1 + e2 + e3)

    branch(c1_ref, C_BRANCHES[0][0], 1, 0, emit1)


def _attn_c_call(c1, c4, c16, bsz, seq):
    views = [c.reshape(bsz, seq // dil, dil * C_COLS) for c, dil in zip((c1, c4, c16), DILS)]
    spec = lambda dil: pl.BlockSpec((1, seq // dil, dil * 3 * LANES), lambda b, p: (b, 0, p))
    return pl.pallas_call(
        functools.partial(_attn_c_kernel, seq=seq),
        grid=(bsz, N_PAIRS_C),
        in_specs=[spec(dil) for dil in DILS],
        out_specs=pl.BlockSpec((1, seq, LANES), lambda b, p: (b, 0, p)),
        out_shape=jax.ShapeDtypeStruct((bsz, seq, C_W), F32),
        scratch_shapes=[pltpu.VMEM((seq, LANES), F32)] * 4,
        compiler_params=pltpu.CompilerParams(dimension_semantics=("parallel",) * 2,
                                             vmem_limit_bytes=VMEM_LIMIT),
    )(*views)


def _out_kernel(oa_ref, ob_ref, oc_ref, h_ref, ga_ref, gb_ref, gc_ref, w_ref, gf_ref, wrh_ref, wrl_ref,
                h1_ref, hn_ref, lt_ref):
    mixed = jnp.concatenate([_rms(oa_ref[...], ga_ref[...]),
                             _rms(ob_ref[...], gb_ref[...]),
                             _rms(oc_ref[...], gc_ref[...])], axis=-1).astype(BF16)
    h1 = h_ref[...] + jnp.dot(mixed, w_ref[...], preferred_element_type=F32)
    h1_ref[...] = h1
    hn = _rms(h1, gf_ref[...])
    hi = hn.astype(BF16)
    hn_ref[...] = hi
    lo = (hn - hi.astype(F32)).astype(BF16)
    logits = (jnp.dot(hi, wrh_ref[...], preferred_element_type=F32)
              + jnp.dot(lo, wrh_ref[...], preferred_element_type=F32)
              + jnp.dot(hi, wrl_ref[...], preferred_element_type=F32))
    lt_ref[...] = logits.T[:N_EXPERTS, :]


def _out_call(oa, ob, oc, h2d, ga, gb, gc, w, gf, wrh, wrl):
    t_tot = h2d.shape[0]
    row = lambda i: (i, 0)
    fixed = lambda i: (0, 0)
    rs = lambda w_: pl.BlockSpec((TM, w_), row)
    fs = lambda a: pl.BlockSpec(a.shape, fixed)
    return pl.pallas_call(
        _out_kernel,
        grid=(t_tot // TM,),
        in_specs=[rs(A_Q_W), rs(B_W), rs(C_W), rs(D_MODEL)]
                 + [fs(ga), fs(gb), fs(gc), fs(w), fs(gf), fs(wrh), fs(wrl)],
        out_specs=[rs(D_MODEL), rs(D_MODEL), pl.BlockSpec((N_EXPERTS, TM), lambda i: (0, i))],
        out_shape=[jax.ShapeDtypeStruct((t_tot, D_MODEL), F32),
                   jax.ShapeDtypeStruct((t_tot, D_MODEL), BF16),
                   jax.ShapeDtypeStruct((N_EXPERTS, t_tot), F32)],
        compiler_params=pltpu.CompilerParams(dimension_semantics=("parallel",),
                                             vmem_limit_bytes=VMEM_LIMIT),
    )(oa, ob, oc, h2d, ga, gb, gc, w, gf, wrh, wrl)


def _route_kernel(lt_ref, tri_ref, blk_ref, pos_ref, gate_ref, meta_ref, *, cap):
    l = lt_ref[...]
    seq = l.shape[1]
    m = jnp.max(l, axis=0, keepdims=True)
    ex = jnp.exp(l - m)
    aff = ex / jnp.sum(ex, axis=0, keepdims=True)
    gate_ref[0] = aff
    bits = lax.bitcast_convert_type(aff, I32)

    def count(mask):
        return jnp.sum(mask.astype(F32), axis=1, keepdims=True)

    def search(i, t):
        cand = t | (1 << (30 - i))
        return jnp.where(count(bits >= cand) >= cap, cand, t)

    thr = lax.fori_loop(0, 31, search, jnp.zeros((N_EXPERTS, 1), I32))
    gt = bits > thr
    eq = bits == thr
    need = cap - count(gt)

    tri = tri_ref[...]

    def excl_prefix(mask):
        mb = jnp.where(mask, 1.0, 0.0).astype(BF16)
        parts = []
        run = jnp.zeros((N_EXPERTS, 1), F32)
        for c in range(seq // LANES):
            ch = mb[:, c * LANES:(c + 1) * LANES]
            parts.append(jnp.dot(ch, tri, preferred_element_type=F32) + run)
            run = run + jnp.sum(ch.astype(F32), axis=1, keepdims=True)
        return jnp.concatenate(parts, axis=1)

    sel = gt | (eq & (excl_prefix(eq) < need))
    pos = excl_prefix(sel)
    pos_ref[0] = jnp.where(sel, pos, -1.0).astype(I32)
    sb = jnp.where(sel, 1.0, 0.0).astype(BF16)
    sc = jnp.dot(sb, blk_ref[...], preferred_element_type=F32)
    start = sc[:, :LANES]
    cnt = sc[:, LANES:]
    ilo = jnp.minimum(jnp.floor(start * (1.0 / TOK_BLK)), cap // TOK_BLK - 1.0)
    cross = jnp.where(start + cnt > (ilo + 1.0) * TOK_BLK, 1.0, 0.0)
    lane = lax.broadcasted_iota(I32, cross.shape, 1)
    has = jnp.sum(cross, axis=1, keepdims=True)
    jstar = jnp.sum(cross * lane.astype(F32), axis=1, keepdims=True)
    ifix = cap // TOK_BLK - has
    n_tb = seq // TOK_BLK
    meta = jnp.where(lane < n_tb, ilo, jnp.where(lane == n_tb, jstar, jnp.where(lane == n_tb + 1, ifix, 0.0)))
    meta_ref[0] = meta.astype(I32)


def _route_call(lt, tri, blk, bsz, seq, cap):
    e = N_EXPERTS
    big = lambda dt: jax.ShapeDtypeStruct((bsz, e, seq), dt)
    small = jax.ShapeDtypeStruct((bsz, e, LANES), I32)
    bspec = pl.BlockSpec((1, e, seq), lambda b: (b, 0, 0))
    sspec = pl.BlockSpec((1, e, LANES), lambda b: (b, 0, 0))
    return pl.pallas_call(
        functools.partial(_route_kernel, cap=cap),
        grid=(bsz,),
        in_specs=[pl.BlockSpec((e, seq), lambda b: (0, b)),
                  pl.BlockSpec(tri.shape, lambda b: (0, 0)),
                  pl.BlockSpec(blk.shape, lambda b: (0, 0))],
        out_specs=[bspec, bspec, sspec],
        out_shape=[big(I32), big(F32), small],
        compiler_params=pltpu.CompilerParams(dimension_semantics=("parallel",),
                                             vmem_limit_bytes=VMEM_LIMIT),
    )(lt, tri, blk)


def _moe_kernel(meta_ref, hn_ref, pos_ref, gate_ref, wg_ref, wu_ref, wd_ref, o_ref,
                acc_ref, xg_ref, y_ref, *, cap, n_tb):
    b = pl.program_id(0)
    k = pl.program_id(1)
    out_rows = o_ref.shape[1]

    @pl.when(k == 0)
    def _():
        acc_ref[...] = jnp.zeros_like(acc_ref)
        y_ref[cap:, :] = jnp.zeros((TOK_BLK, y_ref.shape[1]), BF16)

    @pl.when(k < N_EXPERTS)
    def _():
        xg_ref[...] = jnp.zeros_like(xg_ref)
        slot = lax.broadcasted_iota(I32, (TOK_BLK, TOK_BLK), 0)
        mbase = (b * N_EXPERTS + k) * META_W
        jstar = meta_ref[mbase + n_tb]
        ifix = meta_ref[mbase + n_tb + 1]

        def gather(j, i):
            lp = pos_ref[0, 0, j]
            hb = hn_ref[0, pl.ds(pl.multiple_of(j * TOK_BLK, TOK_BLK), TOK_BLK), :]
            oh = jnp.where(slot + i * TOK_BLK == lp, 1.0, 0.0).astype(BF16)
            r0 = pl.multiple_of(i * TOK_BLK, TOK_BLK)
            xg_ref[pl.ds(r0, TOK_BLK), :] += jnp.dot(oh, hb, preferred_element_type=F32)

        def gather_body(j, carry):
            gather(j, meta_ref[mbase + j])
            return carry

        lax.fori_loop(0, n_tb, gather_body, 0, unroll=2)
        gather(jstar, ifix)

        x = xg_ref[:cap, :].astype(BF16)
        y = jnp.zeros((cap, D_MODEL), F32)
        d_ff = wg_ref.shape[2]
        for fc in range(d_ff // MXU_N):
            cs = slice(fc * MXU_N, (fc + 1) * MXU_N)
            g = jnp.dot(x, wg_ref[0, :, cs], preferred_element_type=F32)
            u = jnp.dot(x, wu_ref[0, :, cs], preferred_element_type=F32)
            hid = (g * jax.nn.sigmoid(g) * u).astype(BF16)
            y = y + jnp.dot(hid, wd_ref[0, cs, :], preferred_element_type=F32)
        y_ref[:cap, :] = y.astype(BF16)

        def scatter(j, i):
            lp = pos_ref[0, 0, j]
            gr = gate_ref[0, 0, j]
            t0 = pl.multiple_of(j * TOK_BLK, TOK_BLK)
            oh = jnp.where(slot + i * TOK_BLK == lp, gr, 0.0).astype(BF16)
            r0 = pl.multiple_of(i * TOK_BLK, TOK_BLK)
            acc_ref[pl.ds(t0, TOK_BLK), :] += lax.dot_general(
                oh, y_ref[pl.ds(r0, TOK_BLK), :], (((0,), (0,)), ((), ())),
                preferred_element_type=F32)

        def scatter_body(j, carry):
            scatter(j, meta_ref[mbase + j])
            return carry

        lax.fori_loop(0, n_tb, scatter_body, 0, unroll=2)
        scatter(jstar, ifix)

    @pl.when(k >= N_EXPERTS)
    def _():
        r0 = pl.multiple_of((k - N_EXPERTS) * out_rows, out_rows)
        o_ref[0] = acc_ref[pl.ds(r0, out_rows), :]


def _moe_call(meta, hn, pos, gate, wg, wu, wd, cap):
    bsz, seq, d = hn.shape
    n_tb = seq // TOK_BLK
    e_idx = lambda k: jnp.minimum(k, N_EXPERTS - 1)
    tok_spec = pl.BlockSpec((1, 1, n_tb, 1, TOK_BLK), lambda b, k, *_: (b, e_idx(k), 0, 0, 0))
    w_spec = lambda w: pl.BlockSpec((1,) + w.shape[1:], lambda b, k, *_: (e_idx(k), 0, 0))
    out_rows = seq // N_OUT_CHUNKS
    grid_spec = pltpu.PrefetchScalarGridSpec(
        num_scalar_prefetch=1,
        grid=(bsz, N_EXPERTS + N_OUT_CHUNKS),
        in_specs=[pl.BlockSpec((1, seq, d), lambda b, k, *_: (b, 0, 0), pipeline_mode=pl.Buffered(1)),
                  tok_spec, tok_spec, w_spec(wg), w_spec(wu), w_spec(wd)],
        out_specs=pl.BlockSpec((1, out_rows, d),
                               lambda b, k, *_: (b, jnp.maximum(k - N_EXPERTS, 0), 0)),
        scratch_shapes=[pltpu.VMEM((seq, d), F32), pltpu.VMEM((cap + TOK_BLK, d), F32),
                        pltpu.VMEM((cap + TOK_BLK, d), BF16)],
    )
    return pl.pallas_call(
        functools.partial(_moe_kernel, cap=cap, n_tb=n_tb),
        grid_spec=grid_spec,
        out_shape=jax.ShapeDtypeStruct((bsz, seq, d), F32),
        compiler_params=pltpu.CompilerParams(dimension_semantics=("parallel", "arbitrary"),
                                             vmem_limit_bytes=VMEM_LIMIT),
    )(meta, hn, pos, gate, wg, wu, wd)


def _ple_kernel(h_ref, moe_ref, p_ref, g_ref, wg_ref, wp_ref, o_ref):
    h2 = h_ref[...] + moe_ref[...]
    a = _rms(h2, g_ref[...]).astype(BF16)
    gate = jax.nn.sigmoid(jnp.dot(a, wg_ref[...], preferred_element_type=F32))
    proj = jnp.dot(p_ref[...].astype(BF16), wp_ref[...], preferred_element_type=F32)
    o_ref[...] = h2 + gate * proj


def _ple_call(h1, moe, p2d, g, wg, wp):
    t_tot = h1.shape[0]
    row = lambda i: (i, 0)
    fixed = lambda i: (0, 0)
    return pl.pallas_call(
        _ple_kernel,
        grid=(t_tot // TM,),
        in_specs=[pl.BlockSpec((TM, D_MODEL), row), pl.BlockSpec((TM, D_MODEL), row),
                  pl.BlockSpec((TM, p2d.shape[1]), row), pl.BlockSpec((1, D_MODEL), fixed),
                  pl.BlockSpec(wg.shape, fixed), pl.BlockSpec(wp.shape, fixed)],
        out_specs=pl.BlockSpec((TM, D_MODEL), row),
        out_shape=jax.ShapeDtypeStruct((t_tot, D_MODEL), F32),
        compiler_params=pltpu.CompilerParams(dimension_semantics=("parallel",),
                                             vmem_limit_bytes=VMEM_LIMIT),
    )(h1, moe, p2d, g, wg, wp)


def _rope_tables(seq):
    pos = jnp.arange(seq)

    def angles(pos_f, n, theta):
        inv = theta ** (-jnp.arange(0, n, 2, dtype=F32) / n)
        return pos_f[:, None] * inv[None, :]

    ar = angles((pos // GRID_W).astype(F32), HEAD_DIM // 2, AXIAL_THETA)
    ac = angles((pos % GRID_W).astype(F32), HEAD_DIM // 2, AXIAL_THETA)
    a1 = angles(pos.astype(F32), ROPE_DIMS, ROPE_THETA)
    z16 = jnp.zeros_like(ar)
    rep = LANES // HEAD_DIM
    cos_a = jnp.tile(jnp.concatenate([jnp.cos(ar), jnp.cos(ar), jnp.cos(ac), jnp.cos(ac)], -1), (1, rep))
    sm_a = jnp.tile(jnp.concatenate([-jnp.sin(ar), z16, -jnp.sin(ac), z16], -1), (1, rep))
    sp_a = jnp.tile(jnp.concatenate([z16, jnp.sin(ar), z16, jnp.sin(ac)], -1), (1, rep))
    z8 = jnp.zeros_like(a1)
    rest = HEAD_DIM - ROPE_DIMS
    ones = jnp.ones((seq, rest), F32)
    zr = jnp.zeros((seq, rest), F32)
    cos_c = jnp.tile(jnp.concatenate([jnp.cos(a1), jnp.cos(a1), ones], -1), (1, rep))
    sm_c = jnp.tile(jnp.concatenate([-jnp.sin(a1), z8, zr], -1), (1, rep))
    sp_c = jnp.tile(jnp.concatenate([z8, jnp.sin(a1), zr], -1), (1, rep))
    return cos_a, sm_a, sp_a, cos_c, sm_c, sp_c


_A_HEAD_ORDER = (0, 2, 1, 3)
_A_PERM = np.concatenate([np.arange(h * HEAD_DIM, (h + 1) * HEAD_DIM) for h in _A_HEAD_ORDER])


def _gain_row(qg):
    scale = HEAD_DIM ** -0.5
    one = lambda n: jnp.ones((n,), F32)
    return jnp.concatenate([
        jnp.tile(qg[0, 0] * scale, A_Q_HEADS), jnp.tile(qg[0, 1], A_KV_HEADS), one(A_KV_W),
        jnp.tile(qg[1, 0] * scale, B_HEADS), jnp.tile(qg[1, 1], B_HEADS), one(B_W),
        jnp.tile(qg[2, 0] * scale, C_HEADS), jnp.tile(qg[2, 1], C_HEADS), one(C_W)])[None, :]


def kernel(x, p, g_mix, w_in, qk_gain, na_bias, g_out, w_out, g_ffn, w_router,
           w_gate, w_up, w_down, g_ple, w_ple_gate, w_ple_proj):
    bsz, seq, d = x.shape
    depth = w_in.shape[0]
    t_tot = bsz * seq
    cap = max(1, EC_CAPACITY * seq // N_EXPERTS)
    n_tb = seq // TOK_BLK

    tabs = _rope_tables(seq)
    lane = np.arange(LANES)
    gsum = jnp.asarray((lane[:, None] // HEAD_DIM) == (lane[None, :] // HEAD_DIM), BF16)
    tri = jnp.asarray(lane[:, None] < lane[None, :], BF16)
    tok = np.arange(seq)[:, None]
    j = np.arange(LANES)[None, :]
    blk_start = (tok < j * TOK_BLK) & (j < n_tb)
    blk_count = (tok // TOK_BLK == j) & (j < n_tb)
    blk = jnp.asarray(np.concatenate([blk_start, blk_count], axis=1), BF16)

    bias = _na_bias_tiles(na_bias, seq // GRID_W)
    h = x.reshape(t_tot, d)
    for i in range(depth):
        w_i = jnp.concatenate([w_in[i][:, _A_PERM], w_in[i][:, A_Q_W:]], axis=1).astype(BF16)
        ab, c1, c4, c16 = _proj_call(h, g_mix[i][None], w_i, _gain_row(qk_gain[i]), gsum, tabs, seq)
        ab = ab.reshape(bsz, seq, AB_COLS)
        oa = _attn_a_call(ab).reshape(t_tot, A_Q_W)
        ob = _attn_b_call(ab, bias[i]).reshape(t_tot, B_W)
        oc = _attn_c_call(c1, c4, c16, bsz, seq).reshape(t_tot, C_W)
        go = g_out[i]
        w_o = jnp.concatenate([w_out[i][_A_PERM], w_out[i][A_Q_W:]], axis=0).astype(BF16)
        wr = jnp.pad(w_router[i], ((0, 0), (0, LANES - N_EXPERTS)))
        wr_hi = wr.astype(BF16)
        wr_lo = (wr - wr_hi.astype(F32)).astype(BF16)
        h1, hn, lt = _out_call(oa, ob, oc, h, go[:A_Q_W][_A_PERM][None], go[None, A_Q_W:A_Q_W + B_W],
                               go[None, A_Q_W + B_W:], w_o, g_ffn[i][None], wr_hi, wr_lo)
        pos, gate, meta = _route_call(lt, tri, blk, bsz, seq, cap)
        moe = _moe_call(meta[:, :, :META_W].reshape(-1),
                        hn.reshape(bsz, seq, d),
                        pos.reshape(bsz, N_EXPERTS, n_tb, 1, TOK_BLK),
                        gate.reshape(bsz, N_EXPERTS, n_tb, 1, TOK_BLK),
                        w_gate[i].astype(BF16), w_up[i].astype(BF16), w_down[i].astype(BF16), cap)
        h = _ple_call(h1, moe.reshape(t_tot, d), p[i].reshape(t_tot, -1), g_ple[i][None],
                      w_ple_gate[i].astype(BF16), w_ple_proj[i].astype(BF16))
    return h.reshape(bsz, seq, d)
```

```python
import functools

import numpy as np
import jax
import jax.numpy as jnp
from jax import lax
from jax.experimental import pallas as pl
from jax.experimental.pallas import tpu as pltpu

F32 = jnp.float32
BF16 = jnp.bfloat16
I32 = jnp.int32

D_MODEL = 1024
HEAD_DIM = 64
A_Q_HEADS = 4
A_KV_HEADS = 2
B_HEADS = 6
C_HEADS = 6
A_Q_W = A_Q_HEADS * HEAD_DIM
A_KV_W = A_KV_HEADS * HEAD_DIM
B_W = B_HEADS * HEAD_DIM
C_W = C_HEADS * HEAD_DIM
IN_COLS = A_Q_W + 2 * A_KV_W + 3 * B_W + 3 * C_W
GRID_W = 64
AXIAL_THETA = 10000.0
NA_ROWS = 8
NA_COLS = 16
C_BRANCHES = ((128, 1), (512, 4), (2048, 16))
ROPE_THETA = 500000.0
ROPE_DIMS = HEAD_DIM // 4
N_EXPERTS = 16
EC_CAPACITY = 2
EPS = 1e-6
NEG_INF = -1e30

LANES = 128
MXU_N = 256
VMEM_LIMIT = 56 * 1024 * 1024

N_CB = IN_COLS // LANES
CB_QA, CB_KA, CB_VA = 0, 2, 3
CB_QB, CB_KB, CB_VB = 4, 7, 10
CB_QC, CB_KC, CB_VC = 13, 16, 19
CB_KIND = ("A", "A", "A", "V") + ("N",) * 6 + ("V",) * 3 + ("C",) * 6 + ("V",) * 3
AB_COLS = CB_QC * LANES
C_COLS = 3 * C_W
N_PAIRS_C = C_W // LANES
DILS = tuple(d for _, d in C_BRANCHES)

TM = 512
TQ_A = 512
TQ_A_CHAIN = 128
NA_QROWS = 4
NA_KROWS = 12
TQ_C = 256
TOK_BLK = 256
N_OUT_CHUNKS = 8
META_W = 32


def _rms(x, g):
    return x * lax.rsqrt(jnp.mean(x * x, axis=-1, keepdims=True) + EPS) * g


def _proj_kernel(h_ref, g_ref, w_ref, gain_ref, gsum_ref, ca_ref, sma_ref, spa_ref,
                 cc_ref, smc_ref, spc_ref, ab_ref, c1_ref, c4_ref, c16_ref, cs_ref):
    x = h_ref[...]
    a = _rms(x, g_ref[...]).astype(BF16)
    gsum = gsum_ref[...]
    per = MXU_N // LANES
    n_c = IN_COLS // MXU_N
    accs = [jnp.dot(a, w_ref[:, c * MXU_N:(c + 1) * MXU_N], preferred_element_type=F32) for c in range(n_c)]
    normed = [c for c in range(n_c) if any(CB_KIND[cb] != "V" for cb in range(c * per, (c + 1) * per))]
    sq = jnp.concatenate([accs[c] * accs[c] for c in normed], axis=0)
    hi = sq.astype(BF16)
    lo = (sq - hi.astype(F32)).astype(BF16)
    ss = jnp.dot(hi, gsum, preferred_element_type=F32) + jnp.dot(lo, gsum, preferred_element_type=F32)
    inv_all = lax.rsqrt(ss * (1.0 / HEAD_DIM) + EPS)
    for c in range(n_c):
        acc = accs[c]
        if c in normed:
            inv = inv_all[normed.index(c) * TM:(normed.index(c) + 1) * TM]
        for hf, cb in enumerate(range(c * per, (c + 1) * per)):
            t = acc[:, hf * LANES:(hf + 1) * LANES]
            kind = CB_KIND[cb]
            if kind != "V":
                t = t * inv[:, hf * LANES:(hf + 1) * LANES] * gain_ref[:, cb * LANES:(cb + 1) * LANES]
                if kind == "A":
                    sh = HEAD_DIM // 4
                    t = (t * ca_ref[...] + pltpu.roll(t, LANES - sh, 1) * sma_ref[...]
                         + pltpu.roll(t, sh, 1) * spa_ref[...])
                elif kind == "C":
                    sh = ROPE_DIMS // 2
                    t = (t * cc_ref[...] + pltpu.roll(t, LANES - sh, 1) * smc_ref[...]
                         + pltpu.roll(t, sh, 1) * spc_ref[...])
            if cb < CB_QC:
                ab_ref[:, cb * LANES:(cb + 1) * LANES] = t.astype(BF16)
            else:
                cs_ref[cb - CB_QC] = t
    for p in range(N_PAIRS_C):
        for which in range(3):
            src = which * N_PAIRS_C + p
            for dil, ref in zip(DILS, (c1_ref, c4_ref, c16_ref)):
                for r in range(dil):
                    dst = ((p * dil + r) * 3 + which) * LANES
                    rows = pl.ds(r, TM // dil, stride=dil) if dil > 1 else slice(None)
                    ref[:, dst:dst + LANES] = cs_ref[src, rows, :].astype(BF16)


def _proj_call(h2d, g, w, gain, gsum, tabs, seq):
    t_tot = h2d.shape[0]
    nseq = seq // TM
    row = lambda i: (i, 0)
    fixed = lambda i: (0, 0)
    tab = lambda i: (i % nseq, 0)
    return pl.pallas_call(
        _proj_kernel,
        grid=(t_tot // TM,),
        in_specs=[pl.BlockSpec((TM, D_MODEL), row),
                  pl.BlockSpec((1, D_MODEL), fixed),
                  pl.BlockSpec((D_MODEL, IN_COLS), fixed),
                  pl.BlockSpec((1, IN_COLS), fixed),
                  pl.BlockSpec((MXU_N, MXU_N), fixed)]
                 + [pl.BlockSpec((TM, LANES), tab)] * 6,
        out_specs=[pl.BlockSpec((TM, AB_COLS), row)]
                  + [pl.BlockSpec((TM // dil, dil * C_COLS), row) for dil in DILS],
        out_shape=[jax.ShapeDtypeStruct((t_tot, AB_COLS), BF16)]
                  + [jax.ShapeDtypeStruct((t_tot // dil, dil * C_COLS), BF16) for dil in DILS],
        scratch_shapes=[pltpu.VMEM((C_COLS // LANES, TM, LANES), F32)],
        compiler_params=pltpu.CompilerParams(dimension_semantics=("parallel",),
                                             vmem_limit_bytes=VMEM_LIMIT),
    )(h2d, g, w, gain, gsum, *tabs)


def _stack_heads(q):
    qf = q.astype(F32)
    lo = lax.broadcasted_iota(I32, qf.shape, 1) < HEAD_DIM
    return jnp.concatenate([jnp.where(lo, qf, 0.0), jnp.where(lo, 0.0, qf)], axis=0).astype(BF16)


def _merge_heads(o):
    n = o.shape[0] // 2
    lo = lax.broadcasted_iota(I32, (n, LANES), 1) < HEAD_DIM
    return jnp.where(lo, o[:n], o[n:])


def _scores(qs, k):
    return lax.dot_general(qs, k, (((1,), (1,)), ((), ())), preferred_element_type=F32)


def _attn_a_kernel(q_ref, k_ref, v_ref, o_ref):
    k = k_ref[0]
    v = v_ref[0]
    for rc in range(TQ_A // TQ_A_CHAIN):
        rows = slice(rc * TQ_A_CHAIN, (rc + 1) * TQ_A_CHAIN)
        for blk in range(A_Q_W // LANES):
            qs = _stack_heads(q_ref[0, rows, blk * LANES:(blk + 1) * LANES])
            s = _scores(qs, k)
            m = jnp.max(s, axis=-1, keepdims=True)
            p = jnp.exp(s - m)
            l = jnp.sum(p, axis=-1, keepdims=True)
            o = jnp.dot(p.astype(BF16), v, preferred_element_type=F32) / l
            o_ref[0, rows, blk * LANES:(blk + 1) * LANES] = _merge_heads(o)


def _attn_a_call(qkv):
    bsz, seq, _ = qkv.shape
    return pl.pallas_call(
        _attn_a_kernel,
        grid=(bsz, seq // TQ_A),
        in_specs=[pl.BlockSpec((1, TQ_A, A_Q_W), lambda b, i: (b, i, 0)),
                  pl.BlockSpec((1, seq, LANES), lambda b, i: (b, 0, CB_KA)),
                  pl.BlockSpec((1, seq, LANES), lambda b, i: (b, 0, CB_VA))],
        out_specs=pl.BlockSpec((1, TQ_A, A_Q_W), lambda b, i: (b, i, 0)),
        out_shape=jax.ShapeDtypeStruct((bsz, seq, A_Q_W), F32),
        compiler_params=pltpu.CompilerParams(dimension_semantics=("parallel", "parallel"),
                                             vmem_limit_bytes=VMEM_LIMIT),
    )(qkv, qkv, qkv)


def _attn_b_kernel(q_ref, k_ref, v_ref, bias_ref, o_ref, *, n_rows):
    tq = NA_QROWS * GRID_W
    n_g = n_rows // NA_QROWS

    def group(g, cfg):
        q0 = pl.multiple_of(g * tq, tq)
        krow0 = jnp.clip(g * NA_QROWS - NA_ROWS // 2, 0, n_rows - NA_KROWS)
        t0 = pl.multiple_of(krow0 * GRID_W, GRID_W)
        kw = k_ref[0, pl.ds(t0, NA_KROWS * GRID_W), :]
        vw = v_ref[0, pl.ds(t0, NA_KROWS * GRID_W), :]
        s = _scores(_stack_heads(q_ref[0, pl.ds(q0, tq), :]), kw) + bias_ref[0, cfg]
        m = jnp.max(s, axis=-1, keepdims=True)
        p = jnp.exp(s - m)
        l = jnp.sum(p, axis=-1, keepdims=True)
        o = jnp.dot(p.astype(BF16), vw, preferred_element_type=F32) / l
        o_ref[0, pl.ds(q0, tq), :] = _merge_heads(o)

    def interior(g, carry):
        group(g, 1)
        return carry

    group(0, 0)
    lax.fori_loop(1, n_g - 1, interior, 0, unroll=2)
    group(n_g - 1, 2)


def _attn_b_call(qkv, bias):
    bsz, seq, _ = qkv.shape
    spec = lambda cb: pl.BlockSpec((1, seq, LANES), lambda p, b: (b, 0, cb + p))
    return pl.pallas_call(
        functools.partial(_attn_b_kernel, n_rows=seq // GRID_W),
        grid=(B_W // LANES, bsz),
        in_specs=[spec(CB_QB), spec(CB_KB), spec(CB_VB),
                  pl.BlockSpec((1,) + bias.shape[1:], lambda p, b: (p, 0, 0, 0))],
        out_specs=pl.BlockSpec((1, seq, LANES), lambda p, b: (b, 0, p)),
        out_shape=jax.ShapeDtypeStruct((bsz, seq, B_W), F32),
        compiler_params=pltpu.CompilerParams(dimension_semantics=("parallel",) * 2,
                                             vmem_limit_bytes=VMEM_LIMIT),
    )(qkv, qkv, qkv, bias)


def _na_bias_tiles(rpb, n_rows):
    wr = NA_ROWS
    n_ro, n_co = 2 * NA_ROWS - 1, 2 * NA_COLS - 1
    row_sel = np.zeros((3, NA_QROWS, NA_KROWS, n_ro), np.float32)
    for z, r0 in enumerate((0, NA_QROWS, n_rows - NA_QROWS)):
        start = int(np.clip(r0 - wr // 2, 0, n_rows - NA_KROWS))
        for qi in range(NA_QROWS):
            r = r0 + qi
            rs = int(np.clip(r - wr // 2, 0, n_rows - wr))
            for kj in range(NA_KROWS):
                krow = start + kj
                if rs <= krow < rs + wr:
                    row_sel[z, qi, kj, krow - r + NA_ROWS - 1] = 1.0
    col_sel = np.zeros((GRID_W, GRID_W, n_co), np.float32)
    for c in range(GRID_W):
        cstart = int(np.clip(c - NA_COLS // 2, 0, GRID_W - NA_COLS))
        for kc in range(cstart, cstart + NA_COLS):
            col_sel[c, kc, kc - c + NA_COLS - 1] = 1.0
    ok = (row_sel.sum(-1)[:, :, None, :, None] * col_sel.sum(-1)[None, None, :, None, :]) > 0
    vals = jnp.einsum("zqka,lhab,cjb->lhzqckj", row_sel, rpb.astype(F32), col_sel,
                      precision=lax.Precision.HIGHEST)
    vals = jnp.where(ok[None, None], vals, NEG_INF)
    nq, nk = NA_QROWS * GRID_W, NA_KROWS * GRID_W
    t = vals.reshape(rpb.shape[0], B_HEADS // 2, 2, 3, nq, nk).transpose(0, 1, 3, 2, 4, 5)
    return t.reshape(rpb.shape[0], B_HEADS // 2, 3, 2 * nq, nk)


def _band_tile(q, kw, vw, band):
    s = _scores(_stack_heads(q), kw) + band
    m = jnp.max(s, axis=-1, keepdims=True)
    p = jnp.exp(s - m)
    l = jnp.sum(p, axis=-1, keepdims=True)
    o = jnp.dot(p.astype(BF16), vw, preferred_element_type=F32) / l
    lse = jnp.broadcast_to(m + jnp.log(l), (2 * TQ_C, LANES))
    return _merge_heads(o), _merge_heads(lse)


def _attn_c_kernel(c1_ref, c4_ref, c16_ref, bandw_ref, bandf_ref, o_ref, o2_ref, l2_ref, o3_ref, l3_ref,
                   *, seq):
    def branch(ref, window, dil, r, emit):
        length = seq // dil
        half = window // 2 // dil
        n_q = length // TQ_C
        win = min(TQ_C + 2 * half, length)
        base = r * 3 * LANES

        def tile(qi, band):
            if isinstance(qi, int):
                q0 = qi * TQ_C
                ks = min(max(q0 - half, 0), length - win)
            else:
                q0 = pl.multiple_of(qi * TQ_C, TQ_C)
                ks = pl.multiple_of(jnp.clip(q0 - half, 0, length - win), half)
            o, lse = _band_tile(ref[0, pl.ds(q0, TQ_C), base:base + LANES],
                                ref[0, pl.ds(ks, win), base + LANES:base + 2 * LANES],
                                ref[0, pl.ds(ks, win), base + 2 * LANES:base + 3 * LANES], band)
            emit(q0, o, lse)

        if n_q == 1:
            tile(0, bandf_ref[...])
            return

        def interior(qi, carry):
            tile(qi, bandw_ref[1])
            return carry

        tile(0, bandw_ref[0])
        lax.fori_loop(1, n_q - 1, interior, 0, unroll=2)
        tile(n_q - 1, bandw_ref[2])

    for ref, (window, dil), o_scr, l_scr in ((c4_ref, C_BRANCHES[1], o2_ref, l2_ref),
                                            (c16_ref, C_BRANCHES[2], o3_ref, l3_ref)):
        for r in range(dil):
            def emit(q0, o, lse, r=r, dil=dil, o_scr=o_scr, l_scr=l_scr):
                rows = pl.ds(r + dil * q0, TQ_C, stride=dil)
                o_scr[rows, :] = o
                l_scr[rows, :] = lse
            branch(ref, window, dil, r, emit)

    def emit1(q0, o1, l1):
        rows = pl.ds(q0, TQ_C)
        l2, l3 = l2_ref[rows, :], l3_ref[rows, :]
        lm = jnp.maximum(jnp.maximum(l1, l2), l3)
        e1, e2, e3 = jnp.exp(l1 - lm), jnp.exp(l2 - lm), jnp.exp(l3 - lm)
        o_ref[0, rows, :] = (e1 * o1 + e2 * o2_ref[rows, :] + e3 * o3_ref[rows, :]) / (e1 + e2 + e3)

    branch(c1_ref, C_BRANCHES[0][0], 1, 0, emit1)


def _band_masks():
    half = C_BRANCHES[0][0] // 2
    assert all(w // 2 // d == half for w, d in C_BRANCHES)
    i = np.arange(2 * TQ_C)[:, None] % TQ_C

    def mask(off, win):
        return np.where(np.abs(off + i - np.arange(win)[None, :]) <= half, 0.0, NEG_INF).astype(np.float32)

    wide = np.stack([mask(off, TQ_C + 2 * half) for off in (0, half, 2 * half)])
    return jnp.asarray(wide), jnp.asarray(mask(0, TQ_C))


def _attn_c_call(c1, c4, c16, bsz, seq):
    views = [c.reshape(bsz, seq // dil, dil * C_COLS) for c, dil in zip((c1, c4, c16), DILS)]
    spec = lambda dil: pl.BlockSpec((1, seq // dil, dil * 3 * LANES), lambda b, p: (b, 0, p))
    bandw, bandf = _band_masks()
    assert seq // DILS[-1] == TQ_C and seq // DILS[1] > TQ_C
    return pl.pallas_call(
        functools.partial(_attn_c_kernel, seq=seq),
        grid=(bsz, N_PAIRS_C),
        in_specs=[spec(dil) for dil in DILS]
                 + [pl.BlockSpec(bandw.shape, lambda b, p: (0, 0, 0)),
                    pl.BlockSpec(bandf.shape, lambda b, p: (0, 0))],
        out_specs=pl.BlockSpec((1, seq, LANES), lambda b, p: (b, 0, p)),
        out_shape=jax.ShapeDtypeStruct((bsz, seq, C_W), F32),
        scratch_shapes=[pltpu.VMEM((seq, LANES), F32)] * 4,
        compiler_params=pltpu.CompilerParams(dimension_semantics=("parallel",) * 2,
                                             vmem_limit_bytes=VMEM_LIMIT),
    )(*views, bandw, bandf)


def _out_kernel(oa_ref, ob_ref, oc_ref, h_ref, ga_ref, gb_ref, gc_ref, w_ref, gf_ref, wrh_ref, wrl_ref,
                h1_ref, hn_ref, lt_ref):
    mixed = jnp.concatenate([_rms(oa_ref[...], ga_ref[...]),
                             _rms(ob_ref[...], gb_ref[...]),
                             _rms(oc_ref[...], gc_ref[...])], axis=-1).astype(BF16)
    h1 = h_ref[...] + jnp.dot(mixed, w_ref[...], preferred_element_type=F32)
    h1_ref[...] = h1
    hn = _rms(h1, gf_ref[...])
    hi = hn.astype(BF16)
    hn_ref[...] = hi
    lo = (hn - hi.astype(F32)).astype(BF16)
    logits = (jnp.dot(hi, wrh_ref[...], preferred_element_type=F32)
              + jnp.dot(lo, wrh_ref[...], preferred_element_type=F32)
              + jnp.dot(hi, wrl_ref[...], preferred_element_type=F32))
    lt_ref[...] = logits.T[:N_EXPERTS, :]


def _out_call(oa, ob, oc, h2d, ga, gb, gc, w, gf, wrh, wrl):
    t_tot = h2d.shape[0]
    row = lambda i: (i, 0)
    fixed = lambda i: (0, 0)
    rs = lambda w_: pl.BlockSpec((TM, w_), row)
    fs = lambda a: pl.BlockSpec(a.shape, fixed)
    return pl.pallas_call(
        _out_kernel,
        grid=(t_tot // TM,),
        in_specs=[rs(A_Q_W), rs(B_W), rs(C_W), rs(D_MODEL)]
                 + [fs(ga), fs(gb), fs(gc), fs(w), fs(gf), fs(wrh), fs(wrl)],
        out_specs=[rs(D_MODEL), rs(D_MODEL), pl.BlockSpec((N_EXPERTS, TM), lambda i: (0, i))],
        out_shape=[jax.ShapeDtypeStruct((t_tot, D_MODEL), F32),
                   jax.ShapeDtypeStruct((t_tot, D_MODEL), BF16),
                   jax.ShapeDtypeStruct((N_EXPERTS, t_tot), F32)],
        compiler_params=pltpu.CompilerParams(dimension_semantics=("parallel",),
                                             vmem_limit_bytes=VMEM_LIMIT),
    )(oa, ob, oc, h2d, ga, gb, gc, w, gf, wrh, wrl)


def _route_kernel(lt_ref, tri_ref, blk_ref, pos_ref, gate_ref, meta_ref, *, cap):
    l = lt_ref[...]
    seq = l.shape[1]
    m = jnp.max(l, axis=0, keepdims=True)
    ex = jnp.exp(l - m)
    aff = ex / jnp.sum(ex, axis=0, keepdims=True)
    gate_ref[0] = aff
    bits = lax.bitcast_convert_type(aff, I32)

    def count(mask):
        return jnp.sum(mask.astype(F32), axis=1, keepdims=True)

    def search(i, t):
        cand = t | (1 << (30 - i))
        return jnp.where(count(bits >= cand) >= cap, cand, t)

    thr = lax.fori_loop(0, 31, search, jnp.zeros((N_EXPERTS, 1), I32))
    gt = bits > thr
    eq = bits == thr
    need = cap - count(gt)

    tri = tri_ref[...]

    def excl_prefix(mask):
        mb = jnp.where(mask, 1.0, 0.0).astype(BF16)
        parts = []
        run = jnp.zeros((N_EXPERTS, 1), F32)
        for c in range(seq // LANES):
            ch = mb[:, c * LANES:(c + 1) * LANES]
            parts.append(jnp.dot(ch, tri, preferred_element_type=F32) + run)
            run = run + jnp.sum(ch.astype(F32), axis=1, keepdims=True)
        return jnp.concatenate(parts, axis=1)

    sel = gt | (eq & (excl_prefix(eq) < need))
    pos = excl_prefix(sel)
    pos_ref[0] = jnp.where(sel, pos, -1.0).astype(I32)
    sb = jnp.where(sel, 1.0, 0.0).astype(BF16)
    sc = jnp.dot(sb, blk_ref[...], preferred_element_type=F32)
    start = sc[:, :LANES]
    cnt = sc[:, LANES:]
    ilo = jnp.minimum(jnp.floor(start * (1.0 / TOK_BLK)), cap // TOK_BLK - 1.0)
    cross = jnp.where(start + cnt > (ilo + 1.0) * TOK_BLK, 1.0, 0.0)
    lane = lax.broadcasted_iota(I32, cross.shape, 1)
    has = jnp.sum(cross, axis=1, keepdims=True)
    jstar = jnp.sum(cross * lane.astype(F32), axis=1, keepdims=True)
    ifix = cap // TOK_BLK - has
    n_tb = seq // TOK_BLK
    meta = jnp.where(lane < n_tb, ilo, jnp.where(lane == n_tb, jstar, jnp.where(lane == n_tb + 1, ifix, 0.0)))
    meta_ref[0] = meta.astype(I32)


def _route_call(lt, tri, blk, bsz, seq, cap):
    e = N_EXPERTS
    big = lambda dt: jax.ShapeDtypeStruct((bsz, e, seq), dt)
    small = jax.ShapeDtypeStruct((bsz, e, LANES), I32)
    bspec = pl.BlockSpec((1, e, seq), lambda b: (b, 0, 0))
    sspec = pl.BlockSpec((1, e, LANES), lambda b: (b, 0, 0))
    return pl.pallas_call(
        functools.partial(_route_kernel, cap=cap),
        grid=(bsz,),
        in_specs=[pl.BlockSpec((e, seq), lambda b: (0, b)),
                  pl.BlockSpec(tri.shape, lambda b: (0, 0)),
                  pl.BlockSpec(blk.shape, lambda b: (0, 0))],
        out_specs=[bspec, bspec, sspec],
        out_shape=[big(I32), big(F32), small],
        compiler_params=pltpu.CompilerParams(dimension_semantics=("parallel",),
                                             vmem_limit_bytes=VMEM_LIMIT),
    )(lt, tri, blk)


def _moe_kernel(meta_ref, hn_ref, pos_ref, gate_ref, wg_ref, wu_ref, wd_ref, o_ref,
                acc_ref, xg_ref, y_ref, *, cap, n_tb):
    b = pl.program_id(0)
    k = pl.program_id(1)
    out_rows = o_ref.shape[1]

    @pl.when(k == 0)
    def _():
        acc_ref[...] = jnp.zeros_like(acc_ref)
        y_ref[cap:, :] = jnp.zeros((TOK_BLK, y_ref.shape[1]), BF16)

    @pl.when(k < N_EXPERTS)
    def _():
        xg_ref[...] = jnp.zeros_like(xg_ref)
        slot = lax.broadcasted_iota(I32, (TOK_BLK, TOK_BLK), 0)
        mbase = (b * N_EXPERTS + k) * META_W
        jstar = meta_ref[mbase + n_tb]
        ifix = meta_ref[mbase + n_tb + 1]

        def gather(j, i):
            lp = pos_ref[0, 0, j]
            hb = hn_ref[0, pl.ds(pl.multiple_of(j * TOK_BLK, TOK_BLK), TOK_BLK), :]
            oh = jnp.where(slot + i * TOK_BLK == lp, 1.0, 0.0).astype(BF16)
            r0 = pl.multiple_of(i * TOK_BLK, TOK_BLK)
            xg_ref[pl.ds(r0, TOK_BLK), :] += jnp.dot(oh, hb, preferred_element_type=F32)

        def gather_body(j, carry):
            gather(j, meta_ref[mbase + j])
            return carry

        lax.fori_loop(0, n_tb, gather_body, 0, unroll=2)
        gather(jstar, ifix)

        x = xg_ref[:cap, :].astype(BF16)
        y = jnp.zeros((cap, D_MODEL), F32)
        d_ff = wg_ref.shape[2]
        for fc in range(d_ff // MXU_N):
            cs = slice(fc * MXU_N, (fc + 1) * MXU_N)
            g = jnp.dot(x, wg_ref[0, :, cs], preferred_element_type=F32)
            u = jnp.dot(x, wu_ref[0, :, cs], preferred_element_type=F32)
            hid = (g * jax.nn.sigmoid(g) * u).astype(BF16)
            y = y + jnp.dot(hid, wd_ref[0, cs, :], preferred_element_type=F32)
        y_ref[:cap, :] = y.astype(BF16)

        def scatter(j, i):
            lp = pos_ref[0, 0, j]
            gr = gate_ref[0, 0, j]
            t0 = pl.multiple_of(j * TOK_BLK, TOK_BLK)
            oh = jnp.where(slot + i * TOK_BLK == lp, gr, 0.0).astype(BF16)
            r0 = pl.multiple_of(i * TOK_BLK, TOK_BLK)
            acc_ref[pl.ds(t0, TOK_BLK), :] += lax.dot_general(
                oh, y_ref[pl.ds(r0, TOK_BLK), :], (((0,), (0,)), ((), ())),
                preferred_element_type=F32)

        def scatter_body(j, carry):
            scatter(j, meta_ref[mbase + j])
            return carry

        lax.fori_loop(0, n_tb, scatter_body, 0, unroll=2)
        scatter(jstar, ifix)

    @pl.when(k >= N_EXPERTS)
    def _():
        r0 = pl.multiple_of((k - N_EXPERTS) * out_rows, out_rows)
        o_ref[0] = acc_ref[pl.ds(r0, out_rows), :]


def _moe_call(meta, hn, pos, gate, wg, wu, wd, cap):
    bsz, seq, d = hn.shape
    n_tb = seq // TOK_BLK
    e_idx = lambda k: jnp.minimum(k, N_EXPERTS - 1)
    tok_spec = pl.BlockSpec((1, 1, n_tb, 1, TOK_BLK), lambda b, k, *_: (b, e_idx(k), 0, 0, 0))
    w_spec = lambda w: pl.BlockSpec((1,) + w.shape[1:], lambda b, k, *_: (e_idx(k), 0, 0))
    out_rows = seq // N_OUT_CHUNKS
    grid_spec = pltpu.PrefetchScalarGridSpec(
        num_scalar_prefetch=1,
        grid=(bsz, N_EXPERTS + N_OUT_CHUNKS),
        in_specs=[pl.BlockSpec((1, seq, d), lambda b, k, *_: (b, 0, 0), pipeline_mode=pl.Buffered(1)),
                  tok_spec, tok_spec, w_spec(wg), w_spec(wu), w_spec(wd)],
        out_specs=pl.BlockSpec((1, out_rows, d),
                               lambda b, k, *_: (b, jnp.maximum(k - N_EXPERTS, 0), 0)),
        scratch_shapes=[pltpu.VMEM((seq, d), F32), pltpu.VMEM((cap + TOK_BLK, d), F32),
                        pltpu.VMEM((cap + TOK_BLK, d), BF16)],
    )
    return pl.pallas_call(
        functools.partial(_moe_kernel, cap=cap, n_tb=n_tb),
        grid_spec=grid_spec,
        out_shape=jax.ShapeDtypeStruct((bsz, seq, d), F32),
        compiler_params=pltpu.CompilerParams(dimension_semantics=("parallel", "arbitrary"),
                                             vmem_limit_bytes=VMEM_LIMIT),
    )(meta, hn, pos, gate, wg, wu, wd)


def _ple_kernel(h_ref, moe_ref, p_ref, g_ref, wg_ref, wp_ref, o_ref):
    h2 = h_ref[...] + moe_ref[...]
    a = _rms(h2, g_ref[...]).astype(BF16)
    gate = jax.nn.sigmoid(jnp.dot(a, wg_ref[...], preferred_element_type=F32))
    proj = jnp.dot(p_ref[...].astype(BF16), wp_ref[...], preferred_element_type=F32)
    o_ref[...] = h2 + gate * proj


def _ple_call(h1, moe, p2d, g, wg, wp):
    t_tot = h1.shape[0]
    row = lambda i: (i, 0)
    fixed = lambda i: (0, 0)
    return pl.pallas_call(
        _ple_kernel,
        grid=(t_tot // TM,),
        in_specs=[pl.BlockSpec((TM, D_MODEL), row), pl.BlockSpec((TM, D_MODEL), row),
                  pl.BlockSpec((TM, p2d.shape[1]), row), pl.BlockSpec((1, D_MODEL), fixed),
                  pl.BlockSpec(wg.shape, fixed), pl.BlockSpec(wp.shape, fixed)],
        out_specs=pl.BlockSpec((TM, D_MODEL), row),
        out_shape=jax.ShapeDtypeStruct((t_tot, D_MODEL), F32),
        compiler_params=pltpu.CompilerParams(dimension_semantics=("parallel",),
                                             vmem_limit_bytes=VMEM_LIMIT),
    )(h1, moe, p2d, g, wg, wp)


def _rope_tables(seq):
    pos = jnp.arange(seq)

    def angles(pos_f, n, theta):
        inv = theta ** (-jnp.arange(0, n, 2, dtype=F32) / n)
        return pos_f[:, None] * inv[None, :]

    ar = angles((pos // GRID_W).astype(F32), HEAD_DIM // 2, AXIAL_THETA)
    ac = angles((pos % GRID_W).astype(F32), HEAD_DIM // 2, AXIAL_THETA)
    a1 = angles(pos.astype(F32), ROPE_DIMS, ROPE_THETA)
    z16 = jnp.zeros_like(ar)
    rep = LANES // HEAD_DIM
    cos_a = jnp.tile(jnp.concatenate([jnp.cos(ar), jnp.cos(ar), jnp.cos(ac), jnp.cos(ac)], -1), (1, rep))
    sm_a = jnp.tile(jnp.concatenate([-jnp.sin(ar), z16, -jnp.sin(ac), z16], -1), (1, rep))
    sp_a = jnp.tile(jnp.concatenate([z16, jnp.sin(ar), z16, jnp.sin(ac)], -1), (1, rep))
    z8 = jnp.zeros_like(a1)
    rest = HEAD_DIM - ROPE_DIMS
    ones = jnp.ones((seq, rest), F32)
    zr = jnp.zeros((seq, rest), F32)
    cos_c = jnp.tile(jnp.concatenate([jnp.cos(a1), jnp.cos(a1), ones], -1), (1, rep))
    sm_c = jnp.tile(jnp.concatenate([-jnp.sin(a1), z8, zr], -1), (1, rep))
    sp_c = jnp.tile(jnp.concatenate([z8, jnp.sin(a1), zr], -1), (1, rep))
    return cos_a, sm_a, sp_a, cos_c, sm_c, sp_c


_A_HEAD_ORDER = (0, 2, 1, 3)
_A_PERM = np.concatenate([np.arange(h * HEAD_DIM, (h + 1) * HEAD_DIM) for h in _A_HEAD_ORDER])


def _gain_row(qg):
    scale = HEAD_DIM ** -0.5
    one = lambda n: jnp.ones((n,), F32)
    return jnp.concatenate([
        jnp.tile(qg[0, 0] * scale, A_Q_HEADS), jnp.tile(qg[0, 1], A_KV_HEADS), one(A_KV_W),
        jnp.tile(qg[1, 0] * scale, B_HEADS), jnp.tile(qg[1, 1], B_HEADS), one(B_W),
        jnp.tile(qg[2, 0] * scale, C_HEADS), jnp.tile(qg[2, 1], C_HEADS), one(C_W)])[None, :]


def kernel(x, p, g_mix, w_in, qk_gain, na_bias, g_out, w_out, g_ffn, w_router,
           w_gate, w_up, w_down, g_ple, w_ple_gate, w_ple_proj):
    bsz, seq, d = x.shape
    depth = w_in.shape[0]
    t_tot = bsz * seq
    cap = max(1, EC_CAPACITY * seq // N_EXPERTS)
    n_tb = seq // TOK_BLK

    tabs = _rope_tables(seq)
    lane = np.arange(LANES)
    col = np.arange(MXU_N)
    gsum = jnp.asarray((col[:, None] // HEAD_DIM) == (col[None, :] // HEAD_DIM), BF16)
    tri = jnp.asarray(lane[:, None] < lane[None, :], BF16)
    tok = np.arange(seq)[:, None]
    j = np.arange(LANES)[None, :]
    blk_start = (tok < j * TOK_BLK) & (j < n_tb)
    blk_count = (tok // TOK_BLK == j) & (j < n_tb)
    blk = jnp.asarray(np.concatenate([blk_start, blk_count], axis=1), BF16)

    bias = _na_bias_tiles(na_bias, seq // GRID_W)
    h = x.reshape(t_tot, d)
    for i in range(depth):
        w_i = jnp.concatenate([w_in[i][:, _A_PERM], w_in[i][:, A_Q_W:]], axis=1).astype(BF16)
        ab, c1, c4, c16 = _proj_call(h, g_mix[i][None], w_i, _gain_row(qk_gain[i]), gsum, tabs, seq)
        ab = ab.reshape(bsz, seq, AB_COLS)
        oa = _attn_a_call(ab).reshape(t_tot, A_Q_W)
        ob = _attn_b_call(ab, bias[i]).reshape(t_tot, B_W)
        oc = _attn_c_call(c1, c4, c16, bsz, seq).reshape(t_tot, C_W)
        go = g_out[i]
        w_o = jnp.concatenate([w_out[i][_A_PERM], w_out[i][A_Q_W:]], axis=0).astype(BF16)
        wr = jnp.pad(w_router[i], ((0, 0), (0, LANES - N_EXPERTS)))
        wr_hi = wr.astype(BF16)
        wr_lo = (wr - wr_hi.astype(F32)).astype(BF16)
        h1, hn, lt = _out_call(oa, ob, oc, h, go[:A_Q_W][_A_PERM][None], go[None, A_Q_W:A_Q_W + B_W],
                               go[None, A_Q_W + B_W:], w_o, g_ffn[i][None], wr_hi, wr_lo)
        pos, gate, meta = _route_call(lt, tri, blk, bsz, seq, cap)
        moe = _moe_call(meta[:, :, :META_W].reshape(-1),
                        hn.reshape(bsz, seq, d),
                        pos.reshape(bsz, N_EXPERTS, n_tb, 1, TOK_BLK),
                        gate.reshape(bsz, N_EXPERTS, n_tb, 1, TOK_BLK),
                        w_gate[i].astype(BF16), w_up[i].astype(BF16), w_down[i].astype(BF16), cap)
        h = _ple_call(h1, moe.reshape(t_tot, d), p[i].reshape(t_tot, -1), g_ple[i][None],
                      w_ple_gate[i].astype(BF16), w_ple_proj[i].astype(BF16))
    return h.reshape(bsz, seq, d)
```

```python
import functools

import numpy as np
import jax
import jax.numpy as jnp
from jax import lax
from jax.experimental import pallas as pl
from jax.experimental.pallas import tpu as pltpu

F32 = jnp.float32
BF16 = jnp.bfloat16
I32 = jnp.int32

D_MODEL = 1024
HEAD_DIM = 64
A_Q_HEADS = 4
A_KV_HEADS = 2
B_HEADS = 6
C_HEADS = 6
A_Q_W = A_Q_HEADS * HEAD_DIM
A_KV_W = A_KV_HEADS * HEAD_DIM
B_W = B_HEADS * HEAD_DIM
C_W = C_HEADS * HEAD_DIM
IN_COLS = A_Q_W + 2 * A_KV_W + 3 * B_W + 3 * C_W
GRID_W = 64
AXIAL_THETA = 10000.0
NA_ROWS = 8
NA_COLS = 16
C_BRANCHES = ((128, 1), (512, 4), (2048, 16))
ROPE_THETA = 500000.0
ROPE_DIMS = HEAD_DIM // 4
N_EXPERTS = 16
EC_CAPACITY = 2
EPS = 1e-6
NEG_INF = -1e30

LANES = 128
MXU_N = 256
VMEM_LIMIT = 56 * 1024 * 1024

N_CB = IN_COLS // LANES
CB_QA, CB_KA, CB_VA = 0, 2, 3
CB_QB, CB_KB, CB_VB = 4, 7, 10
CB_QC, CB_KC, CB_VC = 13, 16, 19
CB_KIND = ("A", "A", "A", "V") + ("N",) * 6 + ("V",) * 3 + ("C",) * 6 + ("V",) * 3
AB_COLS = CB_QC * LANES
C_COLS = 3 * C_W
N_PAIRS_C = C_W // LANES
DILS = tuple(d for _, d in C_BRANCHES)

TM = 512
TQ_A = 512
TQ_A_CHAIN = 128
NA_QROWS = 4
NA_KROWS = 12
NA_UNROLL = 7
TQ_C = 128
C_UNROLL = 10
TOK_BLK = 256
N_OUT_CHUNKS = 8
META_W = 32


def _rms(x, g):
    return x * lax.rsqrt(jnp.mean(x * x, axis=-1, keepdims=True) + EPS) * g


def _proj_kernel(h_ref, g_ref, w_ref, gain_ref, gsum_ref, ca_ref, sma_ref, spa_ref,
                 cc_ref, smc_ref, spc_ref, ab_ref, c1_ref, c4_ref, c16_ref, cs_ref):
    x = h_ref[...]
    a = _rms(x, g_ref[...]).astype(BF16)
    gsum = gsum_ref[...]
    per = MXU_N // LANES
    n_c = IN_COLS // MXU_N
    accs = [jnp.dot(a, w_ref[:, c * MXU_N:(c + 1) * MXU_N], preferred_element_type=F32) for c in range(n_c)]
    normed = [c for c in range(n_c) if any(CB_KIND[cb] != "V" for cb in range(c * per, (c + 1) * per))]
    sq = jnp.concatenate([accs[c] * accs[c] for c in normed], axis=0)
    hi = sq.astype(BF16)
    lo = (sq - hi.astype(F32)).astype(BF16)
    ss = jnp.dot(hi, gsum, preferred_element_type=F32) + jnp.dot(lo, gsum, preferred_element_type=F32)
    inv_all = lax.rsqrt(ss * (1.0 / HEAD_DIM) + EPS)
    for c in range(n_c):
        acc = accs[c]
        if c in normed:
            inv = inv_all[normed.index(c) * TM:(normed.index(c) + 1) * TM]
        for hf, cb in enumerate(range(c * per, (c + 1) * per)):
            t = acc[:, hf * LANES:(hf + 1) * LANES]
            kind = CB_KIND[cb]
            if kind != "V":
                t = t * inv[:, hf * LANES:(hf + 1) * LANES] * gain_ref[:, cb * LANES:(cb + 1) * LANES]
                if kind == "A":
                    sh = HEAD_DIM // 4
                    t = (t * ca_ref[...] + pltpu.roll(t, LANES - sh, 1) * sma_ref[...]
                         + pltpu.roll(t, sh, 1) * spa_ref[...])
                elif kind == "C":
                    sh = ROPE_DIMS // 2
                    t = (t * cc_ref[...] + pltpu.roll(t, LANES - sh, 1) * smc_ref[...]
                         + pltpu.roll(t, sh, 1) * spc_ref[...])
            if cb < CB_QC:
                ab_ref[:, cb * LANES:(cb + 1) * LANES] = t.astype(BF16)
            else:
                cs_ref[cb - CB_QC] = t
    for p in range(N_PAIRS_C):
        for which in range(3):
            src = which * N_PAIRS_C + p
            for dil, ref in zip(DILS, (c1_ref, c4_ref, c16_ref)):
                for r in range(dil):
                    dst = ((p * dil + r) * 3 + which) * LANES
                    rows = pl.ds(r, TM // dil, stride=dil) if dil > 1 else slice(None)
                    ref[:, dst:dst + LANES] = cs_ref[src, rows, :].astype(BF16)


def _proj_call(h2d, g, w, gain, gsum, tabs, seq):
    t_tot = h2d.shape[0]
    nseq = seq // TM
    row = lambda i: (i, 0)
    fixed = lambda i: (0, 0)
    tab = lambda i: (i % nseq, 0)
    return pl.pallas_call(
        _proj_kernel,
        grid=(t_tot // TM,),
        in_specs=[pl.BlockSpec((TM, D_MODEL), row),
                  pl.BlockSpec((1, D_MODEL), fixed),
                  pl.BlockSpec((D_MODEL, IN_COLS), fixed),
                  pl.BlockSpec((1, IN_COLS), fixed),
                  pl.BlockSpec((MXU_N, MXU_N), fixed)]
                 + [pl.BlockSpec((TM, LANES), tab)] * 6,
        out_specs=[pl.BlockSpec((TM, AB_COLS), row)]
                  + [pl.BlockSpec((TM // dil, dil * C_COLS), row) for dil in DILS],
        out_shape=[jax.ShapeDtypeStruct((t_tot, AB_COLS), BF16)]
                  + [jax.ShapeDtypeStruct((t_tot // dil, dil * C_COLS), BF16) for dil in DILS],
        scratch_shapes=[pltpu.VMEM((C_COLS // LANES, TM, LANES), F32)],
        compiler_params=pltpu.CompilerParams(dimension_semantics=("parallel",),
                                             vmem_limit_bytes=VMEM_LIMIT),
    )(h2d, g, w, gain, gsum, *tabs)


def _stack_heads(q):
    qf = q.astype(F32)
    lo = lax.broadcasted_iota(I32, qf.shape, 1) < HEAD_DIM
    return jnp.concatenate([jnp.where(lo, qf, 0.0), jnp.where(lo, 0.0, qf)], axis=0).astype(BF16)


def _merge_heads(o):
    n = o.shape[0] // 2
    lo = lax.broadcasted_iota(I32, (n, LANES), 1) < HEAD_DIM
    return jnp.where(lo, o[:n], o[n:])


def _scores(qs, k):
    return lax.dot_general(qs, k, (((1,), (1,)), ((), ())), preferred_element_type=F32)


def _attn_a_kernel(q_ref, k_ref, v_ref, o_ref):
    k = k_ref[0]
    v = v_ref[0]
    for rc in range(TQ_A // TQ_A_CHAIN):
        rows = slice(rc * TQ_A_CHAIN, (rc + 1) * TQ_A_CHAIN)
        for blk in range(A_Q_W // LANES):
            qs = _stack_heads(q_ref[0, rows, blk * LANES:(blk + 1) * LANES])
            s = _scores(qs, k)
            m = jnp.max(s, axis=-1, keepdims=True)
            p = jnp.exp(s - m)
            l = jnp.sum(p, axis=-1, keepdims=True)
            o = jnp.dot(p.astype(BF16), v, preferred_element_type=F32) / l
            o_ref[0, rows, blk * LANES:(blk + 1) * LANES] = _merge_heads(o)


def _attn_a_call(qkv):
    bsz, seq, _ = qkv.shape
    return pl.pallas_call(
        _attn_a_kernel,
        grid=(bsz, seq // TQ_A),
        in_specs=[pl.BlockSpec((1, TQ_A, A_Q_W), lambda b, i: (b, i, 0)),
                  pl.BlockSpec((1, seq, LANES), lambda b, i: (b, 0, CB_KA)),
                  pl.BlockSpec((1, seq, LANES), lambda b, i: (b, 0, CB_VA))],
        out_specs=pl.BlockSpec((1, TQ_A, A_Q_W), lambda b, i: (b, i, 0)),
        out_shape=jax.ShapeDtypeStruct((bsz, seq, A_Q_W), F32),
        compiler_params=pltpu.CompilerParams(dimension_semantics=("parallel", "parallel"),
                                             vmem_limit_bytes=VMEM_LIMIT),
    )(qkv, qkv, qkv)


def _attn_b_kernel(q_ref, k_ref, v_ref, bias_ref, o_ref, *, n_rows):
    tq = NA_QROWS * GRID_W
    n_g = n_rows // NA_QROWS

    def group(g, cfg):
        q0 = pl.multiple_of(g * tq, tq)
        krow0 = jnp.clip(g * NA_QROWS - NA_ROWS // 2, 0, n_rows - NA_KROWS)
        t0 = pl.multiple_of(krow0 * GRID_W, GRID_W)
        kw = k_ref[0, pl.ds(t0, NA_KROWS * GRID_W), :]
        vw = v_ref[0, pl.ds(t0, NA_KROWS * GRID_W), :]
        s = _scores(_stack_heads(q_ref[0, pl.ds(q0, tq), :]), kw) + bias_ref[0, cfg]
        m = jnp.max(s, axis=-1, keepdims=True)
        p = jnp.exp(s - m)
        l = jnp.sum(p, axis=-1, keepdims=True)
        o = jnp.dot(p.astype(BF16), vw, preferred_element_type=F32) / l
        o_ref[0, pl.ds(q0, tq), :] = _merge_heads(o)

    def interior(g, carry):
        group(g, 1)
        return carry

    group(0, 0)
    lax.fori_loop(1, n_g - 1, interior, 0, unroll=NA_UNROLL)
    group(n_g - 1, 2)


def _attn_b_call(qkv, bias):
    bsz, seq, _ = qkv.shape
    spec = lambda cb: pl.BlockSpec((1, seq, LANES), lambda p, b: (b, 0, cb + p))
    return pl.pallas_call(
        functools.partial(_attn_b_kernel, n_rows=seq // GRID_W),
        grid=(B_W // LANES, bsz),
        in_specs=[spec(CB_QB), spec(CB_KB), spec(CB_VB),
                  pl.BlockSpec((1,) + bias.shape[1:], lambda p, b: (p, 0, 0, 0))],
        out_specs=pl.BlockSpec((1, seq, LANES), lambda p, b: (b, 0, p)),
        out_shape=jax.ShapeDtypeStruct((bsz, seq, B_W), F32),
        compiler_params=pltpu.CompilerParams(dimension_semantics=("parallel",) * 2,
                                             vmem_limit_bytes=VMEM_LIMIT),
    )(qkv, qkv, qkv, bias)


def _na_bias_tiles(rpb, n_rows):
    wr = NA_ROWS
    n_ro, n_co = 2 * NA_ROWS - 1, 2 * NA_COLS - 1
    row_sel = np.zeros((3, NA_QROWS, NA_KROWS, n_ro), np.float32)
    for z, r0 in enumerate((0, NA_QROWS, n_rows - NA_QROWS)):
        start = int(np.clip(r0 - wr // 2, 0, n_rows - NA_KROWS))
        for qi in range(NA_QROWS):
            r = r0 + qi
            rs = int(np.clip(r - wr // 2, 0, n_rows - wr))
            for kj in range(NA_KROWS):
                krow = start + kj
                if rs <= krow < rs + wr:
                    row_sel[z, qi, kj, krow - r + NA_ROWS - 1] = 1.0
    col_sel = np.zeros((GRID_W, GRID_W, n_co), np.float32)
    for c in range(GRID_W):
        cstart = int(np.clip(c - NA_COLS // 2, 0, GRID_W - NA_COLS))
        for kc in range(cstart, cstart + NA_COLS):
            col_sel[c, kc, kc - c + NA_COLS - 1] = 1.0
    ok = (row_sel.sum(-1)[:, :, None, :, None] * col_sel.sum(-1)[None, None, :, None, :]) > 0
    outside = np.where(ok, 0.0, NEG_INF).astype(np.float32)[None, None, :, None]
    n_l = rpb.shape[0]
    pairs = rpb.astype(F32).reshape(n_l, B_HEADS // 2, 2, n_ro, n_co)
    vals = jnp.einsum("zqka,lptab,cjb->lpztqckj", row_sel, pairs, col_sel,
                      precision=lax.Precision.HIGHEST) + outside
    return vals.reshape(n_l, B_HEADS // 2, 3, 2 * NA_QROWS * GRID_W, NA_KROWS * GRID_W)


def _band_tile(q, kw, vw, band):
    s = _scores(_stack_heads(q), kw) + band
    m = jnp.max(s, axis=-1, keepdims=True)
    p = jnp.exp(s - m)
    l = jnp.sum(p, axis=-1, keepdims=True)
    o = jnp.dot(p.astype(BF16), vw, preferred_element_type=F32) / l
    lse = jnp.broadcast_to(m + jnp.log(l), (2 * TQ_C, LANES))
    return _merge_heads(o), _merge_heads(lse)


def _attn_c_kernel(c1_ref, c4_ref, c16_ref, band_ref, o_ref, o2_ref, l2_ref, o3_ref, l3_ref, *, seq):
    def branch(ref, window, dil, r, emit):
        length = seq // dil
        half = window // 2 // dil
        n_q = length // TQ_C
        win = min(TQ_C + 2 * half, length)
        base = r * 3 * LANES

        def tile(qi, band):
            if isinstance(qi, int):
                q0 = qi * TQ_C
                ks = min(max(q0 - half, 0), length - win)
            else:
                q0 = pl.multiple_of(qi * TQ_C, TQ_C)
                ks = pl.multiple_of(jnp.clip(q0 - half, 0, length - win), half)
            o, lse = _band_tile(ref[0, pl.ds(q0, TQ_C), base:base + LANES],
                                ref[0, pl.ds(ks, win), base + LANES:base + 2 * LANES],
                                ref[0, pl.ds(ks, win), base + 2 * LANES:base + 3 * LANES], band)
            emit(q0, o, lse)

        def interior(qi, carry):
            tile(qi, band_ref[1])
            return carry

        tile(0, band_ref[0])
        if n_q > 2:
            lax.fori_loop(1, n_q - 1, interior, 0, unroll=min(C_UNROLL, n_q - 2))
        tile(n_q - 1, band_ref[2])

    for ref, (window, dil), o_scr, l_scr in ((c4_ref, C_BRANCHES[1], o2_ref, l2_ref),
                                            (c16_ref, C_BRANCHES[2], o3_ref, l3_ref)):
        for r in range(dil):
            def emit(q0, o, lse, r=r, dil=dil, o_scr=o_scr, l_scr=l_scr):
                rows = pl.ds(r + dil * q0, TQ_C, stride=dil)
                o_scr[rows, :] = o
                l_scr[rows, :] = lse
            branch(ref, window, dil, r, emit)

    def emit1(q0, o1, l1):
        rows = pl.ds(q0, TQ_C)
        l2, l3 = l2_ref[rows, :], l3_ref[rows, :]
        lm = jnp.maximum(jnp.maximum(l1, l2), l3)
        e1, e2, e3 = jnp.exp(l1 - lm), jnp.exp(l2 - lm), jnp.exp(l3 - lm)
        o_ref[0, rows, :] = (e1 * o1 + e2 * o2_ref[rows, :] + e3 * o3_ref[rows, :]) / (e1 + e2 + e3)

    branch(c1_ref, C_BRANCHES[0][0], 1, 0, emit1)


def _band_masks():
    half = C_BRANCHES[0][0] // 2
    assert all(w // 2 // d == half for w, d in C_BRANCHES)
    i = np.arange(2 * TQ_C)[:, None] % TQ_C
    j = np.arange(TQ_C + 2 * half)[None, :]
    return jnp.asarray(np.stack([np.where(np.abs(off + i - j) <= half, 0.0, NEG_INF)
                                 for off in (0, half, 2 * half)]).astype(np.float32))


def _attn_c_call(c1, c4, c16, bsz, seq):
    views = [c.reshape(bsz, seq // dil, dil * C_COLS) for c, dil in zip((c1, c4, c16), DILS)]
    spec = lambda dil: pl.BlockSpec((1, seq // dil, dil * 3 * LANES), lambda b, p: (b, 0, p))
    band = _band_masks()
    assert seq // DILS[-1] >= band.shape[2]
    return pl.pallas_call(
        functools.partial(_attn_c_kernel, seq=seq),
        grid=(bsz, N_PAIRS_C),
        in_specs=[spec(dil) for dil in DILS] + [pl.BlockSpec(band.shape, lambda b, p: (0, 0, 0))],
        out_specs=pl.BlockSpec((1, seq, LANES), lambda b, p: (b, 0, p)),
        out_shape=jax.ShapeDtypeStruct((bsz, seq, C_W), F32),
        scratch_shapes=[pltpu.VMEM((seq, LANES), F32)] * 4,
        compiler_params=pltpu.CompilerParams(dimension_semantics=("parallel",) * 2,
                                             vmem_limit_bytes=VMEM_LIMIT),
    )(*views, band)


def _out_kernel(oa_ref, ob_ref, oc_ref, h_ref, ga_ref, gb_ref, gc_ref, w_ref, gf_ref, wr_ref,
                h1_ref, hn_ref, lt_ref):
    mixed = jnp.concatenate([_rms(oa_ref[...], ga_ref[...]),
                             _rms(ob_ref[...], gb_ref[...]),
                             _rms(oc_ref[...], gc_ref[...])], axis=-1).astype(BF16)
    h1 = h_ref[...] + jnp.dot(mixed, w_ref[...], preferred_element_type=F32)
    h1_ref[...] = h1
    hn = _rms(h1, gf_ref[...])
    hi = hn.astype(BF16)
    hn_ref[...] = hi
    lo = (hn - hi.astype(F32)).astype(BF16)
    both = jnp.dot(hi, wr_ref[...], preferred_element_type=F32)
    logits = (both[:, :LANES] + both[:, LANES:]
              + jnp.dot(lo, wr_ref[:, :LANES], preferred_element_type=F32))
    lt_ref[...] = logits.T[:N_EXPERTS, :]


def _out_call(oa, ob, oc, h2d, ga, gb, gc, w, gf, wr):
    t_tot = h2d.shape[0]
    row = lambda i: (i, 0)
    fixed = lambda i: (0, 0)
    rs = lambda w_: pl.BlockSpec((TM, w_), row)
    fs = lambda a: pl.BlockSpec(a.shape, fixed)
    return pl.pallas_call(
        _out_kernel,
        grid=(t_tot // TM,),
        in_specs=[rs(A_Q_W), rs(B_W), rs(C_W), rs(D_MODEL)]
                 + [fs(ga), fs(gb), fs(gc), fs(w), fs(gf), fs(wr)],
        out_specs=[rs(D_MODEL), rs(D_MODEL), pl.BlockSpec((N_EXPERTS, TM), lambda i: (0, i))],
        out_shape=[jax.ShapeDtypeStruct((t_tot, D_MODEL), F32),
                   jax.ShapeDtypeStruct((t_tot, D_MODEL), BF16),
                   jax.ShapeDtypeStruct((N_EXPERTS, t_tot), F32)],
        compiler_params=pltpu.CompilerParams(dimension_semantics=("parallel",),
                                             vmem_limit_bytes=VMEM_LIMIT),
    )(oa, ob, oc, h2d, ga, gb, gc, w, gf, wr)


def _route_kernel(lt_ref, tri_ref, blk_ref, pos_ref, gate_ref, meta_ref, *, cap):
    l = lt_ref[...]
    seq = l.shape[1]
    m = jnp.max(l, axis=0, keepdims=True)
    ex = jnp.exp(l - m)
    aff = ex / jnp.sum(ex, axis=0, keepdims=True)
    gate_ref[0] = aff
    bits = lax.bitcast_convert_type(aff, I32)

    def count(mask):
        return jnp.sum(mask.astype(F32), axis=1, keepdims=True)

    def search(i, t):
        cand = t | (1 << (30 - i))
        return jnp.where(count(bits >= cand) >= cap, cand, t)

    thr = lax.fori_loop(0, 31, search, jnp.zeros((N_EXPERTS, 1), I32))
    gt = bits > thr
    eq = bits == thr
    need = cap - count(gt)

    tri = tri_ref[...]

    def excl_prefix(mask):
        mb = jnp.where(mask, 1.0, 0.0).astype(BF16)
        parts = []
        run = jnp.zeros((N_EXPERTS, 1), F32)
        for c in range(seq // LANES):
            ch = mb[:, c * LANES:(c + 1) * LANES]
            parts.append(jnp.dot(ch, tri, preferred_element_type=F32) + run)
            run = run + jnp.sum(ch.astype(F32), axis=1, keepdims=True)
        return jnp.concatenate(parts, axis=1)

    sel = gt | (eq & (excl_prefix(eq) < need))
    pos = excl_prefix(sel)
    pos_ref[0] = jnp.where(sel, pos, -1.0).astype(I32)
    sb = jnp.where(sel, 1.0, 0.0).astype(BF16)
    sc = jnp.dot(sb, blk_ref[...], preferred_element_type=F32)
    start = sc[:, :LANES]
    cnt = sc[:, LANES:]
    ilo = jnp.minimum(jnp.floor(start * (1.0 / TOK_BLK)), cap // TOK_BLK - 1.0)
    cross = jnp.where(start + cnt > (ilo + 1.0) * TOK_BLK, 1.0, 0.0)
    lane = lax.broadcasted_iota(I32, cross.shape, 1)
    has = jnp.sum(cross, axis=1, keepdims=True)
    jstar = jnp.sum(cross * lane.astype(F32), axis=1, keepdims=True)
    ifix = cap // TOK_BLK - has
    n_tb = seq // TOK_BLK
    meta = jnp.where(lane < n_tb, ilo, jnp.where(lane == n_tb, jstar, jnp.where(lane == n_tb + 1, ifix, 0.0)))
    meta_ref[0] = meta.astype(I32)


def _route_call(lt, tri, blk, bsz, seq, cap):
    e = N_EXPERTS
    big = lambda dt: jax.ShapeDtypeStruct((bsz, e, seq), dt)
    small = jax.ShapeDtypeStruct((bsz, e, LANES), I32)
    bspec = pl.BlockSpec((1, e, seq), lambda b: (b, 0, 0))
    sspec = pl.BlockSpec((1, e, LANES), lambda b: (b, 0, 0))
    return pl.pallas_call(
        functools.partial(_route_kernel, cap=cap),
        grid=(bsz,),
        in_specs=[pl.BlockSpec((e, seq), lambda b: (0, b)),
                  pl.BlockSpec(tri.shape, lambda b: (0, 0)),
                  pl.BlockSpec(blk.shape, lambda b: (0, 0))],
        out_specs=[bspec, bspec, sspec],
        out_shape=[big(I32), big(F32), small],
        compiler_params=pltpu.CompilerParams(dimension_semantics=("parallel",),
                                             vmem_limit_bytes=VMEM_LIMIT),
    )(lt, tri, blk)


def _moe_kernel(meta_ref, hn_ref, pos_ref, gate_ref, wg_ref, wu_ref, wd_ref, o_ref,
                acc_ref, xg_ref, y_ref, *, cap, n_tb):
    b = pl.program_id(0)
    k = pl.program_id(1)
    out_rows = o_ref.shape[1]

    @pl.when(k == 0)
    def _():
        acc_ref[...] = jnp.zeros_like(acc_ref)
        y_ref[cap:, :] = jnp.zeros((TOK_BLK, y_ref.shape[1]), BF16)

    @pl.when(k < N_EXPERTS)
    def _():
        xg_ref[...] = jnp.zeros_like(xg_ref)
        slot = lax.broadcasted_iota(I32, (TOK_BLK, TOK_BLK), 0)
        mbase = (b * N_EXPERTS + k) * META_W
        jstar = meta_ref[mbase + n_tb]
        ifix = meta_ref[mbase + n_tb + 1]

        def gather(j, i):
            lp = pos_ref[0, 0, j]
            hb = hn_ref[0, pl.ds(pl.multiple_of(j * TOK_BLK, TOK_BLK), TOK_BLK), :]
            oh = jnp.where(slot + i * TOK_BLK == lp, 1.0, 0.0).astype(BF16)
            r0 = pl.multiple_of(i * TOK_BLK, TOK_BLK)
            xg_ref[pl.ds(r0, TOK_BLK), :] += jnp.dot(oh, hb, preferred_element_type=F32)

        def gather_body(j, carry):
            gather(j, meta_ref[mbase + j])
            return carry

        lax.fori_loop(0, n_tb, gather_body, 0, unroll=2)
        gather(jstar, ifix)

        x = xg_ref[:cap, :].astype(BF16)
        y = jnp.zeros((cap, D_MODEL), F32)
        d_ff = wg_ref.shape[2]
        for fc in range(d_ff // MXU_N):
            cs = slice(fc * MXU_N, (fc + 1) * MXU_N)
            g = jnp.dot(x, wg_ref[0, :, cs], preferred_element_type=F32)
            u = jnp.dot(x, wu_ref[0, :, cs], preferred_element_type=F32)
            hid = (g * jax.nn.sigmoid(g) * u).astype(BF16)
            y = y + jnp.dot(hid, wd_ref[0, cs, :], preferred_element_type=F32)
        y_ref[:cap, :] = y.astype(BF16)

        def scatter(j, i):
            lp = pos_ref[0, 0, j]
            gr = gate_ref[0, 0, j]
            t0 = pl.multiple_of(j * TOK_BLK, TOK_BLK)
            oh = jnp.where(slot + i * TOK_BLK == lp, gr, 0.0).astype(BF16)
            r0 = pl.multiple_of(i * TOK_BLK, TOK_BLK)
            acc_ref[pl.ds(t0, TOK_BLK), :] += lax.dot_general(
                oh, y_ref[pl.ds(r0, TOK_BLK), :], (((0,), (0,)), ((), ())),
                preferred_element_type=F32)

        def scatter_body(j, carry):
            scatter(j, meta_ref[mbase + j])
            return carry

        lax.fori_loop(0, n_tb, scatter_body, 0, unroll=2)
        scatter(jstar, ifix)

    @pl.when(k >= N_EXPERTS)
    def _():
        r0 = pl.multiple_of((k - N_EXPERTS) * out_rows, out_rows)
        o_ref[0] = acc_ref[pl.ds(r0, out_rows), :]


def _moe_call(meta, hn, pos, gate, wg, wu, wd, cap):
    bsz, seq, d = hn.shape
    n_tb = seq // TOK_BLK
    e_idx = lambda k: jnp.minimum(k, N_EXPERTS - 1)
    tok_spec = pl.BlockSpec((1, 1, n_tb, 1, TOK_BLK), lambda b, k, *_: (b, e_idx(k), 0, 0, 0))
    w_spec = lambda w: pl.BlockSpec((1,) + w.shape[1:], lambda b, k, *_: (e_idx(k), 0, 0))
    out_rows = seq // N_OUT_CHUNKS
    grid_spec = pltpu.PrefetchScalarGridSpec(
        num_scalar_prefetch=1,
        grid=(bsz, N_EXPERTS + N_OUT_CHUNKS),
        in_specs=[pl.BlockSpec((1, seq, d), lambda b, k, *_: (b, 0, 0), pipeline_mode=pl.Buffered(1)),
                  tok_spec, tok_spec, w_spec(wg), w_spec(wu), w_spec(wd)],
        out_specs=pl.BlockSpec((1, out_rows, d),
                               lambda b, k, *_: (b, jnp.maximum(k - N_EXPERTS, 0), 0)),
        scratch_shapes=[pltpu.VMEM((seq, d), F32), pltpu.VMEM((cap + TOK_BLK, d), F32),
                        pltpu.VMEM((cap + TOK_BLK, d), BF16)],
    )
    return pl.pallas_call(
        functools.partial(_moe_kernel, cap=cap, n_tb=n_tb),
        grid_spec=grid_spec,
        out_shape=jax.ShapeDtypeStruct((bsz, seq, d), F32),
        compiler_params=pltpu.CompilerParams(dimension_semantics=("parallel", "arbitrary"),
                                             vmem_limit_bytes=VMEM_LIMIT),
    )(meta, hn, pos, gate, wg, wu, wd)


def _ple_kernel(h_ref, moe_ref, p_ref, g_ref, wg_ref, wp_ref, o_ref):
    h2 = h_ref[...] + moe_ref[...]
    a = _rms(h2, g_ref[...]).astype(BF16)
    gate = jax.nn.sigmoid(jnp.dot(a, wg_ref[...], preferred_element_type=F32))
    proj = jnp.dot(p_ref[...].astype(BF16), wp_ref[...], preferred_element_type=F32)
    o_ref[...] = h2 + gate * proj


def _ple_call(h1, moe, p2d, g, wg, wp):
    t_tot = h1.shape[0]
    row = lambda i: (i, 0)
    fixed = lambda i: (0, 0)
    return pl.pallas_call(
        _ple_kernel,
        grid=(t_tot // TM,),
        in_specs=[pl.BlockSpec((TM, D_MODEL), row), pl.BlockSpec((TM, D_MODEL), row),
                  pl.BlockSpec((TM, p2d.shape[1]), row), pl.BlockSpec((1, D_MODEL), fixed),
                  pl.BlockSpec(wg.shape, fixed), pl.BlockSpec(wp.shape, fixed)],
        out_specs=pl.BlockSpec((TM, D_MODEL), row),
        out_shape=jax.ShapeDtypeStruct((t_tot, D_MODEL), F32),
        compiler_params=pltpu.CompilerParams(dimension_semantics=("parallel",),
                                             vmem_limit_bytes=VMEM_LIMIT),
    )(h1, moe, p2d, g, wg, wp)


def _rope_tables(seq):
    pos = jnp.arange(seq)

    def angles(pos_f, n, theta):
        inv = theta ** (-jnp.arange(0, n, 2, dtype=F32) / n)
        return pos_f[:, None] * inv[None, :]

    ar = angles((pos // GRID_W).astype(F32), HEAD_DIM // 2, AXIAL_THETA)
    ac = angles((pos % GRID_W).astype(F32), HEAD_DIM // 2, AXIAL_THETA)
    a1 = angles(pos.astype(F32), ROPE_DIMS, ROPE_THETA)
    z16 = jnp.zeros_like(ar)
    rep = LANES // HEAD_DIM
    cos_a = jnp.tile(jnp.concatenate([jnp.cos(ar), jnp.cos(ar), jnp.cos(ac), jnp.cos(ac)], -1), (1, rep))
    sm_a = jnp.tile(jnp.concatenate([-jnp.sin(ar), z16, -jnp.sin(ac), z16], -1), (1, rep))
    sp_a = jnp.tile(jnp.concatenate([z16, jnp.sin(ar), z16, jnp.sin(ac)], -1), (1, rep))
    z8 = jnp.zeros_like(a1)
    rest = HEAD_DIM - ROPE_DIMS
    ones = jnp.ones((seq, rest), F32)
    zr = jnp.zeros((seq, rest), F32)
    cos_c = jnp.tile(jnp.concatenate([jnp.cos(a1), jnp.cos(a1), ones], -1), (1, rep))
    sm_c = jnp.tile(jnp.concatenate([-jnp.sin(a1), z8, zr], -1), (1, rep))
    sp_c = jnp.tile(jnp.concatenate([z8, jnp.sin(a1), zr], -1), (1, rep))
    return cos_a, sm_a, sp_a, cos_c, sm_c, sp_c


_A_HEAD_ORDER = (0, 2, 1, 3)
_A_PERM = np.concatenate([np.arange(h * HEAD_DIM, (h + 1) * HEAD_DIM) for h in _A_HEAD_ORDER])


def _gain_row(qg):
    scale = HEAD_DIM ** -0.5
    one = lambda n: jnp.ones((n,), F32)
    return jnp.concatenate([
        jnp.tile(qg[0, 0] * scale, A_Q_HEADS), jnp.tile(qg[0, 1], A_KV_HEADS), one(A_KV_W),
        jnp.tile(qg[1, 0] * scale, B_HEADS), jnp.tile(qg[1, 1], B_HEADS), one(B_W),
        jnp.tile(qg[2, 0] * scale, C_HEADS), jnp.tile(qg[2, 1], C_HEADS), one(C_W)])[None, :]


def kernel(x, p, g_mix, w_in, qk_gain, na_bias, g_out, w_out, g_ffn, w_router,
           w_gate, w_up, w_down, g_ple, w_ple_gate, w_ple_proj):
    bsz, seq, d = x.shape
    depth = w_in.shape[0]
    t_tot = bsz * seq
    cap = max(1, EC_CAPACITY * seq // N_EXPERTS)
    n_tb = seq // TOK_BLK

    tabs = _rope_tables(seq)
    lane = np.arange(LANES)
    col = np.arange(MXU_N)
    gsum = jnp.asarray((col[:, None] // HEAD_DIM) == (col[None, :] // HEAD_DIM), BF16)
    tri = jnp.asarray(lane[:, None] < lane[None, :], BF16)
    tok = np.arange(seq)[:, None]
    j = np.arange(LANES)[None, :]
    blk_start = (tok < j * TOK_BLK) & (j < n_tb)
    blk_count = (tok // TOK_BLK == j) & (j < n_tb)
    blk = jnp.asarray(np.concatenate([blk_start, blk_count], axis=1), BF16)

    bias = _na_bias_tiles(na_bias, seq // GRID_W)
    h = x.reshape(t_tot, d)
    for i in range(depth):
        w_i = jnp.concatenate([w_in[i][:, _A_PERM], w_in[i][:, A_Q_W:]], axis=1).astype(BF16)
        ab, c1, c4, c16 = _proj_call(h, g_mix[i][None], w_i, _gain_row(qk_gain[i]), gsum, tabs, seq)
        ab = ab.reshape(bsz, seq, AB_COLS)
        oa = _attn_a_call(ab).reshape(t_tot, A_Q_W)
        ob = _attn_b_call(ab, bias[i]).reshape(t_tot, B_W)
        oc = _attn_c_call(c1, c4, c16, bsz, seq).reshape(t_tot, C_W)
        go = g_out[i]
        w_o = jnp.concatenate([w_out[i][_A_PERM], w_out[i][A_Q_W:]], axis=0).astype(BF16)
        wr = jnp.pad(w_router[i], ((0, 0), (0, LANES - N_EXPERTS)))
        wr_hi = wr.astype(BF16)
        wr_lo = (wr - wr_hi.astype(F32)).astype(BF16)
        h1, hn, lt = _out_call(oa, ob, oc, h, go[:A_Q_W][_A_PERM][None], go[None, A_Q_W:A_Q_W + B_W],
                               go[None, A_Q_W + B_W:], w_o, g_ffn[i][None],
                               jnp.concatenate([wr_hi, wr_lo], axis=1))
        pos, gate, meta = _route_call(lt, tri, blk, bsz, seq, cap)
        moe = _moe_call(meta[:, :, :META_W].reshape(-1),
                        hn.reshape(bsz, seq, d),
                        pos.reshape(bsz, N_EXPERTS, n_tb, 1, TOK_BLK),
                        gate.reshape(bsz, N_EXPERTS, n_tb, 1, TOK_BLK),
                        w_gate[i].astype(BF16), w_up[i].astype(BF16), w_down[i].astype(BF16), cap)
        h = _ple_call(h1, moe.reshape(t_tot, d), p[i].reshape(t_tot, -1), g_ple[i][None],
                      w_ple_gate[i].astype(BF16), w_ple_proj[i].astype(BF16))
    return h.reshape(bsz, seq, d)
```

```python
import functools

import numpy as np
import jax
import jax.numpy as jnp
from jax import lax
from jax.experimental import pallas as pl
from jax.experimental.pallas import tpu as pltpu

F32 = jnp.float32
BF16 = jnp.bfloat16
I32 = jnp.int32

D_MODEL = 1024
HEAD_DIM = 64
A_Q_HEADS = 4
A_KV_HEADS = 2
B_HEADS = 6
C_HEADS = 6
A_Q_W = A_Q_HEADS * HEAD_DIM
A_KV_W = A_KV_HEADS * HEAD_DIM
B_W = B_HEADS * HEAD_DIM
C_W = C_HEADS * HEAD_DIM
IN_COLS = A_Q_W + 2 * A_KV_W + 3 * B_W + 3 * C_W
GRID_W = 64
AXIAL_THETA = 10000.0
NA_ROWS = 8
NA_COLS = 16
C_BRANCHES = ((128, 1), (512, 4), (2048, 16))
ROPE_THETA = 500000.0
ROPE_DIMS = HEAD_DIM // 4
N_EXPERTS = 16
EC_CAPACITY = 2
EPS = 1e-6
NEG_INF = -1e30

LANES = 128
MXU_N = 256
VMEM_LIMIT = 56 * 1024 * 1024

N_CB = IN_COLS // LANES
CB_QA, CB_KA, CB_VA = 0, 2, 3
CB_QB, CB_KB, CB_VB = 4, 7, 10
CB_QC, CB_KC, CB_VC = 13, 16, 19
CB_KIND = ("A", "A", "A", "V") + ("N",) * 6 + ("V",) * 3 + ("C",) * 6 + ("V",) * 3
AB_COLS = CB_QC * LANES
C_COLS = 3 * C_W
N_PAIRS_C = C_W // LANES
DILS = tuple(d for _, d in C_BRANCHES)

TM = 512
TQ_A = 512
TQ_A_CHAIN = 128
NA_QROWS = 4
NA_KROWS = 12
NA_UNROLL = 7
TQ_C = 128
C_UNROLL = 10
TOK_BLK = 256
MOE_UNROLL = 16
N_OUT_CHUNKS = 8
META_W = 32


def _rms(x, g):
    return x * lax.rsqrt(jnp.mean(x * x, axis=-1, keepdims=True) + EPS) * g


def _proj_kernel(h_ref, g_ref, w_ref, gain_ref, gsum_ref, ca_ref, sma_ref, spa_ref,
                 cc_ref, smc_ref, spc_ref, ab_ref, c1_ref, c4_ref, c16_ref, cs_ref):
    x = h_ref[...]
    a = _rms(x, g_ref[...]).astype(BF16)
    gsum = gsum_ref[...]
    per = MXU_N // LANES
    n_c = IN_COLS // MXU_N
    accs = [jnp.dot(a, w_ref[:, c * MXU_N:(c + 1) * MXU_N], preferred_element_type=F32) for c in range(n_c)]
    normed = [c for c in range(n_c) if any(CB_KIND[cb] != "V" for cb in range(c * per, (c + 1) * per))]
    sq = jnp.concatenate([accs[c] * accs[c] for c in normed], axis=0)
    hi = sq.astype(BF16)
    lo = (sq - hi.astype(F32)).astype(BF16)
    ss = jnp.dot(hi, gsum, preferred_element_type=F32) + jnp.dot(lo, gsum, preferred_element_type=F32)
    inv_all = lax.rsqrt(ss * (1.0 / HEAD_DIM) + EPS)
    for c in range(n_c):
        acc = accs[c]
        if c in normed:
            inv = inv_all[normed.index(c) * TM:(normed.index(c) + 1) * TM]
        for hf, cb in enumerate(range(c * per, (c + 1) * per)):
            t = acc[:, hf * LANES:(hf + 1) * LANES]
            kind = CB_KIND[cb]
            if kind != "V":
                t = t * inv[:, hf * LANES:(hf + 1) * LANES] * gain_ref[:, cb * LANES:(cb + 1) * LANES]
                if kind == "A":
                    sh = HEAD_DIM // 4
                    t = (t * ca_ref[...] + pltpu.roll(t, LANES - sh, 1) * sma_ref[...]
                         + pltpu.roll(t, sh, 1) * spa_ref[...])
                elif kind == "C":
                    sh = ROPE_DIMS // 2
                    t = (t * cc_ref[...] + pltpu.roll(t, LANES - sh, 1) * smc_ref[...]
                         + pltpu.roll(t, sh, 1) * spc_ref[...])
            if cb < CB_QC:
                ab_ref[:, cb * LANES:(cb + 1) * LANES] = t.astype(BF16)
            else:
                cs_ref[cb - CB_QC] = t
    for p in range(N_PAIRS_C):
        for which in range(3):
            src = which * N_PAIRS_C + p
            for dil, ref in zip(DILS, (c1_ref, c4_ref, c16_ref)):
                for r in range(dil):
                    dst = ((p * dil + r) * 3 + which) * LANES
                    rows = pl.ds(r, TM // dil, stride=dil) if dil > 1 else slice(None)
                    ref[:, dst:dst + LANES] = cs_ref[src, rows, :].astype(BF16)


def _proj_call(h2d, g, w, gain, gsum, tabs, seq):
    t_tot = h2d.shape[0]
    nseq = seq // TM
    row = lambda i: (i, 0)
    fixed = lambda i: (0, 0)
    tab = lambda i: (i % nseq, 0)
    return pl.pallas_call(
        _proj_kernel,
        grid=(t_tot // TM,),
        in_specs=[pl.BlockSpec((TM, D_MODEL), row),
                  pl.BlockSpec((1, D_MODEL), fixed),
                  pl.BlockSpec((D_MODEL, IN_COLS), fixed),
                  pl.BlockSpec((1, IN_COLS), fixed),
                  pl.BlockSpec((MXU_N, MXU_N), fixed)]
                 + [pl.BlockSpec((TM, LANES), tab)] * 6,
        out_specs=[pl.BlockSpec((TM, AB_COLS), row)]
                  + [pl.BlockSpec((TM // dil, dil * C_COLS), row) for dil in DILS],
        out_shape=[jax.ShapeDtypeStruct((t_tot, AB_COLS), BF16)]
                  + [jax.ShapeDtypeStruct((t_tot // dil, dil * C_COLS), BF16) for dil in DILS],
        scratch_shapes=[pltpu.VMEM((C_COLS // LANES, TM, LANES), F32)],
        compiler_params=pltpu.CompilerParams(dimension_semantics=("parallel",),
                                             vmem_limit_bytes=VMEM_LIMIT),
    )(h2d, g, w, gain, gsum, *tabs)


def _stack_heads(q):
    qf = q.astype(F32)
    lo = lax.broadcasted_iota(I32, qf.shape, 1) < HEAD_DIM
    return jnp.concatenate([jnp.where(lo, qf, 0.0), jnp.where(lo, 0.0, qf)], axis=0).astype(BF16)


def _merge_heads(o):
    n = o.shape[0] // 2
    lo = lax.broadcasted_iota(I32, (n, LANES), 1) < HEAD_DIM
    return jnp.where(lo, o[:n], o[n:])


def _scores(qs, k):
    return lax.dot_general(qs, k, (((1,), (1,)), ((), ())), preferred_element_type=F32)


def _attn_a_kernel(q_ref, k_ref, v_ref, o_ref):
    k = k_ref[0]
    v = v_ref[0]
    for rc in range(TQ_A // TQ_A_CHAIN):
        rows = slice(rc * TQ_A_CHAIN, (rc + 1) * TQ_A_CHAIN)
        for blk in range(A_Q_W // LANES):
            qs = _stack_heads(q_ref[0, rows, blk * LANES:(blk + 1) * LANES])
            s = _scores(qs, k)
            m = jnp.max(s, axis=-1, keepdims=True)
            p = jnp.exp(s - m)
            l = jnp.sum(p, axis=-1, keepdims=True)
            o = jnp.dot(p.astype(BF16), v, preferred_element_type=F32) / l
            o_ref[0, rows, blk * LANES:(blk + 1) * LANES] = _merge_heads(o)


def _attn_a_call(qkv):
    bsz, seq, _ = qkv.shape
    return pl.pallas_call(
        _attn_a_kernel,
        grid=(bsz, seq // TQ_A),
        in_specs=[pl.BlockSpec((1, TQ_A, A_Q_W), lambda b, i: (b, i, 0)),
                  pl.BlockSpec((1, seq, LANES), lambda b, i: (b, 0, CB_KA)),
                  pl.BlockSpec((1, seq, LANES), lambda b, i: (b, 0, CB_VA))],
        out_specs=pl.BlockSpec((1, TQ_A, A_Q_W), lambda b, i: (b, i, 0)),
        out_shape=jax.ShapeDtypeStruct((bsz, seq, A_Q_W), F32),
        compiler_params=pltpu.CompilerParams(dimension_semantics=("parallel", "parallel"),
                                             vmem_limit_bytes=VMEM_LIMIT),
    )(qkv, qkv, qkv)


def _attn_b_kernel(q_ref, k_ref, v_ref, bias_ref, o_ref, *, n_rows):
    tq = NA_QROWS * GRID_W
    n_g = n_rows // NA_QROWS

    def group(g, cfg):
        q0 = pl.multiple_of(g * tq, tq)
        krow0 = jnp.clip(g * NA_QROWS - NA_ROWS // 2, 0, n_rows - NA_KROWS)
        t0 = pl.multiple_of(krow0 * GRID_W, GRID_W)
        kw = k_ref[0, pl.ds(t0, NA_KROWS * GRID_W), :]
        vw = v_ref[0, pl.ds(t0, NA_KROWS * GRID_W), :]
        s = _scores(_stack_heads(q_ref[0, pl.ds(q0, tq), :]), kw) + bias_ref[0, cfg]
        m = jnp.max(s, axis=-1, keepdims=True)
        p = jnp.exp(s - m)
        l = jnp.sum(p, axis=-1, keepdims=True)
        o = jnp.dot(p.astype(BF16), vw, preferred_element_type=F32) / l
        o_ref[0, pl.ds(q0, tq), :] = _merge_heads(o)

    def interior(g, carry):
        group(g, 1)
        return carry

    group(0, 0)
    lax.fori_loop(1, n_g - 1, interior, 0, unroll=NA_UNROLL)
    group(n_g - 1, 2)


def _attn_b_call(qkv, bias):
    bsz, seq, _ = qkv.shape
    spec = lambda cb: pl.BlockSpec((1, seq, LANES), lambda p, b: (b, 0, cb + p))
    return pl.pallas_call(
        functools.partial(_attn_b_kernel, n_rows=seq // GRID_W),
        grid=(B_W // LANES, bsz),
        in_specs=[spec(CB_QB), spec(CB_KB), spec(CB_VB),
                  pl.BlockSpec((1,) + bias.shape[1:], lambda p, b: (p, 0, 0, 0))],
        out_specs=pl.BlockSpec((1, seq, LANES), lambda p, b: (b, 0, p)),
        out_shape=jax.ShapeDtypeStruct((bsz, seq, B_W), F32),
        compiler_params=pltpu.CompilerParams(dimension_semantics=("parallel",) * 2,
                                             vmem_limit_bytes=VMEM_LIMIT),
    )(qkv, qkv, qkv, bias)


def _na_bias_tiles(rpb, n_rows):
    wr = NA_ROWS
    n_ro, n_co = 2 * NA_ROWS - 1, 2 * NA_COLS - 1
    row_sel = np.zeros((3, NA_QROWS, NA_KROWS, n_ro), np.float32)
    for z, r0 in enumerate((0, NA_QROWS, n_rows - NA_QROWS)):
        start = int(np.clip(r0 - wr // 2, 0, n_rows - NA_KROWS))
        for qi in range(NA_QROWS):
            r = r0 + qi
            rs = int(np.clip(r - wr // 2, 0, n_rows - wr))
            for kj in range(NA_KROWS):
                krow = start + kj
                if rs <= krow < rs + wr:
                    row_sel[z, qi, kj, krow - r + NA_ROWS - 1] = 1.0
    col_sel = np.zeros((GRID_W, GRID_W, n_co), np.float32)
    for c in range(GRID_W):
        cstart = int(np.clip(c - NA_COLS // 2, 0, GRID_W - NA_COLS))
        for kc in range(cstart, cstart + NA_COLS):
            col_sel[c, kc, kc - c + NA_COLS - 1] = 1.0
    ok = (row_sel.sum(-1)[:, :, None, :, None] * col_sel.sum(-1)[None, None, :, None, :]) > 0
    outside = np.where(ok, 0.0, NEG_INF).astype(np.float32)[None, None, :, None]
    n_l = rpb.shape[0]
    pairs = rpb.astype(F32).reshape(n_l, B_HEADS // 2, 2, n_ro, n_co)
    vals = jnp.einsum("zqka,lptab,cjb->lpztqckj", row_sel, pairs, col_sel,
                      precision=lax.Precision.HIGHEST) + outside
    return vals.reshape(n_l, B_HEADS // 2, 3, 2 * NA_QROWS * GRID_W, NA_KROWS * GRID_W)


def _band_tile(q, kw, vw, band):
    s = _scores(_stack_heads(q), kw) + band
    m = jnp.max(s, axis=-1, keepdims=True)
    p = jnp.exp(s - m)
    l = jnp.sum(p, axis=-1, keepdims=True)
    o = jnp.dot(p.astype(BF16), vw, preferred_element_type=F32) / l
    lse = jnp.broadcast_to(m + jnp.log(l), (2 * TQ_C, LANES))
    return _merge_heads(o), _merge_heads(lse)


def _attn_c_kernel(c1_ref, c4_ref, c16_ref, band_ref, o_ref, o2_ref, l2_ref, o3_ref, l3_ref, *, seq):
    def branch(ref, window, dil, r, emit):
        length = seq // dil
        half = window // 2 // dil
        n_q = length // TQ_C
        win = min(TQ_C + 2 * half, length)
        base = r * 3 * LANES

        def tile(qi, band):
            if isinstance(qi, int):
                q0 = qi * TQ_C
                ks = min(max(q0 - half, 0), length - win)
            else:
                q0 = pl.multiple_of(qi * TQ_C, TQ_C)
                ks = pl.multiple_of(jnp.clip(q0 - half, 0, length - win), half)
            o, lse = _band_tile(ref[0, pl.ds(q0, TQ_C), base:base + LANES],
                                ref[0, pl.ds(ks, win), base + LANES:base + 2 * LANES],
                                ref[0, pl.ds(ks, win), base + 2 * LANES:base + 3 * LANES], band)
            emit(q0, o, lse)

        def interior(qi, carry):
            tile(qi, band_ref[1])
            return carry

        tile(0, band_ref[0])
        if n_q > 2:
            lax.fori_loop(1, n_q - 1, interior, 0, unroll=min(C_UNROLL, n_q - 2))
        tile(n_q - 1, band_ref[2])

    for ref, (window, dil), o_scr, l_scr in ((c4_ref, C_BRANCHES[1], o2_ref, l2_ref),
                                            (c16_ref, C_BRANCHES[2], o3_ref, l3_ref)):
        for r in range(dil):
            def emit(q0, o, lse, r=r, dil=dil, o_scr=o_scr, l_scr=l_scr):
                rows = pl.ds(r + dil * q0, TQ_C, stride=dil)
                o_scr[rows, :] = o
                l_scr[rows, :] = lse
            branch(ref, window, dil, r, emit)

    def emit1(q0, o1, l1):
        rows = pl.ds(q0, TQ_C)
        l2, l3 = l2_ref[rows, :], l3_ref[rows, :]
        lm = jnp.maximum(jnp.maximum(l1, l2), l3)
        e1, e2, e3 = jnp.exp(l1 - lm), jnp.exp(l2 - lm), jnp.exp(l3 - lm)
        o_ref[0, rows, :] = (e1 * o1 + e2 * o2_ref[rows, :] + e3 * o3_ref[rows, :]) / (e1 + e2 + e3)

    branch(c1_ref, C_BRANCHES[0][0], 1, 0, emit1)


def _band_masks():
    half = C_BRANCHES[0][0] // 2
    assert all(w // 2 // d == half for w, d in C_BRANCHES)
    i = np.arange(2 * TQ_C)[:, None] % TQ_C
    j = np.arange(TQ_C + 2 * half)[None, :]
    return jnp.asarray(np.stack([np.where(np.abs(off + i - j) <= half, 0.0, NEG_INF)
                                 for off in (0, half, 2 * half)]).astype(np.float32))


def _attn_c_call(c1, c4, c16, bsz, seq):
    views = [c.reshape(bsz, seq // dil, dil * C_COLS) for c, dil in zip((c1, c4, c16), DILS)]
    spec = lambda dil: pl.BlockSpec((1, seq // dil, dil * 3 * LANES), lambda b, p: (b, 0, p))
    band = _band_masks()
    assert seq // DILS[-1] >= band.shape[2]
    return pl.pallas_call(
        functools.partial(_attn_c_kernel, seq=seq),
        grid=(bsz, N_PAIRS_C),
        in_specs=[spec(dil) for dil in DILS] + [pl.BlockSpec(band.shape, lambda b, p: (0, 0, 0))],
        out_specs=pl.BlockSpec((1, seq, LANES), lambda b, p: (b, 0, p)),
        out_shape=jax.ShapeDtypeStruct((bsz, seq, C_W), F32),
        scratch_shapes=[pltpu.VMEM((seq, LANES), F32)] * 4,
        compiler_params=pltpu.CompilerParams(dimension_semantics=("parallel",) * 2,
                                             vmem_limit_bytes=VMEM_LIMIT),
    )(*views, band)


def _out_kernel(oa_ref, ob_ref, oc_ref, h_ref, ga_ref, gb_ref, gc_ref, w_ref, gf_ref, wr_ref,
                h1_ref, hn_ref, lt_ref):
    mixed = jnp.concatenate([_rms(oa_ref[...], ga_ref[...]),
                             _rms(ob_ref[...], gb_ref[...]),
                             _rms(oc_ref[...], gc_ref[...])], axis=-1).astype(BF16)
    h1 = h_ref[...] + jnp.dot(mixed, w_ref[...], preferred_element_type=F32)
    h1_ref[...] = h1
    hn = _rms(h1, gf_ref[...])
    hi = hn.astype(BF16)
    hn_ref[...] = hi
    lo = (hn - hi.astype(F32)).astype(BF16)
    both = jnp.dot(hi, wr_ref[...], preferred_element_type=F32)
    logits = (both[:, :LANES] + both[:, LANES:]
              + jnp.dot(lo, wr_ref[:, :LANES], preferred_element_type=F32))
    lt_ref[...] = logits.T[:N_EXPERTS, :]


def _out_call(oa, ob, oc, h2d, ga, gb, gc, w, gf, wr):
    t_tot = h2d.shape[0]
    row = lambda i: (i, 0)
    fixed = lambda i: (0, 0)
    rs = lambda w_: pl.BlockSpec((TM, w_), row)
    fs = lambda a: pl.BlockSpec(a.shape, fixed)
    return pl.pallas_call(
        _out_kernel,
        grid=(t_tot // TM,),
        in_specs=[rs(A_Q_W), rs(B_W), rs(C_W), rs(D_MODEL)]
                 + [fs(ga), fs(gb), fs(gc), fs(w), fs(gf), fs(wr)],
        out_specs=[rs(D_MODEL), rs(D_MODEL), pl.BlockSpec((N_EXPERTS, TM), lambda i: (0, i))],
        out_shape=[jax.ShapeDtypeStruct((t_tot, D_MODEL), F32),
                   jax.ShapeDtypeStruct((t_tot, D_MODEL), BF16),
                   jax.ShapeDtypeStruct((N_EXPERTS, t_tot), F32)],
        compiler_params=pltpu.CompilerParams(dimension_semantics=("parallel",),
                                             vmem_limit_bytes=VMEM_LIMIT),
    )(oa, ob, oc, h2d, ga, gb, gc, w, gf, wr)


def _route_kernel(lt_ref, tri_ref, blk_ref, pos_ref, gate_ref, meta_ref, *, cap):
    l = lt_ref[...]
    seq = l.shape[1]
    m = jnp.max(l, axis=0, keepdims=True)
    ex = jnp.exp(l - m)
    aff = ex / jnp.sum(ex, axis=0, keepdims=True)
    gate_ref[0] = aff
    bits = lax.bitcast_convert_type(aff, I32)

    def count(mask):
        return jnp.sum(mask.astype(F32), axis=1, keepdims=True)

    def search(i, t):
        cand = t | (1 << (30 - i))
        return jnp.where(count(bits >= cand) >= cap, cand, t)

    thr = lax.fori_loop(0, 31, search, jnp.zeros((N_EXPERTS, 1), I32))
    gt = bits > thr
    eq = bits == thr
    need = cap - count(gt)

    tri = tri_ref[...]

    def excl_prefix(mask):
        mb = jnp.where(mask, 1.0, 0.0).astype(BF16)
        parts = []
        run = jnp.zeros((N_EXPERTS, 1), F32)
        for c in range(seq // LANES):
            ch = mb[:, c * LANES:(c + 1) * LANES]
            parts.append(jnp.dot(ch, tri, preferred_element_type=F32) + run)
            run = run + jnp.sum(ch.astype(F32), axis=1, keepdims=True)
        return jnp.concatenate(parts, axis=1)

    sel = gt | (eq & (excl_prefix(eq) < need))
    pos = excl_prefix(sel)
    pos_ref[0] = jnp.where(sel, pos, -1.0).astype(I32)
    sb = jnp.where(sel, 1.0, 0.0).astype(BF16)
    sc = jnp.dot(sb, blk_ref[...], preferred_element_type=F32)
    start = sc[:, :LANES]
    cnt = sc[:, LANES:]
    ilo = jnp.minimum(jnp.floor(start * (1.0 / TOK_BLK)), cap // TOK_BLK - 1.0)
    cross = jnp.where(start + cnt > (ilo + 1.0) * TOK_BLK, 1.0, 0.0)
    lane = lax.broadcasted_iota(I32, cross.shape, 1)
    has = jnp.sum(cross, axis=1, keepdims=True)
    jstar = jnp.sum(cross * lane.astype(F32), axis=1, keepdims=True)
    ifix = cap // TOK_BLK - has
    n_tb = seq // TOK_BLK
    meta = jnp.where(lane < n_tb, ilo, jnp.where(lane == n_tb, jstar, jnp.where(lane == n_tb + 1, ifix, 0.0)))
    meta_ref[0] = meta.astype(I32)


def _route_call(lt, tri, blk, bsz, seq, cap):
    e = N_EXPERTS
    big = lambda dt: jax.ShapeDtypeStruct((bsz, e, seq), dt)
    small = jax.ShapeDtypeStruct((bsz, e, LANES), I32)
    bspec = pl.BlockSpec((1, e, seq), lambda b: (b, 0, 0))
    sspec = pl.BlockSpec((1, e, LANES), lambda b: (b, 0, 0))
    return pl.pallas_call(
        functools.partial(_route_kernel, cap=cap),
        grid=(bsz,),
        in_specs=[pl.BlockSpec((e, seq), lambda b: (0, b)),
                  pl.BlockSpec(tri.shape, lambda b: (0, 0)),
                  pl.BlockSpec(blk.shape, lambda b: (0, 0))],
        out_specs=[bspec, bspec, sspec],
        out_shape=[big(I32), big(F32), small],
        compiler_params=pltpu.CompilerParams(dimension_semantics=("parallel",),
                                             vmem_limit_bytes=VMEM_LIMIT),
    )(lt, tri, blk)


def _moe_kernel(meta_ref, hn_ref, pos_ref, gate_ref, wg_ref, wu_ref, wd_ref, o_ref,
                acc_ref, xg_ref, y_ref, *, cap, n_tb):
    b = pl.program_id(0)
    k = pl.program_id(1)
    out_rows = o_ref.shape[1]

    @pl.when(k == 0)
    def _():
        acc_ref[...] = jnp.zeros_like(acc_ref)
        y_ref[cap:, :] = jnp.zeros((TOK_BLK, y_ref.shape[1]), BF16)

    @pl.when(k < N_EXPERTS)
    def _():
        xg_ref[...] = jnp.zeros_like(xg_ref)
        slot = lax.broadcasted_iota(I32, (TOK_BLK, TOK_BLK), 0)
        mbase = (b * N_EXPERTS + k) * META_W
        jstar = meta_ref[mbase + n_tb]
        ifix = meta_ref[mbase + n_tb + 1]

        def gather(j, i):
            lp = pos_ref[0, 0, j]
            hb = hn_ref[0, pl.ds(pl.multiple_of(j * TOK_BLK, TOK_BLK), TOK_BLK), :]
            oh = jnp.where(slot + i * TOK_BLK == lp, 1.0, 0.0).astype(BF16)
            r0 = pl.multiple_of(i * TOK_BLK, TOK_BLK)
            xg_ref[pl.ds(r0, TOK_BLK), :] += jnp.dot(oh, hb, preferred_element_type=F32)

        def gather_body(j, carry):
            gather(j, meta_ref[mbase + j])
            return carry

        lax.fori_loop(0, n_tb, gather_body, 0, unroll=MOE_UNROLL)
        gather(jstar, ifix)

        x = xg_ref[:cap, :].astype(BF16)
        y = jnp.zeros((cap, D_MODEL), F32)
        d_ff = wg_ref.shape[2]
        for fc in range(d_ff // MXU_N):
            cs = slice(fc * MXU_N, (fc + 1) * MXU_N)
            g = jnp.dot(x, wg_ref[0, :, cs], preferred_element_type=F32)
            u = jnp.dot(x, wu_ref[0, :, cs], preferred_element_type=F32)
            hid = (g * jax.nn.sigmoid(g) * u).astype(BF16)
            y = y + jnp.dot(hid, wd_ref[0, cs, :], preferred_element_type=F32)
        y_ref[:cap, :] = y.astype(BF16)

        def scatter(j, i):
            lp = pos_ref[0, 0, j]
            gr = gate_ref[0, 0, j]
            t0 = pl.multiple_of(j * TOK_BLK, TOK_BLK)
            oh = jnp.where(slot + i * TOK_BLK == lp, gr, 0.0).astype(BF16)
            r0 = pl.multiple_of(i * TOK_BLK, TOK_BLK)
            acc_ref[pl.ds(t0, TOK_BLK), :] += lax.dot_general(
                oh, y_ref[pl.ds(r0, TOK_BLK), :], (((0,), (0,)), ((), ())),
                preferred_element_type=F32)

        def scatter_body(j, carry):
            scatter(j, meta_ref[mbase + j])
            return carry

        lax.fori_loop(0, n_tb, scatter_body, 0, unroll=MOE_UNROLL)
        scatter(jstar, ifix)

    @pl.when(k >= N_EXPERTS)
    def _():
        r0 = pl.multiple_of((k - N_EXPERTS) * out_rows, out_rows)
        o_ref[0] = acc_ref[pl.ds(r0, out_rows), :]


def _moe_call(meta, hn, pos, gate, wg, wu, wd, cap, layer):
    bsz, seq, d = hn.shape
    n_tb = seq // TOK_BLK
    e_idx = lambda k: jnp.minimum(k, N_EXPERTS - 1)
    tok_spec = pl.BlockSpec((1, 1, n_tb, 1, TOK_BLK), lambda b, k, *_: (b, e_idx(k), 0, 0, 0))
    w_spec = lambda w: pl.BlockSpec((1,) + w.shape[1:],
                                    lambda b, k, *_: (layer * N_EXPERTS + e_idx(k), 0, 0))
    out_rows = seq // N_OUT_CHUNKS
    grid_spec = pltpu.PrefetchScalarGridSpec(
        num_scalar_prefetch=1,
        grid=(bsz, N_EXPERTS + N_OUT_CHUNKS),
        in_specs=[pl.BlockSpec((1, seq, d), lambda b, k, *_: (b, 0, 0), pipeline_mode=pl.Buffered(1)),
                  tok_spec, tok_spec, w_spec(wg), w_spec(wu), w_spec(wd)],
        out_specs=pl.BlockSpec((1, out_rows, d),
                               lambda b, k, *_: (b, jnp.maximum(k - N_EXPERTS, 0), 0)),
        scratch_shapes=[pltpu.VMEM((seq, d), F32), pltpu.VMEM((cap + TOK_BLK, d), F32),
                        pltpu.VMEM((cap + TOK_BLK, d), BF16)],
    )
    return pl.pallas_call(
        functools.partial(_moe_kernel, cap=cap, n_tb=n_tb),
        grid_spec=grid_spec,
        out_shape=jax.ShapeDtypeStruct((bsz, seq, d), F32),
        compiler_params=pltpu.CompilerParams(dimension_semantics=("parallel", "arbitrary"),
                                             vmem_limit_bytes=VMEM_LIMIT),
    )(meta, hn, pos, gate, wg, wu, wd)


def _ple_kernel(h_ref, moe_ref, p_ref, g_ref, wg_ref, wp_ref, o_ref):
    h2 = h_ref[...] + moe_ref[...]
    a = _rms(h2, g_ref[...]).astype(BF16)
    gate = jax.nn.sigmoid(jnp.dot(a, wg_ref[...], preferred_element_type=F32))
    proj = jnp.dot(p_ref[...].astype(BF16), wp_ref[...], preferred_element_type=F32)
    o_ref[...] = h2 + gate * proj


def _ple_call(h1, moe, p2d, g, wg, wp, layer):
    t_tot = h1.shape[0]
    row = lambda i: (i, 0)
    p_row = lambda i: (layer * (t_tot // TM) + i, 0)
    fixed = lambda i: (0, 0)
    return pl.pallas_call(
        _ple_kernel,
        grid=(t_tot // TM,),
        in_specs=[pl.BlockSpec((TM, D_MODEL), row), pl.BlockSpec((TM, D_MODEL), row),
                  pl.BlockSpec((TM, p2d.shape[1]), p_row), pl.BlockSpec((1, D_MODEL), fixed),
                  pl.BlockSpec(wg.shape, fixed), pl.BlockSpec(wp.shape, fixed)],
        out_specs=pl.BlockSpec((TM, D_MODEL), row),
        out_shape=jax.ShapeDtypeStruct((t_tot, D_MODEL), F32),
        compiler_params=pltpu.CompilerParams(dimension_semantics=("parallel",),
                                             vmem_limit_bytes=VMEM_LIMIT),
    )(h1, moe, p2d, g, wg, wp)


def _rope_tables(seq):
    pos = jnp.arange(seq)

    def angles(pos_f, n, theta):
        inv = theta ** (-jnp.arange(0, n, 2, dtype=F32) / n)
        return pos_f[:, None] * inv[None, :]

    ar = angles((pos // GRID_W).astype(F32), HEAD_DIM // 2, AXIAL_THETA)
    ac = angles((pos % GRID_W).astype(F32), HEAD_DIM // 2, AXIAL_THETA)
    a1 = angles(pos.astype(F32), ROPE_DIMS, ROPE_THETA)
    z16 = jnp.zeros_like(ar)
    rep = LANES // HEAD_DIM
    cos_a = jnp.tile(jnp.concatenate([jnp.cos(ar), jnp.cos(ar), jnp.cos(ac), jnp.cos(ac)], -1), (1, rep))
    sm_a = jnp.tile(jnp.concatenate([-jnp.sin(ar), z16, -jnp.sin(ac), z16], -1), (1, rep))
    sp_a = jnp.tile(jnp.concatenate([z16, jnp.sin(ar), z16, jnp.sin(ac)], -1), (1, rep))
    z8 = jnp.zeros_like(a1)
    rest = HEAD_DIM - ROPE_DIMS
    ones = jnp.ones((seq, rest), F32)
    zr = jnp.zeros((seq, rest), F32)
    cos_c = jnp.tile(jnp.concatenate([jnp.cos(a1), jnp.cos(a1), ones], -1), (1, rep))
    sm_c = jnp.tile(jnp.concatenate([-jnp.sin(a1), z8, zr], -1), (1, rep))
    sp_c = jnp.tile(jnp.concatenate([z8, jnp.sin(a1), zr], -1), (1, rep))
    return cos_a, sm_a, sp_a, cos_c, sm_c, sp_c


_A_HEAD_ORDER = (0, 2, 1, 3)
_A_PERM = np.concatenate([np.arange(h * HEAD_DIM, (h + 1) * HEAD_DIM) for h in _A_HEAD_ORDER])


def _gain_row(qg):
    scale = HEAD_DIM ** -0.5
    one = lambda n: jnp.ones((n,), F32)
    return jnp.concatenate([
        jnp.tile(qg[0, 0] * scale, A_Q_HEADS), jnp.tile(qg[0, 1], A_KV_HEADS), one(A_KV_W),
        jnp.tile(qg[1, 0] * scale, B_HEADS), jnp.tile(qg[1, 1], B_HEADS), one(B_W),
        jnp.tile(qg[2, 0] * scale, C_HEADS), jnp.tile(qg[2, 1], C_HEADS), one(C_W)])[None, :]


def kernel(x, p, g_mix, w_in, qk_gain, na_bias, g_out, w_out, g_ffn, w_router,
           w_gate, w_up, w_down, g_ple, w_ple_gate, w_ple_proj):
    bsz, seq, d = x.shape
    depth = w_in.shape[0]
    t_tot = bsz * seq
    cap = max(1, EC_CAPACITY * seq // N_EXPERTS)
    n_tb = seq // TOK_BLK

    tabs = _rope_tables(seq)
    lane = np.arange(LANES)
    col = np.arange(MXU_N)
    gsum = jnp.asarray((col[:, None] // HEAD_DIM) == (col[None, :] // HEAD_DIM), BF16)
    tri = jnp.asarray(lane[:, None] < lane[None, :], BF16)
    tok = np.arange(seq)[:, None]
    j = np.arange(LANES)[None, :]
    blk_start = (tok < j * TOK_BLK) & (j < n_tb)
    blk_count = (tok // TOK_BLK == j) & (j < n_tb)
    blk = jnp.asarray(np.concatenate([blk_start, blk_count], axis=1), BF16)

    bias = _na_bias_tiles(na_bias, seq // GRID_W)
    stack = lambda w: w.astype(BF16).reshape((depth * N_EXPERTS,) + w.shape[2:])
    wg_all, wu_all, wd_all = stack(w_gate), stack(w_up), stack(w_down)
    p_all = p.reshape(depth * t_tot, p.shape[-1])
    h = x.reshape(t_tot, d)
    for i in range(depth):
        w_i = jnp.concatenate([w_in[i][:, _A_PERM], w_in[i][:, A_Q_W:]], axis=1).astype(BF16)
        ab, c1, c4, c16 = _proj_call(h, g_mix[i][None], w_i, _gain_row(qk_gain[i]), gsum, tabs, seq)
        ab = ab.reshape(bsz, seq, AB_COLS)
        oa = _attn_a_call(ab).reshape(t_tot, A_Q_W)
        ob = _attn_b_call(ab, bias[i]).reshape(t_tot, B_W)
        oc = _attn_c_call(c1, c4, c16, bsz, seq).reshape(t_tot, C_W)
        go = g_out[i]
        w_o = jnp.concatenate([w_out[i][_A_PERM], w_out[i][A_Q_W:]], axis=0).astype(BF16)
        wr = jnp.pad(w_router[i], ((0, 0), (0, LANES - N_EXPERTS)))
        wr_hi = wr.astype(BF16)
        wr_lo = (wr - wr_hi.astype(F32)).astype(BF16)
        h1, hn, lt = _out_call(oa, ob, oc, h, go[:A_Q_W][_A_PERM][None], go[None, A_Q_W:A_Q_W + B_W],
                               go[None, A_Q_W + B_W:], w_o, g_ffn[i][None],
                               jnp.concatenate([wr_hi, wr_lo], axis=1))
        pos, gate, meta = _route_call(lt, tri, blk, bsz, seq, cap)
        moe = _moe_call(meta[:, :, :META_W].reshape(-1),
                        hn.reshape(bsz, seq, d),
                        pos.reshape(bsz, N_EXPERTS, n_tb, 1, TOK_BLK),
                        gate.reshape(bsz, N_EXPERTS, n_tb, 1, TOK_BLK),
                        wg_all, wu_all, wd_all, cap, i)
        h = _ple_call(h1, moe.reshape(t_tot, d), p_all, g_ple[i][None],
                      w_ple_gate[i].astype(BF16), w_ple_proj[i].astype(BF16), i)
    return h.reshape(bsz, seq, d)
```

```python
import functools

import numpy as np
import jax
import jax.numpy as jnp
from jax import lax
from jax.experimental import pallas as pl
from jax.experimental.pallas import tpu as pltpu

F32 = jnp.float32
BF16 = jnp.bfloat16
I32 = jnp.int32

D_MODEL = 1024
HEAD_DIM = 64
A_Q_HEADS = 4
A_KV_HEADS = 2
B_HEADS = 6
C_HEADS = 6
A_Q_W = A_Q_HEADS * HEAD_DIM
A_KV_W = A_KV_HEADS * HEAD_DIM
B_W = B_HEADS * HEAD_DIM
C_W = C_HEADS * HEAD_DIM
IN_COLS = A_Q_W + 2 * A_KV_W + 3 * B_W + 3 * C_W
GRID_W = 64
AXIAL_THETA = 10000.0
NA_ROWS = 8
NA_COLS = 16
C_BRANCHES = ((128, 1), (512, 4), (2048, 16))
ROPE_THETA = 500000.0
ROPE_DIMS = HEAD_DIM // 4
N_EXPERTS = 16
EC_CAPACITY = 2
EPS = 1e-6
NEG_INF = -1e30
LOG2_E = 1.4426950408889634

LANES = 128
MXU_N = 256
VMEM_LIMIT = 56 * 1024 * 1024

N_CB = IN_COLS // LANES
CB_QA, CB_KA, CB_VA = 0, 2, 3
CB_QB, CB_KB, CB_VB = 4, 7, 10
CB_QC, CB_KC, CB_VC = 13, 16, 19
CB_KIND = ("A", "A", "A", "V") + ("N",) * 6 + ("V",) * 3 + ("C",) * 6 + ("V",) * 3
AB_COLS = CB_QC * LANES
C_COLS = 3 * C_W
N_PAIRS_C = C_W // LANES
DILS = tuple(d for _, d in C_BRANCHES)

TM = 512
TM_CHAIN = 256
TQ_A = 512
TQ_A_CHAIN = 128
NA_QROWS = 4
NA_KROWS = 12
NA_UNROLL = 7
TQ_C = 128
C_UNROLL = 10
TOK_BLK = 256
MOE_UNROLL = 16
N_OUT_CHUNKS = 8
META_W = 32


def _rms(x, g):
    return x * lax.rsqrt(jnp.mean(x * x, axis=-1, keepdims=True) + EPS) * g


def _proj_kernel(h_ref, g_ref, w_ref, gain_ref, gsum_ref, ca_ref, sma_ref, spa_ref,
                 cc_ref, smc_ref, spc_ref, ab_ref, c1_ref, c4_ref, c16_ref, cs_ref):
    gsum = gsum_ref[...]
    per = MXU_N // LANES
    n_c = IN_COLS // MXU_N
    normed = [c for c in range(n_c) if any(CB_KIND[cb] != "V" for cb in range(c * per, (c + 1) * per))]
    for r0 in range(0, TM, TM_CHAIN):
        rows = slice(r0, r0 + TM_CHAIN)
        a = _rms(h_ref[rows, :], g_ref[...]).astype(BF16)
        accs = [jnp.dot(a, w_ref[:, c * MXU_N:(c + 1) * MXU_N], preferred_element_type=F32)
                for c in range(n_c)]
        sq = jnp.concatenate([accs[c] * accs[c] for c in normed], axis=0)
        hi = sq.astype(BF16)
        lo = (sq - hi.astype(F32)).astype(BF16)
        ss = jnp.dot(hi, gsum, preferred_element_type=F32) + jnp.dot(lo, gsum, preferred_element_type=F32)
        inv_all = lax.rsqrt(ss * (1.0 / HEAD_DIM) + EPS)
        for c in range(n_c):
            acc = accs[c]
            if c in normed:
                inv = inv_all[normed.index(c) * TM_CHAIN:(normed.index(c) + 1) * TM_CHAIN]
            for hf, cb in enumerate(range(c * per, (c + 1) * per)):
                t = acc[:, hf * LANES:(hf + 1) * LANES]
                kind = CB_KIND[cb]
                if kind != "V":
                    t = t * inv[:, hf * LANES:(hf + 1) * LANES] * gain_ref[:, cb * LANES:(cb + 1) * LANES]
                    if kind == "A":
                        sh = HEAD_DIM // 4
                        t = (t * ca_ref[rows, :] + pltpu.roll(t, LANES - sh, 1) * sma_ref[rows, :]
                             + pltpu.roll(t, sh, 1) * spa_ref[rows, :])
                    elif kind == "C":
                        sh = ROPE_DIMS // 2
                        t = (t * cc_ref[rows, :] + pltpu.roll(t, LANES - sh, 1) * smc_ref[rows, :]
                             + pltpu.roll(t, sh, 1) * spc_ref[rows, :])
                if cb < CB_QC:
                    ab_ref[rows, cb * LANES:(cb + 1) * LANES] = t.astype(BF16)
                else:
                    cs_ref[cb - CB_QC, rows, :] = t
        for p in range(N_PAIRS_C):
            for which in range(3):
                src = which * N_PAIRS_C + p
                for dil, ref in zip(DILS, (c1_ref, c4_ref, c16_ref)):
                    n = TM_CHAIN // dil
                    for r in range(dil):
                        dst = ((p * dil + r) * 3 + which) * LANES
                        ref[r0 // dil:r0 // dil + n, dst:dst + LANES] = (
                            cs_ref[src, pl.ds(r0 + r, n, stride=dil), :].astype(BF16))


def _proj_call(h2d, g, w, gain, gsum, tabs, seq):
    t_tot = h2d.shape[0]
    nseq = seq // TM
    row = lambda i: (i, 0)
    fixed = lambda i: (0, 0)
    tab = lambda i: (i % nseq, 0)
    return pl.pallas_call(
        _proj_kernel,
        grid=(t_tot // TM,),
        in_specs=[pl.BlockSpec((TM, D_MODEL), row),
                  pl.BlockSpec((1, D_MODEL), fixed),
                  pl.BlockSpec((D_MODEL, IN_COLS), fixed),
                  pl.BlockSpec((1, IN_COLS), fixed),
                  pl.BlockSpec((MXU_N, MXU_N), fixed)]
                 + [pl.BlockSpec((TM, LANES), tab)] * 6,
        out_specs=[pl.BlockSpec((TM, AB_COLS), row)]
                  + [pl.BlockSpec((TM // dil, dil * C_COLS), row) for dil in DILS],
        out_shape=[jax.ShapeDtypeStruct((t_tot, AB_COLS), BF16)]
                  + [jax.ShapeDtypeStruct((t_tot // dil, dil * C_COLS), BF16) for dil in DILS],
        scratch_shapes=[pltpu.VMEM((C_COLS // LANES, TM, LANES), F32)],
        compiler_params=pltpu.CompilerParams(dimension_semantics=("parallel",),
                                             vmem_limit_bytes=VMEM_LIMIT),
    )(h2d, g, w, gain, gsum, *tabs)


def _stack_heads(q):
    qf = q.astype(F32)
    lo = lax.broadcasted_iota(I32, qf.shape, 1) < HEAD_DIM
    return jnp.concatenate([jnp.where(lo, qf, 0.0), jnp.where(lo, 0.0, qf)], axis=0).astype(BF16)


def _merge_heads(o):
    n = o.shape[0] // 2
    lo = lax.broadcasted_iota(I32, (n, LANES), 1) < HEAD_DIM
    return jnp.where(lo, o[:n], o[n:])


def _scores(qs, k):
    return lax.dot_general(qs, k, (((1,), (1,)), ((), ())), preferred_element_type=F32)


def _attn_a_kernel(q_ref, k_ref, v_ref, o_ref):
    k = k_ref[0]
    v = v_ref[0]
    for rc in range(TQ_A // TQ_A_CHAIN):
        rows = slice(rc * TQ_A_CHAIN, (rc + 1) * TQ_A_CHAIN)
        for blk in range(A_Q_W // LANES):
            qs = _stack_heads(q_ref[0, rows, blk * LANES:(blk + 1) * LANES])
            s = _scores(qs, k)
            m = jnp.max(s, axis=-1, keepdims=True)
            p = jnp.exp2(s - m)
            l = jnp.sum(p, axis=-1, keepdims=True)
            o = jnp.dot(p.astype(BF16), v, preferred_element_type=F32) / l
            o_ref[0, rows, blk * LANES:(blk + 1) * LANES] = _merge_heads(o)


def _attn_a_call(qkv):
    bsz, seq, _ = qkv.shape
    return pl.pallas_call(
        _attn_a_kernel,
        grid=(bsz, seq // TQ_A),
        in_specs=[pl.BlockSpec((1, TQ_A, A_Q_W), lambda b, i: (b, i, 0)),
                  pl.BlockSpec((1, seq, LANES), lambda b, i: (b, 0, CB_KA)),
                  pl.BlockSpec((1, seq, LANES), lambda b, i: (b, 0, CB_VA))],
        out_specs=pl.BlockSpec((1, TQ_A, A_Q_W), lambda b, i: (b, i, 0)),
        out_shape=jax.ShapeDtypeStruct((bsz, seq, A_Q_W), F32),
        compiler_params=pltpu.CompilerParams(dimension_semantics=("parallel", "parallel"),
                                             vmem_limit_bytes=VMEM_LIMIT),
    )(qkv, qkv, qkv)


def _attn_b_kernel(q_ref, k_ref, v_ref, bias_ref, o_ref, *, n_rows):
    tq = NA_QROWS * GRID_W
    n_g = n_rows // NA_QROWS

    def group(g, cfg):
        q0 = pl.multiple_of(g * tq, tq)
        krow0 = jnp.clip(g * NA_QROWS - NA_ROWS // 2, 0, n_rows - NA_KROWS)
        t0 = pl.multiple_of(krow0 * GRID_W, GRID_W)
        kw = k_ref[0, pl.ds(t0, NA_KROWS * GRID_W), :]
        vw = v_ref[0, pl.ds(t0, NA_KROWS * GRID_W), :]
        s = _scores(_stack_heads(q_ref[0, pl.ds(q0, tq), :]), kw) + bias_ref[0, cfg]
        m = jnp.max(s, axis=-1, keepdims=True)
        p = jnp.exp2(s - m)
        l = jnp.sum(p, axis=-1, keepdims=True)
        o = jnp.dot(p.astype(BF16), vw, preferred_element_type=F32) / l
        o_ref[0, pl.ds(q0, tq), :] = _merge_heads(o)

    def interior(g, carry):
        group(g, 1)
        return carry

    group(0, 0)
    lax.fori_loop(1, n_g - 1, interior, 0, unroll=NA_UNROLL)
    group(n_g - 1, 2)


def _attn_b_call(qkv, bias):
    bsz, seq, _ = qkv.shape
    spec = lambda cb: pl.BlockSpec((1, seq, LANES), lambda p, b: (b, 0, cb + p))
    return pl.pallas_call(
        functools.partial(_attn_b_kernel, n_rows=seq // GRID_W),
        grid=(B_W // LANES, bsz),
        in_specs=[spec(CB_QB), spec(CB_KB), spec(CB_VB),
                  pl.BlockSpec((1,) + bias.shape[1:], lambda p, b: (p, 0, 0, 0))],
        out_specs=pl.BlockSpec((1, seq, LANES), lambda p, b: (b, 0, p)),
        out_shape=jax.ShapeDtypeStruct((bsz, seq, B_W), F32),
        compiler_params=pltpu.CompilerParams(dimension_semantics=("parallel",) * 2,
                                             vmem_limit_bytes=VMEM_LIMIT),
    )(qkv, qkv, qkv, bias)


def _na_bias_tiles(rpb, n_rows):
    wr = NA_ROWS
    n_ro, n_co = 2 * NA_ROWS - 1, 2 * NA_COLS - 1
    row_sel = np.zeros((3, NA_QROWS, NA_KROWS, n_ro), np.float32)
    for z, r0 in enumerate((0, NA_QROWS, n_rows - NA_QROWS)):
        start = int(np.clip(r0 - wr // 2, 0, n_rows - NA_KROWS))
        for qi in range(NA_QROWS):
            r = r0 + qi
            rs = int(np.clip(r - wr // 2, 0, n_rows - wr))
            for kj in range(NA_KROWS):
                krow = start + kj
                if rs <= krow < rs + wr:
                    row_sel[z, qi, kj, krow - r + NA_ROWS - 1] = 1.0
    col_sel = np.zeros((GRID_W, GRID_W, n_co), np.float32)
    for c in range(GRID_W):
        cstart = int(np.clip(c - NA_COLS // 2, 0, GRID_W - NA_COLS))
        for kc in range(cstart, cstart + NA_COLS):
            col_sel[c, kc, kc - c + NA_COLS - 1] = 1.0
    ok = (row_sel.sum(-1)[:, :, None, :, None] * col_sel.sum(-1)[None, None, :, None, :]) > 0
    outside = np.where(ok, 0.0, NEG_INF).astype(np.float32)[None, None, :, None]
    n_l = rpb.shape[0]
    pairs = (rpb.astype(F32) * LOG2_E).reshape(n_l, B_HEADS // 2, 2, n_ro, n_co)
    vals = jnp.einsum("zqka,lptab,cjb->lpztqckj", row_sel, pairs, col_sel,
                      precision=lax.Precision.HIGHEST) + outside
    return vals.reshape(n_l, B_HEADS // 2, 3, 2 * NA_QROWS * GRID_W, NA_KROWS * GRID_W)


def _band_tile(q, kw, vw, band):
    s = _scores(_stack_heads(q), kw) + band
    m = jnp.max(s, axis=-1, keepdims=True)
    p = jnp.exp2(s - m)
    l = jnp.sum(p, axis=-1, keepdims=True)
    o = jnp.dot(p.astype(BF16), vw, preferred_element_type=F32) / l
    lse = jnp.broadcast_to(m + jnp.log2(l), (2 * TQ_C, LANES))
    return _merge_heads(o), _merge_heads(lse)


def _attn_c_kernel(c1_ref, c4_ref, c16_ref, band_ref, o_ref, o2_ref, l2_ref, o3_ref, l3_ref, *, seq):
    def branch(ref, window, dil, r, emit):
        length = seq // dil
        half = window // 2 // dil
        n_q = length // TQ_C
        win = min(TQ_C + 2 * half, length)
        base = r * 3 * LANES

        def tile(qi, band):
            if isinstance(qi, int):
                q0 = qi * TQ_C
                ks = min(max(q0 - half, 0), length - win)
            else:
                q0 = pl.multiple_of(qi * TQ_C, TQ_C)
                ks = pl.multiple_of(jnp.clip(q0 - half, 0, length - win), half)
            o, lse = _band_tile(ref[0, pl.ds(q0, TQ_C), base:base + LANES],
                                ref[0, pl.ds(ks, win), base + LANES:base + 2 * LANES],
                                ref[0, pl.ds(ks, win), base + 2 * LANES:base + 3 * LANES], band)
            emit(q0, o, lse)

        def interior(qi, carry):
            tile(qi, band_ref[1])
            return carry

        tile(0, band_ref[0])
        if n_q > 2:
            lax.fori_loop(1, n_q - 1, interior, 0, unroll=min(C_UNROLL, n_q - 2))
        tile(n_q - 1, band_ref[2])

    for ref, (window, dil), o_scr, l_scr in ((c4_ref, C_BRANCHES[1], o2_ref, l2_ref),
                                            (c16_ref, C_BRANCHES[2], o3_ref, l3_ref)):
        for r in range(dil):
            def emit(q0, o, lse, r=r, dil=dil, o_scr=o_scr, l_scr=l_scr):
                rows = pl.ds(r + dil * q0, TQ_C, stride=dil)
                o_scr[rows, :] = o
                l_scr[rows, :] = lse
            branch(ref, window, dil, r, emit)

    def emit1(q0, o1, l1):
        rows = pl.ds(q0, TQ_C)
        l2, l3 = l2_ref[rows, :], l3_ref[rows, :]
        lm = jnp.maximum(jnp.maximum(l1, l2), l3)
        e1, e2, e3 = jnp.exp2(l1 - lm), jnp.exp2(l2 - lm), jnp.exp2(l3 - lm)
        o_ref[0, rows, :] = (e1 * o1 + e2 * o2_ref[rows, :] + e3 * o3_ref[rows, :]) / (e1 + e2 + e3)

    branch(c1_ref, C_BRANCHES[0][0], 1, 0, emit1)


def _band_masks():
    half = C_BRANCHES[0][0] // 2
    assert all(w // 2 // d == half for w, d in C_BRANCHES)
    i = np.arange(2 * TQ_C)[:, None] % TQ_C
    j = np.arange(TQ_C + 2 * half)[None, :]
    return jnp.asarray(np.stack([np.where(np.abs(off + i - j) <= half, 0.0, NEG_INF)
                                 for off in (0, half, 2 * half)]).astype(np.float32))


def _attn_c_call(c1, c4, c16, bsz, seq):
    views = [c.reshape(bsz, seq // dil, dil * C_COLS) for c, dil in zip((c1, c4, c16), DILS)]
    spec = lambda dil: pl.BlockSpec((1, seq // dil, dil * 3 * LANES), lambda b, p: (b, 0, p))
    band = _band_masks()
    assert seq // DILS[-1] >= band.shape[2]
    return pl.pallas_call(
        functools.partial(_attn_c_kernel, seq=seq),
        grid=(bsz, N_PAIRS_C),
        in_specs=[spec(dil) for dil in DILS] + [pl.BlockSpec(band.shape, lambda b, p: (0, 0, 0))],
        out_specs=pl.BlockSpec((1, seq, LANES), lambda b, p: (b, 0, p)),
        out_shape=jax.ShapeDtypeStruct((bsz, seq, C_W), F32),
        scratch_shapes=[pltpu.VMEM((seq, LANES), F32)] * 4,
        compiler_params=pltpu.CompilerParams(dimension_semantics=("parallel",) * 2,
                                             vmem_limit_bytes=VMEM_LIMIT),
    )(*views, band)


def _out_kernel(oa_ref, ob_ref, oc_ref, h_ref, ga_ref, gb_ref, gc_ref, w_ref, gf_ref, wr_ref,
                h1_ref, hn_ref, lt_ref):
    mixed = jnp.concatenate([_rms(oa_ref[...], ga_ref[...]),
                             _rms(ob_ref[...], gb_ref[...]),
                             _rms(oc_ref[...], gc_ref[...])], axis=-1).astype(BF16)
    h1 = h_ref[...] + jnp.dot(mixed, w_ref[...], preferred_element_type=F32)
    h1_ref[...] = h1
    hn = _rms(h1, gf_ref[...])
    hi = hn.astype(BF16)
    hn_ref[...] = hi
    lo = (hn - hi.astype(F32)).astype(BF16)
    both = jnp.dot(hi, wr_ref[...], preferred_element_type=F32)
    logits = (both[:, :LANES] + both[:, LANES:]
              + jnp.dot(lo, wr_ref[:, :LANES], preferred_element_type=F32))
    lt_ref[...] = logits.T[:N_EXPERTS, :]


def _out_call(oa, ob, oc, h2d, ga, gb, gc, w, gf, wr):
    t_tot = h2d.shape[0]
    row = lambda i: (i, 0)
    fixed = lambda i: (0, 0)
    rs = lambda w_: pl.BlockSpec((TM, w_), row)
    fs = lambda a: pl.BlockSpec(a.shape, fixed)
    return pl.pallas_call(
        _out_kernel,
        grid=(t_tot // TM,),
        in_specs=[rs(A_Q_W), rs(B_W), rs(C_W), rs(D_MODEL)]
                 + [fs(ga), fs(gb), fs(gc), fs(w), fs(gf), fs(wr)],
        out_specs=[rs(D_MODEL), rs(D_MODEL), pl.BlockSpec((N_EXPERTS, TM), lambda i: (0, i))],
        out_shape=[jax.ShapeDtypeStruct((t_tot, D_MODEL), F32),
                   jax.ShapeDtypeStruct((t_tot, D_MODEL), BF16),
                   jax.ShapeDtypeStruct((N_EXPERTS, t_tot), F32)],
        compiler_params=pltpu.CompilerParams(dimension_semantics=("parallel",),
                                             vmem_limit_bytes=VMEM_LIMIT),
    )(oa, ob, oc, h2d, ga, gb, gc, w, gf, wr)


def _route_kernel(lt_ref, tri_ref, blk_ref, pos_ref, gate_ref, meta_ref, *, cap):
    l = lt_ref[...]
    seq = l.shape[1]
    m = jnp.max(l, axis=0, keepdims=True)
    ex = jnp.exp(l - m)
    aff = ex / jnp.sum(ex, axis=0, keepdims=True)
    gate_ref[0] = aff
    bits = lax.bitcast_convert_type(aff, I32)

    def count(mask):
        return jnp.sum(mask.astype(F32), axis=1, keepdims=True)

    def search(i, t):
        cand = t | (1 << (30 - i))
        return jnp.where(count(bits >= cand) >= cap, cand, t)

    thr = lax.fori_loop(0, 31, search, jnp.zeros((N_EXPERTS, 1), I32))
    gt = bits > thr
    eq = bits == thr
    need = cap - count(gt)

    tri = tri_ref[...]

    def excl_prefix(mask):
        mb = jnp.where(mask, 1.0, 0.0).astype(BF16)
        parts = []
        run = jnp.zeros((N_EXPERTS, 1), F32)
        for c in range(seq // LANES):
            ch = mb[:, c * LANES:(c + 1) * LANES]
            parts.append(jnp.dot(ch, tri, preferred_element_type=F32) + run)
            run = run + jnp.sum(ch.astype(F32), axis=1, keepdims=True)
        return jnp.concatenate(parts, axis=1)

    sel = gt | (eq & (excl_prefix(eq) < need))
    pos = excl_prefix(sel)
    pos_ref[0] = jnp.where(sel, pos, -1.0).astype(I32)
    sb = jnp.where(sel, 1.0, 0.0).astype(BF16)
    sc = jnp.dot(sb, blk_ref[...], preferred_element_type=F32)
    start = sc[:, :LANES]
    cnt = sc[:, LANES:]
    ilo = jnp.minimum(jnp.floor(start * (1.0 / TOK_BLK)), cap // TOK_BLK - 1.0)
    cross = jnp.where(start + cnt > (ilo + 1.0) * TOK_BLK, 1.0, 0.0)
    lane = lax.broadcasted_iota(I32, cross.shape, 1)
    has = jnp.sum(cross, axis=1, keepdims=True)
    jstar = jnp.sum(cross * lane.astype(F32), axis=1, keepdims=True)
    ifix = cap // TOK_BLK - has
    n_tb = seq // TOK_BLK
    meta = jnp.where(lane < n_tb, ilo, jnp.where(lane == n_tb, jstar, jnp.where(lane == n_tb + 1, ifix, 0.0)))
    meta_ref[0] = meta.astype(I32)


def _route_call(lt, tri, blk, bsz, seq, cap):
    e = N_EXPERTS
    big = lambda dt: jax.ShapeDtypeStruct((bsz, e, seq), dt)
    small = jax.ShapeDtypeStruct((bsz, e, LANES), I32)
    bspec = pl.BlockSpec((1, e, seq), lambda b: (b, 0, 0))
    sspec = pl.BlockSpec((1, e, LANES), lambda b: (b, 0, 0))
    return pl.pallas_call(
        functools.partial(_route_kernel, cap=cap),
        grid=(bsz,),
        in_specs=[pl.BlockSpec((e, seq), lambda b: (0, b)),
                  pl.BlockSpec(tri.shape, lambda b: (0, 0)),
                  pl.BlockSpec(blk.shape, lambda b: (0, 0))],
        out_specs=[bspec, bspec, sspec],
        out_shape=[big(I32), big(F32), small],
        compiler_params=pltpu.CompilerParams(dimension_semantics=("parallel",),
                                             vmem_limit_bytes=VMEM_LIMIT),
    )(lt, tri, blk)


def _moe_kernel(meta_ref, hn_ref, pos_ref, gate_ref, wg_ref, wu_ref, wd_ref, o_ref,
                acc_ref, xg_ref, y_ref, *, cap, n_tb):
    b = pl.program_id(0)
    k = pl.program_id(1)
    out_rows = o_ref.shape[1]

    @pl.when(k == 0)
    def _():
        acc_ref[...] = jnp.zeros_like(acc_ref)
        y_ref[cap:, :] = jnp.zeros((TOK_BLK, y_ref.shape[1]), BF16)

    @pl.when(k < N_EXPERTS)
    def _():
        xg_ref[...] = jnp.zeros_like(xg_ref)
        slot = lax.broadcasted_iota(I32, (TOK_BLK, TOK_BLK), 0)
        mbase = (b * N_EXPERTS + k) * META_W
        jstar = meta_ref[mbase + n_tb]
        ifix = meta_ref[mbase + n_tb + 1]

        def gather(j, i):
            lp = pos_ref[0, 0, j]
            hb = hn_ref[0, pl.ds(pl.multiple_of(j * TOK_BLK, TOK_BLK), TOK_BLK), :]
            oh = jnp.where(slot + i * TOK_BLK == lp, 1.0, 0.0).astype(BF16)
            r0 = pl.multiple_of(i * TOK_BLK, TOK_BLK)
            xg_ref[pl.ds(r0, TOK_BLK), :] += jnp.dot(oh, hb, preferred_element_type=F32)

        def gather_body(j, carry):
            gather(j, meta_ref[mbase + j])
            return carry

        lax.fori_loop(0, n_tb, gather_body, 0, unroll=MOE_UNROLL)
        gather(jstar, ifix)

        x = xg_ref[:cap, :].astype(BF16)
        y = jnp.zeros((cap, D_MODEL), F32)
        d_ff = wg_ref.shape[2]
        for fc in range(d_ff // MXU_N):
            cs = slice(fc * MXU_N, (fc + 1) * MXU_N)
            g = jnp.dot(x, wg_ref[0, :, cs], preferred_element_type=F32)
            u = jnp.dot(x, wu_ref[0, :, cs], preferred_element_type=F32)
            hid = (g * jax.nn.sigmoid(g) * u).astype(BF16)
            y = y + jnp.dot(hid, wd_ref[0, cs, :], preferred_element_type=F32)
        y_ref[:cap, :] = y.astype(BF16)

        def scatter(j, i):
            lp = pos_ref[0, 0, j]
            gr = gate_ref[0, 0, j]
            t0 = pl.multiple_of(j * TOK_BLK, TOK_BLK)
            oh = jnp.where(slot + i * TOK_BLK == lp, gr, 0.0).astype(BF16)
            r0 = pl.multiple_of(i * TOK_BLK, TOK_BLK)
            acc_ref[pl.ds(t0, TOK_BLK), :] += lax.dot_general(
                oh, y_ref[pl.ds(r0, TOK_BLK), :], (((0,), (0,)), ((), ())),
                preferred_element_type=F32)

        def scatter_body(j, carry):
            scatter(j, meta_ref[mbase + j])
            return carry

        lax.fori_loop(0, n_tb, scatter_body, 0, unroll=MOE_UNROLL)
        scatter(jstar, ifix)

    @pl.when(k >= N_EXPERTS)
    def _():
        r0 = pl.multiple_of((k - N_EXPERTS) * out_rows, out_rows)
        o_ref[0] = acc_ref[pl.ds(r0, out_rows), :]


def _moe_call(meta, hn, pos, gate, wg, wu, wd, cap, layer):
    bsz, seq, d = hn.shape
    n_tb = seq // TOK_BLK
    e_idx = lambda k: jnp.minimum(k, N_EXPERTS - 1)
    tok_spec = pl.BlockSpec((1, 1, n_tb, 1, TOK_BLK), lambda b, k, *_: (b, e_idx(k), 0, 0, 0))
    w_spec = lambda w: pl.BlockSpec((1,) + w.shape[1:],
                                    lambda b, k, *_: (layer * N_EXPERTS + e_idx(k), 0, 0))
    out_rows = seq // N_OUT_CHUNKS
    grid_spec = pltpu.PrefetchScalarGridSpec(
        num_scalar_prefetch=1,
        grid=(bsz, N_EXPERTS + N_OUT_CHUNKS),
        in_specs=[pl.BlockSpec((1, seq, d), lambda b, k, *_: (b, 0, 0), pipeline_mode=pl.Buffered(1)),
                  tok_spec, tok_spec, w_spec(wg), w_spec(wu), w_spec(wd)],
        out_specs=pl.BlockSpec((1, out_rows, d),
                               lambda b, k, *_: (b, jnp.maximum(k - N_EXPERTS, 0), 0)),
        scratch_shapes=[pltpu.VMEM((seq, d), F32), pltpu.VMEM((cap + TOK_BLK, d), F32),
                        pltpu.VMEM((cap + TOK_BLK, d), BF16)],
    )
    return pl.pallas_call(
        functools.partial(_moe_kernel, cap=cap, n_tb=n_tb),
        grid_spec=grid_spec,
        out_shape=jax.ShapeDtypeStruct((bsz, seq, d), F32),
        compiler_params=pltpu.CompilerParams(dimension_semantics=("parallel", "arbitrary"),
                                             vmem_limit_bytes=VMEM_LIMIT),
    )(meta, hn, pos, gate, wg, wu, wd)


def _ple_kernel(h_ref, moe_ref, p_ref, g_ref, wg_ref, wp_ref, o_ref):
    h2 = h_ref[...] + moe_ref[...]
    a = _rms(h2, g_ref[...]).astype(BF16)
    gate = jax.nn.sigmoid(jnp.dot(a, wg_ref[...], preferred_element_type=F32))
    proj = jnp.dot(p_ref[...].astype(BF16), wp_ref[...], preferred_element_type=F32)
    o_ref[...] = h2 + gate * proj


def _ple_call(h1, moe, p2d, g, wg, wp, layer):
    t_tot = h1.shape[0]
    row = lambda i: (i, 0)
    p_row = lambda i: (layer * (t_tot // TM) + i, 0)
    fixed = lambda i: (0, 0)
    return pl.pallas_call(
        _ple_kernel,
        grid=(t_tot // TM,),
        in_specs=[pl.BlockSpec((TM, D_MODEL), row), pl.BlockSpec((TM, D_MODEL), row),
                  pl.BlockSpec((TM, p2d.shape[1]), p_row), pl.BlockSpec((1, D_MODEL), fixed),
                  pl.BlockSpec(wg.shape, fixed), pl.BlockSpec(wp.shape, fixed)],
        out_specs=pl.BlockSpec((TM, D_MODEL), row),
        out_shape=jax.ShapeDtypeStruct((t_tot, D_MODEL), F32),
        compiler_params=pltpu.CompilerParams(dimension_semantics=("parallel",),
                                             vmem_limit_bytes=VMEM_LIMIT),
    )(h1, moe, p2d, g, wg, wp)


def _rope_tables(seq):
    pos = jnp.arange(seq)

    def angles(pos_f, n, theta):
        inv = theta ** (-jnp.arange(0, n, 2, dtype=F32) / n)
        return pos_f[:, None] * inv[None, :]

    ar = angles((pos // GRID_W).astype(F32), HEAD_DIM // 2, AXIAL_THETA)
    ac = angles((pos % GRID_W).astype(F32), HEAD_DIM // 2, AXIAL_THETA)
    a1 = angles(pos.astype(F32), ROPE_DIMS, ROPE_THETA)
    z16 = jnp.zeros_like(ar)
    rep = LANES // HEAD_DIM
    cos_a = jnp.tile(jnp.concatenate([jnp.cos(ar), jnp.cos(ar), jnp.cos(ac), jnp.cos(ac)], -1), (1, rep))
    sm_a = jnp.tile(jnp.concatenate([-jnp.sin(ar), z16, -jnp.sin(ac), z16], -1), (1, rep))
    sp_a = jnp.tile(jnp.concatenate([z16, jnp.sin(ar), z16, jnp.sin(ac)], -1), (1, rep))
    z8 = jnp.zeros_like(a1)
    rest = HEAD_DIM - ROPE_DIMS
    ones = jnp.ones((seq, rest), F32)
    zr = jnp.zeros((seq, rest), F32)
    cos_c = jnp.tile(jnp.concatenate([jnp.cos(a1), jnp.cos(a1), ones], -1), (1, rep))
    sm_c = jnp.tile(jnp.concatenate([-jnp.sin(a1), z8, zr], -1), (1, rep))
    sp_c = jnp.tile(jnp.concatenate([z8, jnp.sin(a1), zr], -1), (1, rep))
    return cos_a, sm_a, sp_a, cos_c, sm_c, sp_c


_A_HEAD_ORDER = (0, 2, 1, 3)
_A_PERM = np.concatenate([np.arange(h * HEAD_DIM, (h + 1) * HEAD_DIM) for h in _A_HEAD_ORDER])


def _gain_row(qg):
    scale = HEAD_DIM ** -0.5 * LOG2_E
    one = lambda n: jnp.ones((n,), F32)
    return jnp.concatenate([
        jnp.tile(qg[0, 0] * scale, A_Q_HEADS), jnp.tile(qg[0, 1], A_KV_HEADS), one(A_KV_W),
        jnp.tile(qg[1, 0] * scale, B_HEADS), jnp.tile(qg[1, 1], B_HEADS), one(B_W),
        jnp.tile(qg[2, 0] * scale, C_HEADS), jnp.tile(qg[2, 1], C_HEADS), one(C_W)])[None, :]


def kernel(x, p, g_mix, w_in, qk_gain, na_bias, g_out, w_out, g_ffn, w_router,
           w_gate, w_up, w_down, g_ple, w_ple_gate, w_ple_proj):
    bsz, seq, d = x.shape
    depth = w_in.shape[0]
    t_tot = bsz * seq
    cap = max(1, EC_CAPACITY * seq // N_EXPERTS)
    n_tb = seq // TOK_BLK

    tabs = _rope_tables(seq)
    lane = np.arange(LANES)
    col = np.arange(MXU_N)
    gsum = jnp.asarray((col[:, None] // HEAD_DIM) == (col[None, :] // HEAD_DIM), BF16)
    tri = jnp.asarray(lane[:, None] < lane[None, :], BF16)
    tok = np.arange(seq)[:, None]
    j = np.arange(LANES)[None, :]
    blk_start = (tok < j * TOK_BLK) & (j < n_tb)
    blk_count = (tok // TOK_BLK == j) & (j < n_tb)
    blk = jnp.asarray(np.concatenate([blk_start, blk_count], axis=1), BF16)

    bias = _na_bias_tiles(na_bias, seq // GRID_W)
    stack = lambda w: w.astype(BF16).reshape((depth * N_EXPERTS,) + w.shape[2:])
    wg_all, wu_all, wd_all = stack(w_gate), stack(w_up), stack(w_down)
    p_all = p.reshape(depth * t_tot, p.shape[-1])
    h = x.reshape(t_tot, d)
    for i in range(depth):
        w_i = jnp.concatenate([w_in[i][:, _A_PERM], w_in[i][:, A_Q_W:]], axis=1).astype(BF16)
        ab, c1, c4, c16 = _proj_call(h, g_mix[i][None], w_i, _gain_row(qk_gain[i]), gsum, tabs, seq)
        ab = ab.reshape(bsz, seq, AB_COLS)
        oa = _attn_a_call(ab).reshape(t_tot, A_Q_W)
        ob = _attn_b_call(ab, bias[i]).reshape(t_tot, B_W)
        oc = _attn_c_call(c1, c4, c16, bsz, seq).reshape(t_tot, C_W)
        go = g_out[i]
        w_o = jnp.concatenate([w_out[i][_A_PERM], w_out[i][A_Q_W:]], axis=0).astype(BF16)
        wr = jnp.pad(w_router[i], ((0, 0), (0, LANES - N_EXPERTS)))
        wr_hi = wr.astype(BF16)
        wr_lo = (wr - wr_hi.astype(F32)).astype(BF16)
        h1, hn, lt = _out_call(oa, ob, oc, h, go[:A_Q_W][_A_PERM][None], go[None, A_Q_W:A_Q_W + B_W],
                               go[None, A_Q_W + B_W:], w_o, g_ffn[i][None],
                               jnp.concatenate([wr_hi, wr_lo], axis=1))
        pos, gate, meta = _route_call(lt, tri, blk, bsz, seq, cap)
        moe = _moe_call(meta[:, :, :META_W].reshape(-1),
                        hn.reshape(bsz, seq, d),
                        pos.reshape(bsz, N_EXPERTS, n_tb, 1, TOK_BLK),
                        gate.reshape(bsz, N_EXPERTS, n_tb, 1, TOK_BLK),
                        wg_all, wu_all, wd_all, cap, i)
        h = _ple_call(h1, moe.reshape(t_tot, d), p_all, g_ple[i][None],
                      w_ple_gate[i].astype(BF16), w_ple_proj[i].astype(BF16), i)
    return h.reshape(bsz, seq, d)
```

```python
import functools

import numpy as np
import jax
import jax.numpy as jnp
from jax import lax
from jax.experimental import pallas as pl
from jax.experimental.pallas import tpu as pltpu

F32 = jnp.float32
BF16 = jnp.bfloat16
I32 = jnp.int32

D_MODEL = 1024
HEAD_DIM = 64
A_Q_HEADS = 4
A_KV_HEADS = 2
B_HEADS = 6
C_HEADS = 6
A_Q_W = A_Q_HEADS * HEAD_DIM
A_KV_W = A_KV_HEADS * HEAD_DIM
B_W = B_HEADS * HEAD_DIM
C_W = C_HEADS * HEAD_DIM
IN_COLS = A_Q_W + 2 * A_KV_W + 3 * B_W + 3 * C_W
GRID_W = 64
AXIAL_THETA = 10000.0
NA_ROWS = 8
NA_COLS = 16
C_BRANCHES = ((128, 1), (512, 4), (2048, 16))
ROPE_THETA = 500000.0
ROPE_DIMS = HEAD_DIM // 4
N_EXPERTS = 16
EC_CAPACITY = 2
EPS = 1e-6
NEG_INF = -1e30
LOG2_E = 1.4426950408889634

LANES = 128
MXU_N = 256
VMEM_LIMIT = 56 * 1024 * 1024

N_CB = IN_COLS // LANES
CB_QA, CB_KA, CB_VA = 0, 2, 3
CB_QB, CB_KB, CB_VB = 4, 7, 10
CB_QC, CB_KC, CB_VC = 13, 16, 19
CB_KIND = ("A", "A", "A", "V") + ("N",) * 6 + ("V",) * 3 + ("C",) * 6 + ("V",) * 3
AB_COLS = CB_QC * LANES
C_COLS = 3 * C_W
N_PAIRS_C = C_W // LANES
DILS = tuple(d for _, d in C_BRANCHES)

TM = 512
TM_CHAIN = 256
TQ_A = 512
TQ_A_CHAIN = 128
NA_QROWS = 4
NA_KROWS = 12
NA_UNROLL = 7
TQ_C = 128
C_UNROLL = 10
TOK_BLK = 256
MOE_UNROLL = 16
N_OUT_CHUNKS = 8
META_W = 32


def _rms(x, g):
    return x * lax.rsqrt(jnp.mean(x * x, axis=-1, keepdims=True) + EPS) * g


def _ple_rows(h2, p, g_ref, wg_ref, wp_ref):
    a = _rms(h2, g_ref[...]).astype(BF16)
    gate = jax.nn.sigmoid(jnp.dot(a, wg_ref[...], preferred_element_type=F32))
    proj = jnp.dot(p.astype(BF16), wp_ref[...], preferred_element_type=F32)
    return h2 + gate * proj


def _proj_kernel(*refs, with_ple):
    refs = list(refs)
    h_ref = refs.pop(0)
    if with_ple:
        p_ref, gp_ref, wpg_ref, wpp_ref = (refs.pop(0) for _ in range(4))
    g_ref, w_ref, gain_ref, gsum_ref, ca_ref, sma_ref, spa_ref, cc_ref, smc_ref, spc_ref = refs[:10]
    refs = refs[10:]
    if with_ple:
        hout_ref = refs.pop(0)
    ab_ref, c1_ref, c4_ref, c16_ref, cs_ref = refs
    gsum = gsum_ref[...]
    per = MXU_N // LANES
    n_c = IN_COLS // MXU_N
    normed = [c for c in range(n_c) if any(CB_KIND[cb] != "V" for cb in range(c * per, (c + 1) * per))]
    for r0 in range(0, TM, TM_CHAIN):
        rows = slice(r0, r0 + TM_CHAIN)
        x = h_ref[rows, :]
        if with_ple:
            x = _ple_rows(x, p_ref[rows, :], gp_ref, wpg_ref, wpp_ref)
            hout_ref[rows, :] = x
        a = _rms(x, g_ref[...]).astype(BF16)
        accs = [jnp.dot(a, w_ref[:, c * MXU_N:(c + 1) * MXU_N], preferred_element_type=F32)
                for c in range(n_c)]
        sq = jnp.concatenate([accs[c] * accs[c] for c in normed], axis=0)
        hi = sq.astype(BF16)
        lo = (sq - hi.astype(F32)).astype(BF16)
        ss = jnp.dot(hi, gsum, preferred_element_type=F32) + jnp.dot(lo, gsum, preferred_element_type=F32)
        inv_all = lax.rsqrt(ss * (1.0 / HEAD_DIM) + EPS)
        for c in range(n_c):
            acc = accs[c]
            if c in normed:
                inv = inv_all[normed.index(c) * TM_CHAIN:(normed.index(c) + 1) * TM_CHAIN]
            for hf, cb in enumerate(range(c * per, (c + 1) * per)):
                t = acc[:, hf * LANES:(hf + 1) * LANES]
                kind = CB_KIND[cb]
                if kind != "V":
                    t = t * inv[:, hf * LANES:(hf + 1) * LANES] * gain_ref[:, cb * LANES:(cb + 1) * LANES]
                    if kind == "A":
                        sh = HEAD_DIM // 4
                        t = (t * ca_ref[rows, :] + pltpu.roll(t, LANES - sh, 1) * sma_ref[rows, :]
                             + pltpu.roll(t, sh, 1) * spa_ref[rows, :])
                    elif kind == "C":
                        sh = ROPE_DIMS // 2
                        t = (t * cc_ref[rows, :] + pltpu.roll(t, LANES - sh, 1) * smc_ref[rows, :]
                             + pltpu.roll(t, sh, 1) * spc_ref[rows, :])
                if cb < CB_QC:
                    ab_ref[rows, cb * LANES:(cb + 1) * LANES] = t.astype(BF16)
                else:
                    cs_ref[cb - CB_QC, rows, :] = t
        for p in range(N_PAIRS_C):
            for which in range(3):
                src = which * N_PAIRS_C + p
                for dil, ref in zip(DILS, (c1_ref, c4_ref, c16_ref)):
                    n = TM_CHAIN // dil
                    for r in range(dil):
                        dst = ((p * dil + r) * 3 + which) * LANES
                        ref[r0 // dil:r0 // dil + n, dst:dst + LANES] = (
                            cs_ref[src, pl.ds(r0 + r, n, stride=dil), :].astype(BF16))


def _proj_call(h2d, g, w, gain, gsum, tabs, seq, ple=None):
    t_tot = h2d.shape[0]
    n_t = t_tot // TM
    nseq = seq // TM
    row = lambda i: (i, 0)
    fixed = lambda i: (0, 0)
    tab = lambda i: (i % nseq, 0)
    ple_args, ple_specs, ple_out_specs, ple_out_shapes = [], [], [], []
    if ple is not None:
        p2d, p_layer, gp, wpg, wpp = ple
        ple_args = [p2d, gp, wpg, wpp]
        ple_specs = [pl.BlockSpec((TM, p2d.shape[1]), lambda i: (p_layer * n_t + i, 0)),
                     pl.BlockSpec((1, D_MODEL), fixed), pl.BlockSpec(wpg.shape, fixed),
                     pl.BlockSpec(wpp.shape, fixed)]
        ple_out_specs = [pl.BlockSpec((TM, D_MODEL), row)]
        ple_out_shapes = [jax.ShapeDtypeStruct((t_tot, D_MODEL), F32)]
    return pl.pallas_call(
        functools.partial(_proj_kernel, with_ple=ple is not None),
        grid=(n_t,),
        in_specs=[pl.BlockSpec((TM, D_MODEL), row)] + ple_specs
                 + [pl.BlockSpec((1, D_MODEL), fixed),
                    pl.BlockSpec((D_MODEL, IN_COLS), fixed),
                    pl.BlockSpec((1, IN_COLS), fixed),
                    pl.BlockSpec((MXU_N, MXU_N), fixed)]
                 + [pl.BlockSpec((TM, LANES), tab)] * 6,
        out_specs=ple_out_specs + [pl.BlockSpec((TM, AB_COLS), row)]
                  + [pl.BlockSpec((TM // dil, dil * C_COLS), row) for dil in DILS],
        out_shape=ple_out_shapes + [jax.ShapeDtypeStruct((t_tot, AB_COLS), BF16)]
                  + [jax.ShapeDtypeStruct((t_tot // dil, dil * C_COLS), BF16) for dil in DILS],
        scratch_shapes=[pltpu.VMEM((C_COLS // LANES, TM, LANES), F32)],
        compiler_params=pltpu.CompilerParams(dimension_semantics=("parallel",),
                                             vmem_limit_bytes=VMEM_LIMIT),
    )(h2d, *ple_args, g, w, gain, gsum, *tabs)


def _stack_heads(q):
    qf = q.astype(F32)
    lo = lax.broadcasted_iota(I32, qf.shape, 1) < HEAD_DIM
    return jnp.concatenate([jnp.where(lo, qf, 0.0), jnp.where(lo, 0.0, qf)], axis=0).astype(BF16)


def _merge_heads(o):
    n = o.shape[0] // 2
    lo = lax.broadcasted_iota(I32, (n, LANES), 1) < HEAD_DIM
    return jnp.where(lo, o[:n], o[n:])


def _scores(qs, k):
    return lax.dot_general(qs, k, (((1,), (1,)), ((), ())), preferred_element_type=F32)


def _attn_a_kernel(q_ref, k_ref, v_ref, o_ref):
    k = k_ref[0]
    v = v_ref[0]
    for rc in range(TQ_A // TQ_A_CHAIN):
        rows = slice(rc * TQ_A_CHAIN, (rc + 1) * TQ_A_CHAIN)
        for blk in range(A_Q_W // LANES):
            qs = _stack_heads(q_ref[0, rows, blk * LANES:(blk + 1) * LANES])
            s = _scores(qs, k)
            m = jnp.max(s, axis=-1, keepdims=True)
            p = jnp.exp2(s - m)
            l = jnp.sum(p, axis=-1, keepdims=True)
            o = jnp.dot(p.astype(BF16), v, preferred_element_type=F32) / l
            o_ref[0, rows, blk * LANES:(blk + 1) * LANES] = _merge_heads(o)


def _attn_a_call(qkv):
    bsz, seq, _ = qkv.shape
    return pl.pallas_call(
        _attn_a_kernel,
        grid=(bsz, seq // TQ_A),
        in_specs=[pl.BlockSpec((1, TQ_A, A_Q_W), lambda b, i: (b, i, 0)),
                  pl.BlockSpec((1, seq, LANES), lambda b, i: (b, 0, CB_KA)),
                  pl.BlockSpec((1, seq, LANES), lambda b, i: (b, 0, CB_VA))],
        out_specs=pl.BlockSpec((1, TQ_A, A_Q_W), lambda b, i: (b, i, 0)),
        out_shape=jax.ShapeDtypeStruct((bsz, seq, A_Q_W), F32),
        compiler_params=pltpu.CompilerParams(dimension_semantics=("parallel", "parallel"),
                                             vmem_limit_bytes=VMEM_LIMIT),
    )(qkv, qkv, qkv)


def _attn_b_kernel(q_ref, k_ref, v_ref, bias_ref, o_ref, *, n_rows):
    tq = NA_QROWS * GRID_W
    n_g = n_rows // NA_QROWS

    def group(g, cfg):
        q0 = pl.multiple_of(g * tq, tq)
        krow0 = jnp.clip(g * NA_QROWS - NA_ROWS // 2, 0, n_rows - NA_KROWS)
        t0 = pl.multiple_of(krow0 * GRID_W, GRID_W)
        kw = k_ref[0, pl.ds(t0, NA_KROWS * GRID_W), :]
        vw = v_ref[0, pl.ds(t0, NA_KROWS * GRID_W), :]
        s = _scores(_stack_heads(q_ref[0, pl.ds(q0, tq), :]), kw) + bias_ref[0, cfg]
        m = jnp.max(s, axis=-1, keepdims=True)
        p = jnp.exp2(s - m)
        l = jnp.sum(p, axis=-1, keepdims=True)
        o = jnp.dot(p.astype(BF16), vw, preferred_element_type=F32) / l
        o_ref[0, pl.ds(q0, tq), :] = _merge_heads(o)

    def interior(g, carry):
        group(g, 1)
        return carry

    group(0, 0)
    lax.fori_loop(1, n_g - 1, interior, 0, unroll=NA_UNROLL)
    group(n_g - 1, 2)


def _attn_b_call(qkv, bias):
    bsz, seq, _ = qkv.shape
    spec = lambda cb: pl.BlockSpec((1, seq, LANES), lambda p, b: (b, 0, cb + p))
    return pl.pallas_call(
        functools.partial(_attn_b_kernel, n_rows=seq // GRID_W),
        grid=(B_W // LANES, bsz),
        in_specs=[spec(CB_QB), spec(CB_KB), spec(CB_VB),
                  pl.BlockSpec((1,) + bias.shape[1:], lambda p, b: (p, 0, 0, 0))],
        out_specs=pl.BlockSpec((1, seq, LANES), lambda p, b: (b, 0, p)),
        out_shape=jax.ShapeDtypeStruct((bsz, seq, B_W), F32),
        compiler_params=pltpu.CompilerParams(dimension_semantics=("parallel",) * 2,
                                             vmem_limit_bytes=VMEM_LIMIT),
    )(qkv, qkv, qkv, bias)


def _na_bias_tiles(rpb, n_rows):
    wr = NA_ROWS
    n_ro, n_co = 2 * NA_ROWS - 1, 2 * NA_COLS - 1
    row_sel = np.zeros((3, NA_QROWS, NA_KROWS, n_ro), np.float32)
    for z, r0 in enumerate((0, NA_QROWS, n_rows - NA_QROWS)):
        start = int(np.clip(r0 - wr // 2, 0, n_rows - NA_KROWS))
        for qi in range(NA_QROWS):
            r = r0 + qi
            rs = int(np.clip(r - wr // 2, 0, n_rows - wr))
            for kj in range(NA_KROWS):
                krow = start + kj
                if rs <= krow < rs + wr:
                    row_sel[z, qi, kj, krow - r + NA_ROWS - 1] = 1.0
    col_sel = np.zeros((GRID_W, GRID_W, n_co), np.float32)
    for c in range(GRID_W):
        cstart = int(np.clip(c - NA_COLS // 2, 0, GRID_W - NA_COLS))
        for kc in range(cstart, cstart + NA_COLS):
            col_sel[c, kc, kc - c + NA_COLS - 1] = 1.0
    ok = (row_sel.sum(-1)[:, :, None, :, None] * col_sel.sum(-1)[None, None, :, None, :]) > 0
    outside = np.where(ok, 0.0, NEG_INF).astype(np.float32)[None, None, :, None]
    n_l = rpb.shape[0]
    pairs = (rpb.astype(F32) * LOG2_E).reshape(n_l, B_HEADS // 2, 2, n_ro, n_co)
    vals = jnp.einsum("zqka,lptab,cjb->lpztqckj", row_sel, pairs, col_sel,
                      precision=lax.Precision.HIGHEST) + outside
    return vals.reshape(n_l, B_HEADS // 2, 3, 2 * NA_QROWS * GRID_W, NA_KROWS * GRID_W)


def _band_tile(q, kw, vw, band):
    s = _scores(_stack_heads(q), kw) + band
    m = jnp.max(s, axis=-1, keepdims=True)
    p = jnp.exp2(s - m)
    l = jnp.sum(p, axis=-1, keepdims=True)
    o = jnp.dot(p.astype(BF16), vw, preferred_element_type=F32) / l
    lse = jnp.broadcast_to(m + jnp.log2(l), (2 * TQ_C, LANES))
    return _merge_heads(o), _merge_heads(lse)


def _attn_c_kernel(c1_ref, c4_ref, c16_ref, band_ref, o_ref, o2_ref, l2_ref, o3_ref, l3_ref, *, seq):
    def branch(ref, window, dil, r, emit):
        length = seq // dil
        half = window // 2 // dil
        n_q = length // TQ_C
        win = min(TQ_C + 2 * half, length)
        base = r * 3 * LANES

        def tile(qi, band):
            if isinstance(qi, int):
                q0 = qi * TQ_C
                ks = min(max(q0 - half, 0), length - win)
            else:
                q0 = pl.multiple_of(qi * TQ_C, TQ_C)
                ks = pl.multiple_of(jnp.clip(q0 - half, 0, length - win), half)
            o, lse = _band_tile(ref[0, pl.ds(q0, TQ_C), base:base + LANES],
                                ref[0, pl.ds(ks, win), base + LANES:base + 2 * LANES],
                                ref[0, pl.ds(ks, win), base + 2 * LANES:base + 3 * LANES], band)
            emit(q0, o, lse)

        def interior(qi, carry):
            tile(qi, band_ref[1])
            return carry

        tile(0, band_ref[0])
        if n_q > 2:
            lax.fori_loop(1, n_q - 1, interior, 0, unroll=min(C_UNROLL, n_q - 2))
        tile(n_q - 1, band_ref[2])

    for ref, (window, dil), o_scr, l_scr in ((c4_ref, C_BRANCHES[1], o2_ref, l2_ref),
                                            (c16_ref, C_BRANCHES[2], o3_ref, l3_ref)):
        for r in range(dil):
            def emit(q0, o, lse, r=r, dil=dil, o_scr=o_scr, l_scr=l_scr):
                rows = pl.ds(r + dil * q0, TQ_C, stride=dil)
                o_scr[rows, :] = o
                l_scr[rows, :] = lse
            branch(ref, window, dil, r, emit)

    def emit1(q0, o1, l1):
        rows = pl.ds(q0, TQ_C)
        l2, l3 = l2_ref[rows, :], l3_ref[rows, :]
        lm = jnp.maximum(jnp.maximum(l1, l2), l3)
        e1, e2, e3 = jnp.exp2(l1 - lm), jnp.exp2(l2 - lm), jnp.exp2(l3 - lm)
        o_ref[0, rows, :] = (e1 * o1 + e2 * o2_ref[rows, :] + e3 * o3_ref[rows, :]) / (e1 + e2 + e3)

    branch(c1_ref, C_BRANCHES[0][0], 1, 0, emit1)


def _band_masks():
    half = C_BRANCHES[0][0] // 2
    assert all(w // 2 // d == half for w, d in C_BRANCHES)
    i = np.arange(2 * TQ_C)[:, None] % TQ_C
    j = np.arange(TQ_C + 2 * half)[None, :]
    return jnp.asarray(np.stack([np.where(np.abs(off + i - j) <= half, 0.0, NEG_INF)
                                 for off in (0, half, 2 * half)]).astype(np.float32))


def _attn_c_call(c1, c4, c16, bsz, seq):
    views = [c.reshape(bsz, seq // dil, dil * C_COLS) for c, dil in zip((c1, c4, c16), DILS)]
    spec = lambda dil: pl.BlockSpec((1, seq // dil, dil * 3 * LANES), lambda b, p: (b, 0, p))
    band = _band_masks()
    assert seq // DILS[-1] >= band.shape[2]
    return pl.pallas_call(
        functools.partial(_attn_c_kernel, seq=seq),
        grid=(bsz, N_PAIRS_C),
        in_specs=[spec(dil) for dil in DILS] + [pl.BlockSpec(band.shape, lambda b, p: (0, 0, 0))],
        out_specs=pl.BlockSpec((1, seq, LANES), lambda b, p: (b, 0, p)),
        out_shape=jax.ShapeDtypeStruct((bsz, seq, C_W), F32),
        scratch_shapes=[pltpu.VMEM((seq, LANES), F32)] * 4,
        compiler_params=pltpu.CompilerParams(dimension_semantics=("parallel",) * 2,
                                             vmem_limit_bytes=VMEM_LIMIT),
    )(*views, band)


def _out_kernel(oa_ref, ob_ref, oc_ref, h_ref, ga_ref, gb_ref, gc_ref, w_ref, gf_ref, wr_ref,
                h1_ref, hn_ref, lt_ref):
    mixed = jnp.concatenate([_rms(oa_ref[...], ga_ref[...]),
                             _rms(ob_ref[...], gb_ref[...]),
                             _rms(oc_ref[...], gc_ref[...])], axis=-1).astype(BF16)
    h1 = h_ref[...] + jnp.dot(mixed, w_ref[...], preferred_element_type=F32)
    h1_ref[...] = h1
    hn = _rms(h1, gf_ref[...])
    hi = hn.astype(BF16)
    hn_ref[...] = hi
    lo = (hn - hi.astype(F32)).astype(BF16)
    both = jnp.dot(hi, wr_ref[...], preferred_element_type=F32)
    logits = (both[:, :LANES] + both[:, LANES:]
              + jnp.dot(lo, wr_ref[:, :LANES], preferred_element_type=F32))
    lt_ref[...] = logits.T[:N_EXPERTS, :]


def _out_call(oa, ob, oc, h2d, ga, gb, gc, w, gf, wr):
    t_tot = h2d.shape[0]
    row = lambda i: (i, 0)
    fixed = lambda i: (0, 0)
    rs = lambda w_: pl.BlockSpec((TM, w_), row)
    fs = lambda a: pl.BlockSpec(a.shape, fixed)
    return pl.pallas_call(
        _out_kernel,
        grid=(t_tot // TM,),
        in_specs=[rs(A_Q_W), rs(B_W), rs(C_W), rs(D_MODEL)]
                 + [fs(ga), fs(gb), fs(gc), fs(w), fs(gf), fs(wr)],
        out_specs=[rs(D_MODEL), rs(D_MODEL), pl.BlockSpec((N_EXPERTS, TM), lambda i: (0, i))],
        out_shape=[jax.ShapeDtypeStruct((t_tot, D_MODEL), F32),
                   jax.ShapeDtypeStruct((t_tot, D_MODEL), BF16),
                   jax.ShapeDtypeStruct((N_EXPERTS, t_tot), F32)],
        compiler_params=pltpu.CompilerParams(dimension_semantics=("parallel",),
                                             vmem_limit_bytes=VMEM_LIMIT),
    )(oa, ob, oc, h2d, ga, gb, gc, w, gf, wr)


def _route_kernel(lt_ref, tri_ref, blk_ref, pos_ref, gate_ref, meta_ref, *, cap):
    l = lt_ref[...]
    seq = l.shape[1]
    m = jnp.max(l, axis=0, keepdims=True)
    ex = jnp.exp(l - m)
    aff = ex / jnp.sum(ex, axis=0, keepdims=True)
    gate_ref[0] = aff
    bits = lax.bitcast_convert_type(aff, I32)

    def count(mask):
        return jnp.sum(mask.astype(F32), axis=1, keepdims=True)

    def search(i, t):
        cand = t | (1 << (30 - i))
        return jnp.where(count(bits >= cand) >= cap, cand, t)

    thr = lax.fori_loop(0, 31, search, jnp.zeros((N_EXPERTS, 1), I32))
    gt = bits > thr
    eq = bits == thr
    need = cap - count(gt)

    tri = tri_ref[...]

    def excl_prefix(mask):
        mb = jnp.where(mask, 1.0, 0.0).astype(BF16)
        parts = []
        run = jnp.zeros((N_EXPERTS, 1), F32)
        for c in range(seq // LANES):
            ch = mb[:, c * LANES:(c + 1) * LANES]
            parts.append(jnp.dot(ch, tri, preferred_element_type=F32) + run)
            run = run + jnp.sum(ch.astype(F32), axis=1, keepdims=True)
        return jnp.concatenate(parts, axis=1)

    sel = gt | (eq & (excl_prefix(eq) < need))
    pos = excl_prefix(sel)
    pos_ref[0] = jnp.where(sel, pos, -1.0).astype(I32)
    sb = jnp.where(sel, 1.0, 0.0).astype(BF16)
    sc = jnp.dot(sb, blk_ref[...], preferred_element_type=F32)
    start = sc[:, :LANES]
    cnt = sc[:, LANES:]
    ilo = jnp.minimum(jnp.floor(start * (1.0 / TOK_BLK)), cap // TOK_BLK - 1.0)
    cross = jnp.where(start + cnt > (ilo + 1.0) * TOK_BLK, 1.0, 0.0)
    lane = lax.broadcasted_iota(I32, cross.shape, 1)
    has = jnp.sum(cross, axis=1, keepdims=True)
    jstar = jnp.sum(cross * lane.astype(F32), axis=1, keepdims=True)
    ifix = cap // TOK_BLK - has
    n_tb = seq // TOK_BLK
    meta = jnp.where(lane < n_tb, ilo, jnp.where(lane == n_tb, jstar, jnp.where(lane == n_tb + 1, ifix, 0.0)))
    meta_ref[0] = meta.astype(I32)


def _route_call(lt, tri, blk, bsz, seq, cap):
    e = N_EXPERTS
    big = lambda dt: jax.ShapeDtypeStruct((bsz, e, seq), dt)
    small = jax.ShapeDtypeStruct((bsz, e, LANES), I32)
    bspec = pl.BlockSpec((1, e, seq), lambda b: (b, 0, 0))
    sspec = pl.BlockSpec((1, e, LANES), lambda b: (b, 0, 0))
    return pl.pallas_call(
        functools.partial(_route_kernel, cap=cap),
        grid=(bsz,),
        in_specs=[pl.BlockSpec((e, seq), lambda b: (0, b)),
                  pl.BlockSpec(tri.shape, lambda b: (0, 0)),
                  pl.BlockSpec(blk.shape, lambda b: (0, 0))],
        out_specs=[bspec, bspec, sspec],
        out_shape=[big(I32), big(F32), small],
        compiler_params=pltpu.CompilerParams(dimension_semantics=("parallel",),
                                             vmem_limit_bytes=VMEM_LIMIT),
    )(lt, tri, blk)


def _moe_kernel(meta_ref, hn_ref, pos_ref, gate_ref, wg_ref, wu_ref, wd_ref, res_ref, o_ref,
                acc_ref, xg_ref, y_ref, *, cap, n_tb):
    b = pl.program_id(0)
    k = pl.program_id(1)
    out_rows = o_ref.shape[1]

    @pl.when(k == 0)
    def _():
        acc_ref[...] = jnp.zeros_like(acc_ref)
        y_ref[cap:, :] = jnp.zeros((TOK_BLK, y_ref.shape[1]), BF16)

    @pl.when(k < N_EXPERTS)
    def _():
        xg_ref[...] = jnp.zeros_like(xg_ref)
        slot = lax.broadcasted_iota(I32, (TOK_BLK, TOK_BLK), 0)
        mbase = (b * N_EXPERTS + k) * META_W
        jstar = meta_ref[mbase + n_tb]
        ifix = meta_ref[mbase + n_tb + 1]

        def gather(j, i):
            lp = pos_ref[0, 0, j]
            hb = hn_ref[0, pl.ds(pl.multiple_of(j * TOK_BLK, TOK_BLK), TOK_BLK), :]
            oh = jnp.where(slot + i * TOK_BLK == lp, 1.0, 0.0).astype(BF16)
            r0 = pl.multiple_of(i * TOK_BLK, TOK_BLK)
            xg_ref[pl.ds(r0, TOK_BLK), :] += jnp.dot(oh, hb, preferred_element_type=F32)

        def gather_body(j, carry):
            gather(j, meta_ref[mbase + j])
            return carry

        lax.fori_loop(0, n_tb, gather_body, 0, unroll=MOE_UNROLL)
        gather(jstar, ifix)

        x = xg_ref[:cap, :].astype(BF16)
        y = jnp.zeros((cap, D_MODEL), F32)
        d_ff = wg_ref.shape[2]
        for fc in range(d_ff // MXU_N):
            cs = slice(fc * MXU_N, (fc + 1) * MXU_N)
            g = jnp.dot(x, wg_ref[0, :, cs], preferred_element_type=F32)
            u = jnp.dot(x, wu_ref[0, :, cs], preferred_element_type=F32)
            hid = (g * jax.nn.sigmoid(g) * u).astype(BF16)
            y = y + jnp.dot(hid, wd_ref[0, cs, :], preferred_element_type=F32)
        y_ref[:cap, :] = y.astype(BF16)

        def scatter(j, i):
            lp = pos_ref[0, 0, j]
            gr = gate_ref[0, 0, j]
            t0 = pl.multiple_of(j * TOK_BLK, TOK_BLK)
            oh = jnp.where(slot + i * TOK_BLK == lp, gr, 0.0).astype(BF16)
            r0 = pl.multiple_of(i * TOK_BLK, TOK_BLK)
            acc_ref[pl.ds(t0, TOK_BLK), :] += lax.dot_general(
                oh, y_ref[pl.ds(r0, TOK_BLK), :], (((0,), (0,)), ((), ())),
                preferred_element_type=F32)

        def scatter_body(j, carry):
            scatter(j, meta_ref[mbase + j])
            return carry

        lax.fori_loop(0, n_tb, scatter_body, 0, unroll=MOE_UNROLL)
        scatter(jstar, ifix)

    @pl.when(k >= N_EXPERTS)
    def _():
        r0 = pl.multiple_of((k - N_EXPERTS) * out_rows, out_rows)
        o_ref[0] = res_ref[0] + acc_ref[pl.ds(r0, out_rows), :]


def _moe_call(meta, hn, pos, gate, wg, wu, wd, res, cap, layer):
    bsz, seq, d = hn.shape
    n_tb = seq // TOK_BLK
    e_idx = lambda k: jnp.minimum(k, N_EXPERTS - 1)
    tok_spec = pl.BlockSpec((1, 1, n_tb, 1, TOK_BLK), lambda b, k, *_: (b, e_idx(k), 0, 0, 0))
    w_spec = lambda w: pl.BlockSpec((1,) + w.shape[1:],
                                    lambda b, k, *_: (layer * N_EXPERTS + e_idx(k), 0, 0))
    out_rows = seq // N_OUT_CHUNKS
    out_spec = pl.BlockSpec((1, out_rows, d), lambda b, k, *_: (b, jnp.maximum(k - N_EXPERTS, 0), 0))
    grid_spec = pltpu.PrefetchScalarGridSpec(
        num_scalar_prefetch=1,
        grid=(bsz, N_EXPERTS + N_OUT_CHUNKS),
        in_specs=[pl.BlockSpec((1, seq, d), lambda b, k, *_: (b, 0, 0), pipeline_mode=pl.Buffered(1)),
                  tok_spec, tok_spec, w_spec(wg), w_spec(wu), w_spec(wd), out_spec],
        out_specs=out_spec,
        scratch_shapes=[pltpu.VMEM((seq, d), F32), pltpu.VMEM((cap + TOK_BLK, d), F32),
                        pltpu.VMEM((cap + TOK_BLK, d), BF16)],
    )
    return pl.pallas_call(
        functools.partial(_moe_kernel, cap=cap, n_tb=n_tb),
        grid_spec=grid_spec,
        out_shape=jax.ShapeDtypeStruct((bsz, seq, d), F32),
        compiler_params=pltpu.CompilerParams(dimension_semantics=("parallel", "arbitrary"),
                                             vmem_limit_bytes=VMEM_LIMIT),
    )(meta, hn, pos, gate, wg, wu, wd, res)


def _ple_kernel(h_ref, p_ref, g_ref, wg_ref, wp_ref, o_ref):
    o_ref[...] = _ple_rows(h_ref[...], p_ref[...], g_ref, wg_ref, wp_ref)


def _ple_call(h2, p2d, g, wg, wp, layer):
    t_tot = h2.shape[0]
    row = lambda i: (i, 0)
    p_row = lambda i: (layer * (t_tot // TM) + i, 0)
    fixed = lambda i: (0, 0)
    return pl.pallas_call(
        _ple_kernel,
        grid=(t_tot // TM,),
        in_specs=[pl.BlockSpec((TM, D_MODEL), row),
                  pl.BlockSpec((TM, p2d.shape[1]), p_row), pl.BlockSpec((1, D_MODEL), fixed),
                  pl.BlockSpec(wg.shape, fixed), pl.BlockSpec(wp.shape, fixed)],
        out_specs=pl.BlockSpec((TM, D_MODEL), row),
        out_shape=jax.ShapeDtypeStruct((t_tot, D_MODEL), F32),
        compiler_params=pltpu.CompilerParams(dimension_semantics=("parallel",),
                                             vmem_limit_bytes=VMEM_LIMIT),
    )(h2, p2d, g, wg, wp)


def _rope_tables(seq):
    pos = jnp.arange(seq)

    def angles(pos_f, n, theta):
        inv = theta ** (-jnp.arange(0, n, 2, dtype=F32) / n)
        return pos_f[:, None] * inv[None, :]

    ar = angles((pos // GRID_W).astype(F32), HEAD_DIM // 2, AXIAL_THETA)
    ac = angles((pos % GRID_W).astype(F32), HEAD_DIM // 2, AXIAL_THETA)
    a1 = angles(pos.astype(F32), ROPE_DIMS, ROPE_THETA)
    z16 = jnp.zeros_like(ar)
    rep = LANES // HEAD_DIM
    cos_a = jnp.tile(jnp.concatenate([jnp.cos(ar), jnp.cos(ar), jnp.cos(ac), jnp.cos(ac)], -1), (1, rep))
    sm_a = jnp.tile(jnp.concatenate([-jnp.sin(ar), z16, -jnp.sin(ac), z16], -1), (1, rep))
    sp_a = jnp.tile(jnp.concatenate([z16, jnp.sin(ar), z16, jnp.sin(ac)], -1), (1, rep))
    z8 = jnp.zeros_like(a1)
    rest = HEAD_DIM - ROPE_DIMS
    ones = jnp.ones((seq, rest), F32)
    zr = jnp.zeros((seq, rest), F32)
    cos_c = jnp.tile(jnp.concatenate([jnp.cos(a1), jnp.cos(a1), ones], -1), (1, rep))
    sm_c = jnp.tile(jnp.concatenate([-jnp.sin(a1), z8, zr], -1), (1, rep))
    sp_c = jnp.tile(jnp.concatenate([z8, jnp.sin(a1), zr], -1), (1, rep))
    return cos_a, sm_a, sp_a, cos_c, sm_c, sp_c


_A_HEAD_ORDER = (0, 2, 1, 3)
_A_PERM = np.concatenate([np.arange(h * HEAD_DIM, (h + 1) * HEAD_DIM) for h in _A_HEAD_ORDER])


def _gain_row(qg):
    scale = HEAD_DIM ** -0.5 * LOG2_E
    one = lambda n: jnp.ones((n,), F32)
    return jnp.concatenate([
        jnp.tile(qg[0, 0] * scale, A_Q_HEADS), jnp.tile(qg[0, 1], A_KV_HEADS), one(A_KV_W),
        jnp.tile(qg[1, 0] * scale, B_HEADS), jnp.tile(qg[1, 1], B_HEADS), one(B_W),
        jnp.tile(qg[2, 0] * scale, C_HEADS), jnp.tile(qg[2, 1], C_HEADS), one(C_W)])[None, :]


def kernel(x, p, g_mix, w_in, qk_gain, na_bias, g_out, w_out, g_ffn, w_router,
           w_gate, w_up, w_down, g_ple, w_ple_gate, w_ple_proj):
    bsz, seq, d = x.shape
    depth = w_in.shape[0]
    t_tot = bsz * seq
    cap = max(1, EC_CAPACITY * seq // N_EXPERTS)
    n_tb = seq // TOK_BLK

    tabs = _rope_tables(seq)
    lane = np.arange(LANES)
    col = np.arange(MXU_N)
    gsum = jnp.asarray((col[:, None] // HEAD_DIM) == (col[None, :] // HEAD_DIM), BF16)
    tri = jnp.asarray(lane[:, None] < lane[None, :], BF16)
    tok = np.arange(seq)[:, None]
    j = np.arange(LANES)[None, :]
    blk_start = (tok < j * TOK_BLK) & (j < n_tb)
    blk_count = (tok // TOK_BLK == j) & (j < n_tb)
    blk = jnp.asarray(np.concatenate([blk_start, blk_count], axis=1), BF16)

    bias = _na_bias_tiles(na_bias, seq // GRID_W)
    stack = lambda w: w.astype(BF16).reshape((depth * N_EXPERTS,) + w.shape[2:])
    wg_all, wu_all, wd_all = stack(w_gate), stack(w_up), stack(w_down)
    p_all = p.reshape(depth * t_tot, p.shape[-1])
    ple_of = lambda i: (p_all, i, g_ple[i][None], w_ple_gate[i].astype(BF16), w_ple_proj[i].astype(BF16))
    h = x.reshape(t_tot, d)
    h2 = None
    for i in range(depth):
        w_i = jnp.concatenate([w_in[i][:, _A_PERM], w_in[i][:, A_Q_W:]], axis=1).astype(BF16)
        proj_args = (g_mix[i][None], w_i, _gain_row(qk_gain[i]), gsum, tabs, seq)
        if i == 0:
            ab, c1, c4, c16 = _proj_call(h, *proj_args)
        else:
            h, ab, c1, c4, c16 = _proj_call(h2, *proj_args, ple=ple_of(i - 1))
        ab = ab.reshape(bsz, seq, AB_COLS)
        oa = _attn_a_call(ab).reshape(t_tot, A_Q_W)
        ob = _attn_b_call(ab, bias[i]).reshape(t_tot, B_W)
        oc = _attn_c_call(c1, c4, c16, bsz, seq).reshape(t_tot, C_W)
        go = g_out[i]
        w_o = jnp.concatenate([w_out[i][_A_PERM], w_out[i][A_Q_W:]], axis=0).astype(BF16)
        wr = jnp.pad(w_router[i], ((0, 0), (0, LANES - N_EXPERTS)))
        wr_hi = wr.astype(BF16)
        wr_lo = (wr - wr_hi.astype(F32)).astype(BF16)
        h1, hn, lt = _out_call(oa, ob, oc, h, go[:A_Q_W][_A_PERM][None], go[None, A_Q_W:A_Q_W + B_W],
                               go[None, A_Q_W + B_W:], w_o, g_ffn[i][None],
                               jnp.concatenate([wr_hi, wr_lo], axis=1))
        pos, gate, meta = _route_call(lt, tri, blk, bsz, seq, cap)
        h2 = _moe_call(meta[:, :, :META_W].reshape(-1),
                       hn.reshape(bsz, seq, d),
                       pos.reshape(bsz, N_EXPERTS, n_tb, 1, TOK_BLK),
                       gate.reshape(bsz, N_EXPERTS, n_tb, 1, TOK_BLK),
                       wg_all, wu_all, wd_all, h1.reshape(bsz, seq, d), cap, i).reshape(t_tot, d)
    p2d, p_layer, gp, wpg, wpp = ple_of(depth - 1)
    return _ple_call(h2, p2d, gp, wpg, wpp, p_layer).reshape(bsz, seq, d)
```

```python
import functools

import numpy as np
import jax
import jax.numpy as jnp
from jax import lax
from jax.experimental import pallas as pl
from jax.experimental.pallas import tpu as pltpu

F32 = jnp.float32
BF16 = jnp.bfloat16
I32 = jnp.int32

D_MODEL = 1024
HEAD_DIM = 64
A_Q_HEADS = 4
A_KV_HEADS = 2
B_HEADS = 6
C_HEADS = 6
A_Q_W = A_Q_HEADS * HEAD_DIM
A_KV_W = A_KV_HEADS * HEAD_DIM
B_W = B_HEADS * HEAD_DIM
C_W = C_HEADS * HEAD_DIM
IN_COLS = A_Q_W + 2 * A_KV_W + 3 * B_W + 3 * C_W
GRID_W = 64
AXIAL_THETA = 10000.0
NA_ROWS = 8
NA_COLS = 16
C_BRANCHES = ((128, 1), (512, 4), (2048, 16))
ROPE_THETA = 500000.0
ROPE_DIMS = HEAD_DIM // 4
N_EXPERTS = 16
EC_CAPACITY = 2
EPS = 1e-6
NEG_INF = -1e30
LOG2_E = 1.4426950408889634

LANES = 128
MXU_N = 256
VMEM_LIMIT = 56 * 1024 * 1024
VMEM_LIMIT_MOE = 60 * 1024 * 1024

N_CB = IN_COLS // LANES
CB_QA, CB_KA, CB_VA = 0, 2, 3
CB_QB, CB_KB, CB_VB = 4, 7, 10
CB_QC, CB_KC, CB_VC = 13, 16, 19
CB_KIND = ("A", "A", "A", "V") + ("N",) * 6 + ("V",) * 3 + ("C",) * 6 + ("V",) * 3
AB_COLS = CB_QC * LANES
C_COLS = 3 * C_W
N_PAIRS_C = C_W // LANES
DILS = tuple(d for _, d in C_BRANCHES)

TM = 512
TM_CHAIN = 256
TQ_A = 512
TQ_A_CHAIN = 128
NA_QROWS = 4
NA_KROWS = 12
NA_UNROLL = 7
TQ_C = 128
C_UNROLL = 10
TOK_BLK = 256
MOE_UNROLL = 16
N_OUT_CHUNKS = 8
META_W = 32


def _rms(x, g):
    return x * lax.rsqrt(jnp.mean(x * x, axis=-1, keepdims=True) + EPS) * g


def _proj_kernel(h_ref, g_ref, w_ref, gain_ref, gsum_ref, ca_ref, sma_ref, spa_ref,
                 cc_ref, smc_ref, spc_ref, ab_ref, c1_ref, c4_ref, c16_ref, cs_ref):
    gsum = gsum_ref[...]
    per = MXU_N // LANES
    n_c = IN_COLS // MXU_N
    normed = [c for c in range(n_c) if any(CB_KIND[cb] != "V" for cb in range(c * per, (c + 1) * per))]
    for r0 in range(0, TM, TM_CHAIN):
        rows = slice(r0, r0 + TM_CHAIN)
        a = _rms(h_ref[rows, :], g_ref[...]).astype(BF16)
        accs = [jnp.dot(a, w_ref[:, c * MXU_N:(c + 1) * MXU_N], preferred_element_type=F32)
                for c in range(n_c)]
        sq = jnp.concatenate([accs[c] * accs[c] for c in normed], axis=0)
        hi = sq.astype(BF16)
        lo = (sq - hi.astype(F32)).astype(BF16)
        ss = jnp.dot(hi, gsum, preferred_element_type=F32) + jnp.dot(lo, gsum, preferred_element_type=F32)
        inv_all = lax.rsqrt(ss * (1.0 / HEAD_DIM) + EPS)
        for c in range(n_c):
            acc = accs[c]
            if c in normed:
                inv = inv_all[normed.index(c) * TM_CHAIN:(normed.index(c) + 1) * TM_CHAIN]
            for hf, cb in enumerate(range(c * per, (c + 1) * per)):
                t = acc[:, hf * LANES:(hf + 1) * LANES]
                kind = CB_KIND[cb]
                if kind != "V":
                    t = t * inv[:, hf * LANES:(hf + 1) * LANES] * gain_ref[:, cb * LANES:(cb + 1) * LANES]
                    if kind == "A":
                        sh = HEAD_DIM // 4
                        t = (t * ca_ref[rows, :] + pltpu.roll(t, LANES - sh, 1) * sma_ref[rows, :]
                             + pltpu.roll(t, sh, 1) * spa_ref[rows, :])
                    elif kind == "C":
                        sh = ROPE_DIMS // 2
                        t = (t * cc_ref[rows, :] + pltpu.roll(t, LANES - sh, 1) * smc_ref[rows, :]
                             + pltpu.roll(t, sh, 1) * spc_ref[rows, :])
                if cb < CB_QC:
                    ab_ref[rows, cb * LANES:(cb + 1) * LANES] = t.astype(BF16)
                else:
                    cs_ref[cb - CB_QC, rows, :] = t
        for p in range(N_PAIRS_C):
            for which in range(3):
                src = which * N_PAIRS_C + p
                for dil, ref in zip(DILS, (c1_ref, c4_ref, c16_ref)):
                    n = TM_CHAIN // dil
                    for r in range(dil):
                        dst = ((p * dil + r) * 3 + which) * LANES
                        ref[r0 // dil:r0 // dil + n, dst:dst + LANES] = (
                            cs_ref[src, pl.ds(r0 + r, n, stride=dil), :].astype(BF16))


def _proj_call(h2d, g, w, gain, gsum, tabs, seq):
    t_tot = h2d.shape[0]
    nseq = seq // TM
    row = lambda i: (i, 0)
    fixed = lambda i: (0, 0)
    tab = lambda i: (i % nseq, 0)
    return pl.pallas_call(
        _proj_kernel,
        grid=(t_tot // TM,),
        in_specs=[pl.BlockSpec((TM, D_MODEL), row),
                  pl.BlockSpec((1, D_MODEL), fixed),
                  pl.BlockSpec((D_MODEL, IN_COLS), fixed),
                  pl.BlockSpec((1, IN_COLS), fixed),
                  pl.BlockSpec((MXU_N, MXU_N), fixed)]
                 + [pl.BlockSpec((TM, LANES), tab)] * 6,
        out_specs=[pl.BlockSpec((TM, AB_COLS), row)]
                  + [pl.BlockSpec((TM // dil, dil * C_COLS), row) for dil in DILS],
        out_shape=[jax.ShapeDtypeStruct((t_tot, AB_COLS), BF16)]
                  + [jax.ShapeDtypeStruct((t_tot // dil, dil * C_COLS), BF16) for dil in DILS],
        scratch_shapes=[pltpu.VMEM((C_COLS // LANES, TM, LANES), F32)],
        compiler_params=pltpu.CompilerParams(dimension_semantics=("parallel",),
                                             vmem_limit_bytes=VMEM_LIMIT),
    )(h2d, g, w, gain, gsum, *tabs)


def _stack_heads(q):
    qf = q.astype(F32)
    lo = lax.broadcasted_iota(I32, qf.shape, 1) < HEAD_DIM
    return jnp.concatenate([jnp.where(lo, qf, 0.0), jnp.where(lo, 0.0, qf)], axis=0).astype(BF16)


def _merge_heads(o):
    n = o.shape[0] // 2
    lo = lax.broadcasted_iota(I32, (n, LANES), 1) < HEAD_DIM
    return jnp.where(lo, o[:n], o[n:])


def _scores(qs, k):
    return lax.dot_general(qs, k, (((1,), (1,)), ((), ())), preferred_element_type=F32)


def _attn_a_kernel(q_ref, k_ref, v_ref, o_ref):
    k = k_ref[0]
    v = v_ref[0]
    for rc in range(TQ_A // TQ_A_CHAIN):
        rows = slice(rc * TQ_A_CHAIN, (rc + 1) * TQ_A_CHAIN)
        for blk in range(A_Q_W // LANES):
            qs = _stack_heads(q_ref[0, rows, blk * LANES:(blk + 1) * LANES])
            s = _scores(qs, k)
            m = jnp.max(s, axis=-1, keepdims=True)
            p = jnp.exp2(s - m)
            l = jnp.sum(p, axis=-1, keepdims=True)
            o = jnp.dot(p.astype(BF16), v, preferred_element_type=F32) / l
            o_ref[0, rows, blk * LANES:(blk + 1) * LANES] = _merge_heads(o)


def _attn_a_call(qkv):
    bsz, seq, _ = qkv.shape
    return pl.pallas_call(
        _attn_a_kernel,
        grid=(bsz, seq // TQ_A),
        in_specs=[pl.BlockSpec((1, TQ_A, A_Q_W), lambda b, i: (b, i, 0)),
                  pl.BlockSpec((1, seq, LANES), lambda b, i: (b, 0, CB_KA)),
                  pl.BlockSpec((1, seq, LANES), lambda b, i: (b, 0, CB_VA))],
        out_specs=pl.BlockSpec((1, TQ_A, A_Q_W), lambda b, i: (b, i, 0)),
        out_shape=jax.ShapeDtypeStruct((bsz, seq, A_Q_W), F32),
        compiler_params=pltpu.CompilerParams(dimension_semantics=("parallel", "parallel"),
                                             vmem_limit_bytes=VMEM_LIMIT),
    )(qkv, qkv, qkv)


def _attn_b_kernel(q_ref, k_ref, v_ref, bias_ref, o_ref, *, n_rows):
    tq = NA_QROWS * GRID_W
    n_g = n_rows // NA_QROWS

    def group(g, cfg):
        q0 = pl.multiple_of(g * tq, tq)
        krow0 = jnp.clip(g * NA_QROWS - NA_ROWS // 2, 0, n_rows - NA_KROWS)
        t0 = pl.multiple_of(krow0 * GRID_W, GRID_W)
        kw = k_ref[0, pl.ds(t0, NA_KROWS * GRID_W), :]
        vw = v_ref[0, pl.ds(t0, NA_KROWS * GRID_W), :]
        s = _scores(_stack_heads(q_ref[0, pl.ds(q0, tq), :]), kw) + bias_ref[0, cfg]
        m = jnp.max(s, axis=-1, keepdims=True)
        p = jnp.exp2(s - m)
        l = jnp.sum(p, axis=-1, keepdims=True)
        o = jnp.dot(p.astype(BF16), vw, preferred_element_type=F32) / l
        o_ref[0, pl.ds(q0, tq), :] = _merge_heads(o)

    def interior(g, carry):
        group(g, 1)
        return carry

    group(0, 0)
    lax.fori_loop(1, n_g - 1, interior, 0, unroll=NA_UNROLL)
    group(n_g - 1, 2)


def _attn_b_call(qkv, bias):
    bsz, seq, _ = qkv.shape
    spec = lambda cb: pl.BlockSpec((1, seq, LANES), lambda p, b: (b, 0, cb + p))
    return pl.pallas_call(
        functools.partial(_attn_b_kernel, n_rows=seq // GRID_W),
        grid=(B_W // LANES, bsz),
        in_specs=[spec(CB_QB), spec(CB_KB), spec(CB_VB),
                  pl.BlockSpec((1,) + bias.shape[1:], lambda p, b: (p, 0, 0, 0))],
        out_specs=pl.BlockSpec((1, seq, LANES), lambda p, b: (b, 0, p)),
        out_shape=jax.ShapeDtypeStruct((bsz, seq, B_W), F32),
        compiler_params=pltpu.CompilerParams(dimension_semantics=("parallel",) * 2,
                                             vmem_limit_bytes=VMEM_LIMIT),
    )(qkv, qkv, qkv, bias)


def _na_bias_tiles(rpb, n_rows):
    wr = NA_ROWS
    n_ro, n_co = 2 * NA_ROWS - 1, 2 * NA_COLS - 1
    row_sel = np.zeros((3, NA_QROWS, NA_KROWS, n_ro), np.float32)
    for z, r0 in enumerate((0, NA_QROWS, n_rows - NA_QROWS)):
        start = int(np.clip(r0 - wr // 2, 0, n_rows - NA_KROWS))
        for qi in range(NA_QROWS):
            r = r0 + qi
            rs = int(np.clip(r - wr // 2, 0, n_rows - wr))
            for kj in range(NA_KROWS):
                krow = start + kj
                if rs <= krow < rs + wr:
                    row_sel[z, qi, kj, krow - r + NA_ROWS - 1] = 1.0
    col_sel = np.zeros((GRID_W, GRID_W, n_co), np.float32)
    for c in range(GRID_W):
        cstart = int(np.clip(c - NA_COLS // 2, 0, GRID_W - NA_COLS))
        for kc in range(cstart, cstart + NA_COLS):
            col_sel[c, kc, kc - c + NA_COLS - 1] = 1.0
    ok = (row_sel.sum(-1)[:, :, None, :, None] * col_sel.sum(-1)[None, None, :, None, :]) > 0
    outside = np.where(ok, 0.0, NEG_INF).astype(np.float32)[None, None, :, None]
    n_l = rpb.shape[0]
    pairs = (rpb.astype(F32) * LOG2_E).reshape(n_l, B_HEADS // 2, 2, n_ro, n_co)
    vals = jnp.einsum("zqka,lptab,cjb->lpztqckj", row_sel, pairs, col_sel,
                      precision=lax.Precision.HIGHEST) + outside
    return vals.reshape(n_l, B_HEADS // 2, 3, 2 * NA_QROWS * GRID_W, NA_KROWS * GRID_W)


def _band_tile(q, kw, vw, band):
    s = _scores(_stack_heads(q), kw) + band
    m = jnp.max(s, axis=-1, keepdims=True)
    p = jnp.exp2(s - m)
    l = jnp.sum(p, axis=-1, keepdims=True)
    o = jnp.dot(p.astype(BF16), vw, preferred_element_type=F32) / l
    lse = jnp.broadcast_to(m + jnp.log2(l), (2 * TQ_C, LANES))
    return _merge_heads(o), _merge_heads(lse)


def _attn_c_kernel(c1_ref, c4_ref, c16_ref, band_ref, o_ref, o2_ref, l2_ref, o3_ref, l3_ref, *, seq):
    def branch(ref, window, dil, r, emit):
        length = seq // dil
        half = window // 2 // dil
        n_q = length // TQ_C
        win = min(TQ_C + 2 * half, length)
        base = r * 3 * LANES

        def tile(qi, band):
            if isinstance(qi, int):
                q0 = qi * TQ_C
                ks = min(max(q0 - half, 0), length - win)
            else:
                q0 = pl.multiple_of(qi * TQ_C, TQ_C)
                ks = pl.multiple_of(jnp.clip(q0 - half, 0, length - win), half)
            o, lse = _band_tile(ref[0, pl.ds(q0, TQ_C), base:base + LANES],
                                ref[0, pl.ds(ks, win), base + LANES:base + 2 * LANES],
                                ref[0, pl.ds(ks, win), base + 2 * LANES:base + 3 * LANES], band)
            emit(q0, o, lse)

        def interior(qi, carry):
            tile(qi, band_ref[1])
            return carry

        tile(0, band_ref[0])
        if n_q > 2:
            lax.fori_loop(1, n_q - 1, interior, 0, unroll=min(C_UNROLL, n_q - 2))
        tile(n_q - 1, band_ref[2])

    for ref, (window, dil), o_scr, l_scr in ((c4_ref, C_BRANCHES[1], o2_ref, l2_ref),
                                            (c16_ref, C_BRANCHES[2], o3_ref, l3_ref)):
        for r in range(dil):
            def emit(q0, o, lse, r=r, dil=dil, o_scr=o_scr, l_scr=l_scr):
                rows = pl.ds(r + dil * q0, TQ_C, stride=dil)
                o_scr[rows, :] = o
                l_scr[rows, :] = lse
            branch(ref, window, dil, r, emit)

    def emit1(q0, o1, l1):
        rows = pl.ds(q0, TQ_C)
        l2, l3 = l2_ref[rows, :], l3_ref[rows, :]
        lm = jnp.maximum(jnp.maximum(l1, l2), l3)
        e1, e2, e3 = jnp.exp2(l1 - lm), jnp.exp2(l2 - lm), jnp.exp2(l3 - lm)
        o_ref[0, rows, :] = (e1 * o1 + e2 * o2_ref[rows, :] + e3 * o3_ref[rows, :]) / (e1 + e2 + e3)

    branch(c1_ref, C_BRANCHES[0][0], 1, 0, emit1)


def _band_masks():
    half = C_BRANCHES[0][0] // 2
    assert all(w // 2 // d == half for w, d in C_BRANCHES)
    i = np.arange(2 * TQ_C)[:, None] % TQ_C
    j = np.arange(TQ_C + 2 * half)[None, :]
    return jnp.asarray(np.stack([np.where(np.abs(off + i - j) <= half, 0.0, NEG_INF)
                                 for off in (0, half, 2 * half)]).astype(np.float32))


def _attn_c_call(c1, c4, c16, bsz, seq):
    views = [c.reshape(bsz, seq // dil, dil * C_COLS) for c, dil in zip((c1, c4, c16), DILS)]
    spec = lambda dil: pl.BlockSpec((1, seq // dil, dil * 3 * LANES), lambda b, p: (b, 0, p))
    band = _band_masks()
    assert seq // DILS[-1] >= band.shape[2]
    return pl.pallas_call(
        functools.partial(_attn_c_kernel, seq=seq),
        grid=(bsz, N_PAIRS_C),
        in_specs=[spec(dil) for dil in DILS] + [pl.BlockSpec(band.shape, lambda b, p: (0, 0, 0))],
        out_specs=pl.BlockSpec((1, seq, LANES), lambda b, p: (b, 0, p)),
        out_shape=jax.ShapeDtypeStruct((bsz, seq, C_W), F32),
        scratch_shapes=[pltpu.VMEM((seq, LANES), F32)] * 4,
        compiler_params=pltpu.CompilerParams(dimension_semantics=("parallel",) * 2,
                                             vmem_limit_bytes=VMEM_LIMIT),
    )(*views, band)


def _out_kernel(oa_ref, ob_ref, oc_ref, h_ref, ga_ref, gb_ref, gc_ref, w_ref, gf_ref, wr_ref,
                h1_ref, hn_ref, lt_ref):
    mixed = jnp.concatenate([_rms(oa_ref[...], ga_ref[...]),
                             _rms(ob_ref[...], gb_ref[...]),
                             _rms(oc_ref[...], gc_ref[...])], axis=-1).astype(BF16)
    h1 = h_ref[...] + jnp.dot(mixed, w_ref[...], preferred_element_type=F32)
    h1_ref[...] = h1
    hn = _rms(h1, gf_ref[...])
    hi = hn.astype(BF16)
    hn_ref[...] = hi
    lo = (hn - hi.astype(F32)).astype(BF16)
    both = jnp.dot(hi, wr_ref[...], preferred_element_type=F32)
    logits = (both[:, :LANES] + both[:, LANES:]
              + jnp.dot(lo, wr_ref[:, :LANES], preferred_element_type=F32))
    lt_ref[...] = logits.T[:N_EXPERTS, :]


def _out_call(oa, ob, oc, h2d, ga, gb, gc, w, gf, wr):
    t_tot = h2d.shape[0]
    row = lambda i: (i, 0)
    fixed = lambda i: (0, 0)
    rs = lambda w_: pl.BlockSpec((TM, w_), row)
    fs = lambda a: pl.BlockSpec(a.shape, fixed)
    return pl.pallas_call(
        _out_kernel,
        grid=(t_tot // TM,),
        in_specs=[rs(A_Q_W), rs(B_W), rs(C_W), rs(D_MODEL)]
                 + [fs(ga), fs(gb), fs(gc), fs(w), fs(gf), fs(wr)],
        out_specs=[rs(D_MODEL), rs(D_MODEL), pl.BlockSpec((N_EXPERTS, TM), lambda i: (0, i))],
        out_shape=[jax.ShapeDtypeStruct((t_tot, D_MODEL), F32),
                   jax.ShapeDtypeStruct((t_tot, D_MODEL), BF16),
                   jax.ShapeDtypeStruct((N_EXPERTS, t_tot), F32)],
        compiler_params=pltpu.CompilerParams(dimension_semantics=("parallel",),
                                             vmem_limit_bytes=VMEM_LIMIT),
    )(oa, ob, oc, h2d, ga, gb, gc, w, gf, wr)


def _route_kernel(lt_ref, tri_ref, blk_ref, pos_ref, gate_ref, meta_ref, *, cap):
    l = lt_ref[...]
    seq = l.shape[1]
    m = jnp.max(l, axis=0, keepdims=True)
    ex = jnp.exp(l - m)
    aff = ex / jnp.sum(ex, axis=0, keepdims=True)
    gate_ref[0] = aff
    bits = lax.bitcast_convert_type(aff, I32)

    def count(mask):
        return jnp.sum(mask.astype(F32), axis=1, keepdims=True)

    def search(i, t):
        cand = t | (1 << (30 - i))
        return jnp.where(count(bits >= cand) >= cap, cand, t)

    thr = lax.fori_loop(0, 31, search, jnp.zeros((N_EXPERTS, 1), I32))
    gt = bits > thr
    eq = bits == thr
    need = cap - count(gt)

    tri = tri_ref[...]

    def excl_prefix(mask):
        mb = jnp.where(mask, 1.0, 0.0).astype(BF16)
        parts = []
        run = jnp.zeros((N_EXPERTS, 1), F32)
        for c in range(seq // LANES):
            ch = mb[:, c * LANES:(c + 1) * LANES]
            parts.append(jnp.dot(ch, tri, preferred_element_type=F32) + run)
            run = run + jnp.sum(ch.astype(F32), axis=1, keepdims=True)
        return jnp.concatenate(parts, axis=1)

    sel = gt | (eq & (excl_prefix(eq) < need))
    pos = excl_prefix(sel)
    pos_ref[0] = jnp.where(sel, pos, -1.0).astype(I32)
    sb = jnp.where(sel, 1.0, 0.0).astype(BF16)
    sc = jnp.dot(sb, blk_ref[...], preferred_element_type=F32)
    start = sc[:, :LANES]
    cnt = sc[:, LANES:]
    ilo = jnp.minimum(jnp.floor(start * (1.0 / TOK_BLK)), cap // TOK_BLK - 1.0)
    cross = jnp.where(start + cnt > (ilo + 1.0) * TOK_BLK, 1.0, 0.0)
    lane = lax.broadcasted_iota(I32, cross.shape, 1)
    has = jnp.sum(cross, axis=1, keepdims=True)
    jstar = jnp.sum(cross * lane.astype(F32), axis=1, keepdims=True)
    ifix = cap // TOK_BLK - has
    n_tb = seq // TOK_BLK
    meta = jnp.where(lane < n_tb, ilo, jnp.where(lane == n_tb, jstar, jnp.where(lane == n_tb + 1, ifix, 0.0)))
    meta_ref[0] = meta.astype(I32)


def _route_call(lt, tri, blk, bsz, seq, cap):
    e = N_EXPERTS
    big = lambda dt: jax.ShapeDtypeStruct((bsz, e, seq), dt)
    small = jax.ShapeDtypeStruct((bsz, e, LANES), I32)
    bspec = pl.BlockSpec((1, e, seq), lambda b: (b, 0, 0))
    sspec = pl.BlockSpec((1, e, LANES), lambda b: (b, 0, 0))
    return pl.pallas_call(
        functools.partial(_route_kernel, cap=cap),
        grid=(bsz,),
        in_specs=[pl.BlockSpec((e, seq), lambda b: (0, b)),
                  pl.BlockSpec(tri.shape, lambda b: (0, 0)),
                  pl.BlockSpec(blk.shape, lambda b: (0, 0))],
        out_specs=[bspec, bspec, sspec],
        out_shape=[big(I32), big(F32), small],
        compiler_params=pltpu.CompilerParams(dimension_semantics=("parallel",),
                                             vmem_limit_bytes=VMEM_LIMIT),
    )(lt, tri, blk)


def _moe_kernel(meta_ref, hn_ref, pos_ref, gate_ref, wg_ref, wu_ref, wd_ref, o_ref,
                acc_ref, xg_ref, y_ref, *, cap, n_tb):
    b = pl.program_id(0)
    k = pl.program_id(1)
    out_rows = o_ref.shape[1]

    @pl.when(k == 0)
    def _():
        acc_ref[...] = jnp.zeros_like(acc_ref)
        y_ref[cap:, :] = jnp.zeros((TOK_BLK, y_ref.shape[1]), BF16)

    @pl.when(k < N_EXPERTS)
    def _():
        xg_ref[...] = jnp.zeros_like(xg_ref)
        slot = lax.broadcasted_iota(I32, (TOK_BLK, TOK_BLK), 0)
        mbase = (b * N_EXPERTS + k) * META_W
        jstar = meta_ref[mbase + n_tb]
        ifix = meta_ref[mbase + n_tb + 1]

        def gather(j, i):
            lp = pos_ref[0, 0, j]
            hb = hn_ref[0, pl.ds(pl.multiple_of(j * TOK_BLK, TOK_BLK), TOK_BLK), :]
            oh = jnp.where(slot + i * TOK_BLK == lp, 1.0, 0.0).astype(BF16)
            r0 = pl.multiple_of(i * TOK_BLK, TOK_BLK)
            xg_ref[pl.ds(r0, TOK_BLK), :] += jnp.dot(oh, hb, preferred_element_type=F32).astype(BF16)

        def gather_body(j, carry):
            gather(j, meta_ref[mbase + j])
            return carry

        lax.fori_loop(0, n_tb, gather_body, 0, unroll=MOE_UNROLL)
        gather(jstar, ifix)

        x = xg_ref[:cap, :]
        y = jnp.zeros((cap, D_MODEL), F32)
        d_ff = wg_ref.shape[2]
        for fc in range(d_ff // MXU_N):
            cs = slice(fc * MXU_N, (fc + 1) * MXU_N)
            g = jnp.dot(x, wg_ref[0, :, cs], preferred_element_type=F32)
            u = jnp.dot(x, wu_ref[0, :, cs], preferred_element_type=F32)
            hid = (g * jax.nn.sigmoid(g) * u).astype(BF16)
            y = y + jnp.dot(hid, wd_ref[0, cs, :], preferred_element_type=F32)
        y_ref[:cap, :] = y.astype(BF16)

        def scatter(j, i):
            lp = pos_ref[0, 0, j]
            gr = gate_ref[0, 0, j]
            t0 = pl.multiple_of(j * TOK_BLK, TOK_BLK)
            oh = jnp.where(slot + i * TOK_BLK == lp, gr, 0.0).astype(BF16)
            r0 = pl.multiple_of(i * TOK_BLK, TOK_BLK)
            acc_ref[pl.ds(t0, TOK_BLK), :] += lax.dot_general(
                oh, y_ref[pl.ds(r0, TOK_BLK), :], (((0,), (0,)), ((), ())),
                preferred_element_type=F32)

        def scatter_body(j, carry):
            scatter(j, meta_ref[mbase + j])
            return carry

        lax.fori_loop(0, n_tb, scatter_body, 0, unroll=MOE_UNROLL)
        scatter(jstar, ifix)

    @pl.when(k >= N_EXPERTS)
    def _():
        r0 = pl.multiple_of((k - N_EXPERTS) * out_rows, out_rows)
        o_ref[0] = acc_ref[pl.ds(r0, out_rows), :]


def _moe_call(meta, hn, pos, gate, wg, wu, wd, cap, layer):
    bsz, seq, d = hn.shape
    n_tb = seq // TOK_BLK
    e_idx = lambda k: jnp.minimum(k, N_EXPERTS - 1)
    tok_spec = pl.BlockSpec((1, 1, n_tb, 1, TOK_BLK), lambda b, k, *_: (b, e_idx(k), 0, 0, 0))
    w_spec = lambda w: pl.BlockSpec((1,) + w.shape[1:],
                                    lambda b, k, *_: (layer * N_EXPERTS + e_idx(k), 0, 0))
    out_rows = seq // N_OUT_CHUNKS
    grid_spec = pltpu.PrefetchScalarGridSpec(
        num_scalar_prefetch=1,
        grid=(bsz, N_EXPERTS + N_OUT_CHUNKS),
        in_specs=[pl.BlockSpec((1, seq, d), lambda b, k, *_: (b, 0, 0), pipeline_mode=pl.Buffered(1)),
                  tok_spec, tok_spec, w_spec(wg), w_spec(wu), w_spec(wd)],
        out_specs=pl.BlockSpec((1, out_rows, d),
                               lambda b, k, *_: (b, jnp.maximum(k - N_EXPERTS, 0), 0)),
        scratch_shapes=[pltpu.VMEM((seq, d), F32), pltpu.VMEM((cap + TOK_BLK, d), BF16),
                        pltpu.VMEM((cap + TOK_BLK, d), BF16)],
    )
    return pl.pallas_call(
        functools.partial(_moe_kernel, cap=cap, n_tb=n_tb),
        grid_spec=grid_spec,
        out_shape=jax.ShapeDtypeStruct((bsz, seq, d), F32),
        compiler_params=pltpu.CompilerParams(dimension_semantics=("parallel", "arbitrary"),
                                             vmem_limit_bytes=VMEM_LIMIT_MOE),
    )(meta, hn, pos, gate, wg, wu, wd)


def _ple_kernel(h_ref, moe_ref, p_ref, g_ref, wg_ref, wp_ref, o_ref):
    h2 = h_ref[...] + moe_ref[...]
    a = _rms(h2, g_ref[...]).astype(BF16)
    gate = jax.nn.sigmoid(jnp.dot(a, wg_ref[...], preferred_element_type=F32))
    proj = jnp.dot(p_ref[...].astype(BF16), wp_ref[...], preferred_element_type=F32)
    o_ref[...] = h2 + gate * proj


def _ple_call(h1, moe, p2d, g, wg, wp, layer):
    t_tot = h1.shape[0]
    row = lambda i: (i, 0)
    p_row = lambda i: (layer * (t_tot // TM) + i, 0)
    fixed = lambda i: (0, 0)
    return pl.pallas_call(
        _ple_kernel,
        grid=(t_tot // TM,),
        in_specs=[pl.BlockSpec((TM, D_MODEL), row), pl.BlockSpec((TM, D_MODEL), row),
                  pl.BlockSpec((TM, p2d.shape[1]), p_row), pl.BlockSpec((1, D_MODEL), fixed),
                  pl.BlockSpec(wg.shape, fixed), pl.BlockSpec(wp.shape, fixed)],
        out_specs=pl.BlockSpec((TM, D_MODEL), row),
        out_shape=jax.ShapeDtypeStruct((t_tot, D_MODEL), F32),
        compiler_params=pltpu.CompilerParams(dimension_semantics=("parallel",),
                                             vmem_limit_bytes=VMEM_LIMIT),
    )(h1, moe, p2d, g, wg, wp)


def _rope_tables(seq):
    pos = jnp.arange(seq)

    def angles(pos_f, n, theta):
        inv = theta ** (-jnp.arange(0, n, 2, dtype=F32) / n)
        return pos_f[:, None] * inv[None, :]

    ar = angles((pos // GRID_W).astype(F32), HEAD_DIM // 2, AXIAL_THETA)
    ac = angles((pos % GRID_W).astype(F32), HEAD_DIM // 2, AXIAL_THETA)
    a1 = angles(pos.astype(F32), ROPE_DIMS, ROPE_THETA)
    z16 = jnp.zeros_like(ar)
    rep = LANES // HEAD_DIM
    cos_a = jnp.tile(jnp.concatenate([jnp.cos(ar), jnp.cos(ar), jnp.cos(ac), jnp.cos(ac)], -1), (1, rep))
    sm_a = jnp.tile(jnp.concatenate([-jnp.sin(ar), z16, -jnp.sin(ac), z16], -1), (1, rep))
    sp_a = jnp.tile(jnp.concatenate([z16, jnp.sin(ar), z16, jnp.sin(ac)], -1), (1, rep))
    z8 = jnp.zeros_like(a1)
    rest = HEAD_DIM - ROPE_DIMS
    ones = jnp.ones((seq, rest), F32)
    zr = jnp.zeros((seq, rest), F32)
    cos_c = jnp.tile(jnp.concatenate([jnp.cos(a1), jnp.cos(a1), ones], -1), (1, rep))
    sm_c = jnp.tile(jnp.concatenate([-jnp.sin(a1), z8, zr], -1), (1, rep))
    sp_c = jnp.tile(jnp.concatenate([z8, jnp.sin(a1), zr], -1), (1, rep))
    return cos_a, sm_a, sp_a, cos_c, sm_c, sp_c


_A_HEAD_ORDER = (0, 2, 1, 3)
_A_PERM = np.concatenate([np.arange(h * HEAD_DIM, (h + 1) * HEAD_DIM) for h in _A_HEAD_ORDER])


def _gain_row(qg):
    scale = HEAD_DIM ** -0.5 * LOG2_E
    one = lambda n: jnp.ones((n,), F32)
    return jnp.concatenate([
        jnp.tile(qg[0, 0] * scale, A_Q_HEADS), jnp.tile(qg[0, 1], A_KV_HEADS), one(A_KV_W),
        jnp.tile(qg[1, 0] * scale, B_HEADS), jnp.tile(qg[1, 1], B_HEADS), one(B_W),
        jnp.tile(qg[2, 0] * scale, C_HEADS), jnp.tile(qg[2, 1], C_HEADS), one(C_W)])[None, :]


def kernel(x, p, g_mix, w_in, qk_gain, na_bias, g_out, w_out, g_ffn, w_router,
           w_gate, w_up, w_down, g_ple, w_ple_gate, w_ple_proj):
    bsz, seq, d = x.shape
    depth = w_in.shape[0]
    t_tot = bsz * seq
    cap = max(1, EC_CAPACITY * seq // N_EXPERTS)
    n_tb = seq // TOK_BLK

    tabs = _rope_tables(seq)
    lane = np.arange(LANES)
    col = np.arange(MXU_N)
    gsum = jnp.asarray((col[:, None] // HEAD_DIM) == (col[None, :] // HEAD_DIM), BF16)
    tri = jnp.asarray(lane[:, None] < lane[None, :], BF16)
    tok = np.arange(seq)[:, None]
    j = np.arange(LANES)[None, :]
    blk_start = (tok < j * TOK_BLK) & (j < n_tb)
    blk_count = (tok // TOK_BLK == j) & (j < n_tb)
    blk = jnp.asarray(np.concatenate([blk_start, blk_count], axis=1), BF16)

    bias = _na_bias_tiles(na_bias, seq // GRID_W)
    stack = lambda w: w.reshape((depth * N_EXPERTS,) + w.shape[2:])
    wg_all, wu_all, wd_all = stack(w_gate), stack(w_up), stack(w_down)
    p_all = p.reshape(depth * t_tot, p.shape[-1])
    h = x.reshape(t_tot, d)
    for i in range(depth):
        w_i = jnp.concatenate([w_in[i][:, _A_PERM], w_in[i][:, A_Q_W:]], axis=1).astype(BF16)
        ab, c1, c4, c16 = _proj_call(h, g_mix[i][None], w_i, _gain_row(qk_gain[i]), gsum, tabs, seq)
        ab = ab.reshape(bsz, seq, AB_COLS)
        oa = _attn_a_call(ab).reshape(t_tot, A_Q_W)
        ob = _attn_b_call(ab, bias[i]).reshape(t_tot, B_W)
        oc = _attn_c_call(c1, c4, c16, bsz, seq).reshape(t_tot, C_W)
        go = g_out[i]
        w_o = jnp.concatenate([w_out[i][_A_PERM], w_out[i][A_Q_W:]], axis=0).astype(BF16)
        wr = jnp.pad(w_router[i], ((0, 0), (0, LANES - N_EXPERTS)))
        wr_hi = wr.astype(BF16)
        wr_lo = (wr - wr_hi.astype(F32)).astype(BF16)
        h1, hn, lt = _out_call(oa, ob, oc, h, go[:A_Q_W][_A_PERM][None], go[None, A_Q_W:A_Q_W + B_W],
                               go[None, A_Q_W + B_W:], w_o, g_ffn[i][None],
                               jnp.concatenate([wr_hi, wr_lo], axis=1))
        pos, gate, meta = _route_call(lt, tri, blk, bsz, seq, cap)
        moe = _moe_call(meta[:, :, :META_W].reshape(-1),
                        hn.reshape(bsz, seq, d),
                        pos.reshape(bsz, N_EXPERTS, n_tb, 1, TOK_BLK),
                        gate.reshape(bsz, N_EXPERTS, n_tb, 1, TOK_BLK),
                        wg_all, wu_all, wd_all, cap, i)
        h = _ple_call(h1, moe.reshape(t_tot, d), p_all, g_ple[i][None],
                      w_ple_gate[i].astype(BF16), w_ple_proj[i].astype(BF16), i)
    return h.reshape(bsz, seq, d)
```

```python
import functools

import numpy as np
import jax
import jax.numpy as jnp
from jax import lax
from jax.experimental import pallas as pl
from jax.experimental.pallas import tpu as pltpu

F32 = jnp.float32
BF16 = jnp.bfloat16
I32 = jnp.int32

D_MODEL = 1024
HEAD_DIM = 64
A_Q_HEADS = 4
A_KV_HEADS = 2
B_HEADS = 6
C_HEADS = 6
A_Q_W = A_Q_HEADS * HEAD_DIM
A_KV_W = A_KV_HEADS * HEAD_DIM
B_W = B_HEADS * HEAD_DIM
C_W = C_HEADS * HEAD_DIM
IN_COLS = A_Q_W + 2 * A_KV_W + 3 * B_W + 3 * C_W
GRID_W = 64
AXIAL_THETA = 10000.0
NA_ROWS = 8
NA_COLS = 16
C_BRANCHES = ((128, 1), (512, 4), (2048, 16))
ROPE_THETA = 500000.0
ROPE_DIMS = HEAD_DIM // 4
N_EXPERTS = 16
EC_CAPACITY = 2
EPS = 1e-6
NEG_INF = -1e30
LOG2_E = 1.4426950408889634

LANES = 128
MXU_N = 256
VMEM_LIMIT = 56 * 1024 * 1024
VMEM_LIMIT_MOE = 60 * 1024 * 1024

N_CB = IN_COLS // LANES
CB_QA, CB_KA, CB_VA = 0, 2, 3
CB_QB, CB_KB, CB_VB = 4, 7, 10
CB_QC, CB_KC, CB_VC = 13, 16, 19
CB_KIND = ("A", "A", "A", "V") + ("N",) * 6 + ("V",) * 3 + ("C",) * 6 + ("V",) * 3
AB_COLS = CB_QC * LANES
C_COLS = 3 * C_W
N_PAIRS_C = C_W // LANES
DILS = tuple(d for _, d in C_BRANCHES)

TM = 512
TM_CHAIN = 256
TQ_A = 1024
TQ_A_CHAIN = 128
NA_QROWS = 4
NA_KROWS = 12
NA_UNROLL = 14
TQ_C = 128
C_UNROLL = 15
TOK_BLK = 256
MOE_UNROLL = 16
N_OUT_CHUNKS = 8
META_W = 32


def _rms(x, g):
    return x * lax.rsqrt(jnp.mean(x * x, axis=-1, keepdims=True) + EPS) * g


def _proj_kernel(h_ref, g_ref, w_ref, gain_ref, gsum_ref, ca_ref, sma_ref, spa_ref,
                 cc_ref, smc_ref, spc_ref, ab_ref, c1_ref, c4_ref, c16_ref, cs_ref):
    gsum = gsum_ref[...]
    per = MXU_N // LANES
    n_c = IN_COLS // MXU_N
    normed = [c for c in range(n_c) if any(CB_KIND[cb] != "V" for cb in range(c * per, (c + 1) * per))]
    for r0 in range(0, TM, TM_CHAIN):
        rows = slice(r0, r0 + TM_CHAIN)
        a = _rms(h_ref[rows, :], g_ref[...]).astype(BF16)
        accs = [jnp.dot(a, w_ref[:, c * MXU_N:(c + 1) * MXU_N], preferred_element_type=F32)
                for c in range(n_c)]
        sq = jnp.concatenate([accs[c] * accs[c] for c in normed], axis=0)
        hi = sq.astype(BF16)
        lo = (sq - hi.astype(F32)).astype(BF16)
        ss = jnp.dot(hi, gsum, preferred_element_type=F32) + jnp.dot(lo, gsum, preferred_element_type=F32)
        inv_all = lax.rsqrt(ss * (1.0 / HEAD_DIM) + EPS)
        for c in range(n_c):
            acc = accs[c]
            if c in normed:
                inv = inv_all[normed.index(c) * TM_CHAIN:(normed.index(c) + 1) * TM_CHAIN]
            for hf, cb in enumerate(range(c * per, (c + 1) * per)):
                t = acc[:, hf * LANES:(hf + 1) * LANES]
                kind = CB_KIND[cb]
                if kind != "V":
                    t = t * inv[:, hf * LANES:(hf + 1) * LANES] * gain_ref[:, cb * LANES:(cb + 1) * LANES]
                    if kind == "A":
                        sh = HEAD_DIM // 4
                        t = (t * ca_ref[rows, :] + pltpu.roll(t, LANES - sh, 1) * sma_ref[rows, :]
                             + pltpu.roll(t, sh, 1) * spa_ref[rows, :])
                    elif kind == "C":
                        sh = ROPE_DIMS // 2
                        t = (t * cc_ref[rows, :] + pltpu.roll(t, LANES - sh, 1) * smc_ref[rows, :]
                             + pltpu.roll(t, sh, 1) * spc_ref[rows, :])
                if cb < CB_QC:
                    ab_ref[rows, cb * LANES:(cb + 1) * LANES] = t.astype(BF16)
                else:
                    cs_ref[cb - CB_QC, rows, :] = t
        for p in range(N_PAIRS_C):
            for which in range(3):
                src = which * N_PAIRS_C + p
                for dil, ref in zip(DILS, (c1_ref, c4_ref, c16_ref)):
                    n = TM_CHAIN // dil
                    for r in range(dil):
                        dst = ((p * dil + r) * 3 + which) * LANES
                        ref[r0 // dil:r0 // dil + n, dst:dst + LANES] = (
                            cs_ref[src, pl.ds(r0 + r, n, stride=dil), :].astype(BF16))


def _proj_call(h2d, g, w, gain, gsum, tabs, seq):
    t_tot = h2d.shape[0]
    nseq = seq // TM
    row = lambda i: (i, 0)
    fixed = lambda i: (0, 0)
    tab = lambda i: (i % nseq, 0)
    return pl.pallas_call(
        _proj_kernel,
        grid=(t_tot // TM,),
        in_specs=[pl.BlockSpec((TM, D_MODEL), row),
                  pl.BlockSpec((1, D_MODEL), fixed),
                  pl.BlockSpec((D_MODEL, IN_COLS), fixed),
                  pl.BlockSpec((1, IN_COLS), fixed),
                  pl.BlockSpec((MXU_N, MXU_N), fixed)]
                 + [pl.BlockSpec((TM, LANES), tab)] * 6,
        out_specs=[pl.BlockSpec((TM, AB_COLS), row)]
                  + [pl.BlockSpec((TM // dil, dil * C_COLS), row) for dil in DILS],
        out_shape=[jax.ShapeDtypeStruct((t_tot, AB_COLS), BF16)]
                  + [jax.ShapeDtypeStruct((t_tot // dil, dil * C_COLS), BF16) for dil in DILS],
        scratch_shapes=[pltpu.VMEM((C_COLS // LANES, TM, LANES), F32)],
        compiler_params=pltpu.CompilerParams(dimension_semantics=("parallel",),
                                             vmem_limit_bytes=VMEM_LIMIT),
    )(h2d, g, w, gain, gsum, *tabs)


def _stack_heads(q):
    qf = q.astype(F32)
    lo = lax.broadcasted_iota(I32, qf.shape, 1) < HEAD_DIM
    return jnp.concatenate([jnp.where(lo, qf, 0.0), jnp.where(lo, 0.0, qf)], axis=0).astype(BF16)


def _merge_heads(o):
    n = o.shape[0] // 2
    lo = lax.broadcasted_iota(I32, (n, LANES), 1) < HEAD_DIM
    return jnp.where(lo, o[:n], o[n:])


def _scores(qs, k):
    return lax.dot_general(qs, k, (((1,), (1,)), ((), ())), preferred_element_type=F32)


def _attn_a_kernel(q_ref, k_ref, v_ref, o_ref):
    k = k_ref[0]
    v = v_ref[0]
    for rc in range(TQ_A // TQ_A_CHAIN):
        rows = slice(rc * TQ_A_CHAIN, (rc + 1) * TQ_A_CHAIN)
        for blk in range(A_Q_W // LANES):
            qs = _stack_heads(q_ref[0, rows, blk * LANES:(blk + 1) * LANES])
            s = _scores(qs, k)
            m = jnp.max(s, axis=-1, keepdims=True)
            p = jnp.exp2(s - m)
            l = jnp.sum(p, axis=-1, keepdims=True)
            o = jnp.dot(p.astype(BF16), v, preferred_element_type=F32) / l
            o_ref[0, rows, blk * LANES:(blk + 1) * LANES] = _merge_heads(o)


def _attn_a_call(qkv):
    bsz, seq, _ = qkv.shape
    return pl.pallas_call(
        _attn_a_kernel,
        grid=(bsz, seq // TQ_A),
        in_specs=[pl.BlockSpec((1, TQ_A, A_Q_W), lambda b, i: (b, i, 0)),
                  pl.BlockSpec((1, seq, LANES), lambda b, i: (b, 0, CB_KA)),
                  pl.BlockSpec((1, seq, LANES), lambda b, i: (b, 0, CB_VA))],
        out_specs=pl.BlockSpec((1, TQ_A, A_Q_W), lambda b, i: (b, i, 0)),
        out_shape=jax.ShapeDtypeStruct((bsz, seq, A_Q_W), F32),
        compiler_params=pltpu.CompilerParams(dimension_semantics=("parallel", "parallel"),
                                             vmem_limit_bytes=VMEM_LIMIT),
    )(qkv, qkv, qkv)


def _attn_b_kernel(q_ref, k_ref, v_ref, bias_ref, o_ref, *, n_rows):
    tq = NA_QROWS * GRID_W
    n_g = n_rows // NA_QROWS

    def group(g, cfg):
        q0 = pl.multiple_of(g * tq, tq)
        krow0 = jnp.clip(g * NA_QROWS - NA_ROWS // 2, 0, n_rows - NA_KROWS)
        t0 = pl.multiple_of(krow0 * GRID_W, GRID_W)
        kw = k_ref[0, pl.ds(t0, NA_KROWS * GRID_W), :]
        vw = v_ref[0, pl.ds(t0, NA_KROWS * GRID_W), :]
        s = _scores(_stack_heads(q_ref[0, pl.ds(q0, tq), :]), kw) + bias_ref[0, cfg]
        m = jnp.max(s, axis=-1, keepdims=True)
        p = jnp.exp2(s - m)
        l = jnp.sum(p, axis=-1, keepdims=True)
        o = jnp.dot(p.astype(BF16), vw, preferred_element_type=F32) / l
        o_ref[0, pl.ds(q0, tq), :] = _merge_heads(o)

    def interior(g, carry):
        group(g, 1)
        return carry

    group(0, 0)
    lax.fori_loop(1, n_g - 1, interior, 0, unroll=NA_UNROLL)
    group(n_g - 1, 2)


def _attn_b_call(qkv, bias):
    bsz, seq, _ = qkv.shape
    spec = lambda cb: pl.BlockSpec((1, seq, LANES), lambda p, b: (b, 0, cb + p))
    return pl.pallas_call(
        functools.partial(_attn_b_kernel, n_rows=seq // GRID_W),
        grid=(B_W // LANES, bsz),
        in_specs=[spec(CB_QB), spec(CB_KB), spec(CB_VB),
                  pl.BlockSpec((1,) + bias.shape[1:], lambda p, b: (p, 0, 0, 0))],
        out_specs=pl.BlockSpec((1, seq, LANES), lambda p, b: (b, 0, p)),
        out_shape=jax.ShapeDtypeStruct((bsz, seq, B_W), F32),
        compiler_params=pltpu.CompilerParams(dimension_semantics=("parallel",) * 2,
                                             vmem_limit_bytes=VMEM_LIMIT),
    )(qkv, qkv, qkv, bias)


def _na_bias_tiles(rpb, n_rows):
    wr = NA_ROWS
    n_ro, n_co = 2 * NA_ROWS - 1, 2 * NA_COLS - 1
    row_sel = np.zeros((3, NA_QROWS, NA_KROWS, n_ro), np.float32)
    for z, r0 in enumerate((0, NA_QROWS, n_rows - NA_QROWS)):
        start = int(np.clip(r0 - wr // 2, 0, n_rows - NA_KROWS))
        for qi in range(NA_QROWS):
            r = r0 + qi
            rs = int(np.clip(r - wr // 2, 0, n_rows - wr))
            for kj in range(NA_KROWS):
                krow = start + kj
                if rs <= krow < rs + wr:
                    row_sel[z, qi, kj, krow - r + NA_ROWS - 1] = 1.0
    col_sel = np.zeros((GRID_W, GRID_W, n_co), np.float32)
    for c in range(GRID_W):
        cstart = int(np.clip(c - NA_COLS // 2, 0, GRID_W - NA_COLS))
        for kc in range(cstart, cstart + NA_COLS):
            col_sel[c, kc, kc - c + NA_COLS - 1] = 1.0
    ok = (row_sel.sum(-1)[:, :, None, :, None] * col_sel.sum(-1)[None, None, :, None, :]) > 0
    outside = np.where(ok, 0.0, NEG_INF).astype(np.float32)[None, None, :, None]
    n_l = rpb.shape[0]
    pairs = (rpb.astype(F32) * LOG2_E).reshape(n_l, B_HEADS // 2, 2, n_ro, n_co)
    vals = jnp.einsum("zqka,lptab,cjb->lpztqckj", row_sel, pairs, col_sel,
                      precision=lax.Precision.HIGHEST) + outside
    return vals.reshape(n_l, B_HEADS // 2, 3, 2 * NA_QROWS * GRID_W, NA_KROWS * GRID_W)


def _band_tile(q, kw, vw, band):
    s = _scores(_stack_heads(q), kw) + band
    m = jnp.max(s, axis=-1, keepdims=True)
    p = jnp.exp2(s - m)
    l = jnp.sum(p, axis=-1, keepdims=True)
    o = jnp.dot(p.astype(BF16), vw, preferred_element_type=F32) / l
    lse = jnp.broadcast_to(m + jnp.log2(l), (2 * TQ_C, LANES))
    return _merge_heads(o), _merge_heads(lse)


def _attn_c_kernel(c1_ref, c4_ref, c16_ref, band_ref, o_ref, o2_ref, l2_ref, o3_ref, l3_ref, *, seq):
    def branch(ref, window, dil, r, emit):
        length = seq // dil
        half = window // 2 // dil
        n_q = length // TQ_C
        win = min(TQ_C + 2 * half, length)
        base = r * 3 * LANES

        def tile(qi, band):
            if isinstance(qi, int):
                q0 = qi * TQ_C
                ks = min(max(q0 - half, 0), length - win)
            else:
                q0 = pl.multiple_of(qi * TQ_C, TQ_C)
                ks = pl.multiple_of(jnp.clip(q0 - half, 0, length - win), half)
            o, lse = _band_tile(ref[0, pl.ds(q0, TQ_C), base:base + LANES],
                                ref[0, pl.ds(ks, win), base + LANES:base + 2 * LANES],
                                ref[0, pl.ds(ks, win), base + 2 * LANES:base + 3 * LANES], band)
            emit(q0, o, lse)

        def interior(qi, carry):
            tile(qi, band_ref[1])
            return carry

        tile(0, band_ref[0])
        if n_q > 2:
            lax.fori_loop(1, n_q - 1, interior, 0, unroll=min(C_UNROLL, n_q - 2))
        tile(n_q - 1, band_ref[2])

    for ref, (window, dil), o_scr, l_scr in ((c4_ref, C_BRANCHES[1], o2_ref, l2_ref),
                                            (c16_ref, C_BRANCHES[2], o3_ref, l3_ref)):
        for r in range(dil):
            def emit(q0, o, lse, r=r, dil=dil, o_scr=o_scr, l_scr=l_scr):
                rows = pl.ds(r + dil * q0, TQ_C, stride=dil)
                o_scr[rows, :] = o
                l_scr[rows, :] = lse
            branch(ref, window, dil, r, emit)

    def emit1(q0, o1, l1):
        rows = pl.ds(q0, TQ_C)
        l2, l3 = l2_ref[rows, :], l3_ref[rows, :]
        lm = jnp.maximum(jnp.maximum(l1, l2), l3)
        e1, e2, e3 = jnp.exp2(l1 - lm), jnp.exp2(l2 - lm), jnp.exp2(l3 - lm)
        o_ref[0, rows, :] = (e1 * o1 + e2 * o2_ref[rows, :] + e3 * o3_ref[rows, :]) / (e1 + e2 + e3)

    branch(c1_ref, C_BRANCHES[0][0], 1, 0, emit1)


def _band_masks():
    half = C_BRANCHES[0][0] // 2
    assert all(w // 2 // d == half for w, d in C_BRANCHES)
    i = np.arange(2 * TQ_C)[:, None] % TQ_C
    j = np.arange(TQ_C + 2 * half)[None, :]
    return jnp.asarray(np.stack([np.where(np.abs(off + i - j) <= half, 0.0, NEG_INF)
                                 for off in (0, half, 2 * half)]).astype(np.float32))


def _attn_c_call(c1, c4, c16, bsz, seq):
    views = [c.reshape(bsz, seq // dil, dil * C_COLS) for c, dil in zip((c1, c4, c16), DILS)]
    spec = lambda dil: pl.BlockSpec((1, seq // dil, dil * 3 * LANES), lambda b, p: (b, 0, p))
    band = _band_masks()
    assert seq // DILS[-1] >= band.shape[2]
    return pl.pallas_call(
        functools.partial(_attn_c_kernel, seq=seq),
        grid=(bsz, N_PAIRS_C),
        in_specs=[spec(dil) for dil in DILS] + [pl.BlockSpec(band.shape, lambda b, p: (0, 0, 0))],
        out_specs=pl.BlockSpec((1, seq, LANES), lambda b, p: (b, 0, p)),
        out_shape=jax.ShapeDtypeStruct((bsz, seq, C_W), F32),
        scratch_shapes=[pltpu.VMEM((seq, LANES), F32)] * 4,
        compiler_params=pltpu.CompilerParams(dimension_semantics=("parallel",) * 2,
                                             vmem_limit_bytes=VMEM_LIMIT),
    )(*views, band)


def _out_kernel(oa_ref, ob_ref, oc_ref, h_ref, ga_ref, gb_ref, gc_ref, w_ref, gf_ref, wr_ref,
                h1_ref, hn_ref, lt_ref):
    mixed = jnp.concatenate([_rms(oa_ref[...], ga_ref[...]),
                             _rms(ob_ref[...], gb_ref[...]),
                             _rms(oc_ref[...], gc_ref[...])], axis=-1).astype(BF16)
    h1 = h_ref[...] + jnp.dot(mixed, w_ref[...], preferred_element_type=F32)
    h1_ref[...] = h1
    hn = _rms(h1, gf_ref[...])
    hi = hn.astype(BF16)
    hn_ref[...] = hi
    lo = (hn - hi.astype(F32)).astype(BF16)
    both = jnp.dot(hi, wr_ref[...], preferred_element_type=F32)
    logits = (both[:, :LANES] + both[:, LANES:]
              + jnp.dot(lo, wr_ref[:, :LANES], preferred_element_type=F32))
    lt_ref[...] = logits.T[:N_EXPERTS, :]


def _out_call(oa, ob, oc, h2d, ga, gb, gc, w, gf, wr):
    t_tot = h2d.shape[0]
    row = lambda i: (i, 0)
    fixed = lambda i: (0, 0)
    rs = lambda w_: pl.BlockSpec((TM, w_), row)
    fs = lambda a: pl.BlockSpec(a.shape, fixed)
    return pl.pallas_call(
        _out_kernel,
        grid=(t_tot // TM,),
        in_specs=[rs(A_Q_W), rs(B_W), rs(C_W), rs(D_MODEL)]
                 + [fs(ga), fs(gb), fs(gc), fs(w), fs(gf), fs(wr)],
        out_specs=[rs(D_MODEL), rs(D_MODEL), pl.BlockSpec((N_EXPERTS, TM), lambda i: (0, i))],
        out_shape=[jax.ShapeDtypeStruct((t_tot, D_MODEL), F32),
                   jax.ShapeDtypeStruct((t_tot, D_MODEL), BF16),
                   jax.ShapeDtypeStruct((N_EXPERTS, t_tot), F32)],
        compiler_params=pltpu.CompilerParams(dimension_semantics=("parallel",),
                                             vmem_limit_bytes=VMEM_LIMIT),
    )(oa, ob, oc, h2d, ga, gb, gc, w, gf, wr)


def _route_kernel(lt_ref, tri_ref, blk_ref, pos_ref, gate_ref, meta_ref, *, cap):
    l = lt_ref[...]
    seq = l.shape[1]
    m = jnp.max(l, axis=0, keepdims=True)
    ex = jnp.exp(l - m)
    aff = ex / jnp.sum(ex, axis=0, keepdims=True)
    gate_ref[0] = aff
    bits = lax.bitcast_convert_type(aff, I32)

    def count(mask):
        return jnp.sum(mask.astype(F32), axis=1, keepdims=True)

    def search(i, t):
        cand = t | (1 << (30 - i))
        return jnp.where(count(bits >= cand) >= cap, cand, t)

    thr = lax.fori_loop(0, 31, search, jnp.zeros((N_EXPERTS, 1), I32))
    gt = bits > thr
    eq = bits == thr
    need = cap - count(gt)

    tri = tri_ref[...]

    def excl_prefix(mask):
        mb = jnp.where(mask, 1.0, 0.0).astype(BF16)
        parts = []
        run = jnp.zeros((N_EXPERTS, 1), F32)
        for c in range(seq // LANES):
            ch = mb[:, c * LANES:(c + 1) * LANES]
            parts.append(jnp.dot(ch, tri, preferred_element_type=F32) + run)
            run = run + jnp.sum(ch.astype(F32), axis=1, keepdims=True)
        return jnp.concatenate(parts, axis=1)

    sel = gt | (eq & (excl_prefix(eq) < need))
    pos = excl_prefix(sel)
    pos_ref[0] = jnp.where(sel, pos, -1.0).astype(I32)
    sb = jnp.where(sel, 1.0, 0.0).astype(BF16)
    sc = jnp.dot(sb, blk_ref[...], preferred_element_type=F32)
    start = sc[:, :LANES]
    cnt = sc[:, LANES:]
    ilo = jnp.minimum(jnp.floor(start * (1.0 / TOK_BLK)), cap // TOK_BLK - 1.0)
    cross = jnp.where(start + cnt > (ilo + 1.0) * TOK_BLK, 1.0, 0.0)
    lane = lax.broadcasted_iota(I32, cross.shape, 1)
    has = jnp.sum(cross, axis=1, keepdims=True)
    jstar = jnp.sum(cross * lane.astype(F32), axis=1, keepdims=True)
    ifix = cap // TOK_BLK - has
    n_tb = seq // TOK_BLK
    meta = jnp.where(lane < n_tb, ilo, jnp.where(lane == n_tb, jstar, jnp.where(lane == n_tb + 1, ifix, 0.0)))
    meta_ref[0] = meta.astype(I32)


def _route_call(lt, tri, blk, bsz, seq, cap):
    e = N_EXPERTS
    big = lambda dt: jax.ShapeDtypeStruct((bsz, e, seq), dt)
    small = jax.ShapeDtypeStruct((bsz, e, LANES), I32)
    bspec = pl.BlockSpec((1, e, seq), lambda b: (b, 0, 0))
    sspec = pl.BlockSpec((1, e, LANES), lambda b: (b, 0, 0))
    return pl.pallas_call(
        functools.partial(_route_kernel, cap=cap),
        grid=(bsz,),
        in_specs=[pl.BlockSpec((e, seq), lambda b: (0, b)),
                  pl.BlockSpec(tri.shape, lambda b: (0, 0)),
                  pl.BlockSpec(blk.shape, lambda b: (0, 0))],
        out_specs=[bspec, bspec, sspec],
        out_shape=[big(I32), big(F32), small],
        compiler_params=pltpu.CompilerParams(dimension_semantics=("parallel",),
                                             vmem_limit_bytes=VMEM_LIMIT),
    )(lt, tri, blk)


def _moe_kernel(meta_ref, hn_ref, pos_ref, gate_ref, wg_ref, wu_ref, wd_ref, o_ref,
                acc_ref, xg_ref, y_ref, *, cap, n_tb):
    b = pl.program_id(0)
    k = pl.program_id(1)
    out_rows = o_ref.shape[1]

    @pl.when(k == 0)
    def _():
        acc_ref[...] = jnp.zeros_like(acc_ref)
        y_ref[cap:, :] = jnp.zeros((TOK_BLK, y_ref.shape[1]), BF16)

    @pl.when(k < N_EXPERTS)
    def _():
        xg_ref[...] = jnp.zeros_like(xg_ref)
        slot = lax.broadcasted_iota(I32, (TOK_BLK, TOK_BLK), 0)
        mbase = (b * N_EXPERTS + k) * META_W
        jstar = meta_ref[mbase + n_tb]
        ifix = meta_ref[mbase + n_tb + 1]

        def gather(j, i):
            lp = pos_ref[0, 0, j]
            hb = hn_ref[0, pl.ds(pl.multiple_of(j * TOK_BLK, TOK_BLK), TOK_BLK), :]
            oh = jnp.where(slot + i * TOK_BLK == lp, 1.0, 0.0).astype(BF16)
            r0 = pl.multiple_of(i * TOK_BLK, TOK_BLK)
            xg_ref[pl.ds(r0, TOK_BLK), :] += jnp.dot(oh, hb, preferred_element_type=F32).astype(BF16)

        def gather_body(j, carry):
            gather(j, meta_ref[mbase + j])
            return carry

        lax.fori_loop(0, n_tb, gather_body, 0, unroll=MOE_UNROLL)
        gather(jstar, ifix)

        x = xg_ref[:cap, :]
        y = jnp.zeros((cap, D_MODEL), F32)
        d_ff = wg_ref.shape[2]
        for fc in range(d_ff // MXU_N):
            cs = slice(fc * MXU_N, (fc + 1) * MXU_N)
            g = jnp.dot(x, wg_ref[0, :, cs], preferred_element_type=F32)
            u = jnp.dot(x, wu_ref[0, :, cs], preferred_element_type=F32)
            hid = (g * jax.nn.sigmoid(g) * u).astype(BF16)
            y = y + jnp.dot(hid, wd_ref[0, cs, :], preferred_element_type=F32)
        y_ref[:cap, :] = y.astype(BF16)

        def scatter(j, i):
            lp = pos_ref[0, 0, j]
            gr = gate_ref[0, 0, j]
            t0 = pl.multiple_of(j * TOK_BLK, TOK_BLK)
            oh = jnp.where(slot + i * TOK_BLK == lp, gr, 0.0).astype(BF16)
            r0 = pl.multiple_of(i * TOK_BLK, TOK_BLK)
            acc_ref[pl.ds(t0, TOK_BLK), :] += lax.dot_general(
                oh, y_ref[pl.ds(r0, TOK_BLK), :], (((0,), (0,)), ((), ())),
                preferred_element_type=F32)

        def scatter_body(j, carry):
            scatter(j, meta_ref[mbase + j])
            return carry

        lax.fori_loop(0, n_tb, scatter_body, 0, unroll=MOE_UNROLL)
        scatter(jstar, ifix)

    @pl.when(k >= N_EXPERTS)
    def _():
        r0 = pl.multiple_of((k - N_EXPERTS) * out_rows, out_rows)
        o_ref[0] = acc_ref[pl.ds(r0, out_rows), :]


def _moe_call(meta, hn, pos, gate, wg, wu, wd, cap, layer):
    bsz, seq, d = hn.shape
    n_tb = seq // TOK_BLK
    e_idx = lambda k: jnp.minimum(k, N_EXPERTS - 1)
    tok_spec = pl.BlockSpec((1, 1, n_tb, 1, TOK_BLK), lambda b, k, *_: (b, e_idx(k), 0, 0, 0))
    w_spec = lambda w: pl.BlockSpec((1,) + w.shape[1:],
                                    lambda b, k, *_: (layer * N_EXPERTS + e_idx(k), 0, 0))
    out_rows = seq // N_OUT_CHUNKS
    grid_spec = pltpu.PrefetchScalarGridSpec(
        num_scalar_prefetch=1,
        grid=(bsz, N_EXPERTS + N_OUT_CHUNKS),
        in_specs=[pl.BlockSpec((1, seq, d), lambda b, k, *_: (b, 0, 0), pipeline_mode=pl.Buffered(1)),
                  tok_spec, tok_spec, w_spec(wg), w_spec(wu), w_spec(wd)],
        out_specs=pl.BlockSpec((1, out_rows, d),
                               lambda b, k, *_: (b, jnp.maximum(k - N_EXPERTS, 0), 0)),
        scratch_shapes=[pltpu.VMEM((seq, d), F32), pltpu.VMEM((cap + TOK_BLK, d), BF16),
                        pltpu.VMEM((cap + TOK_BLK, d), BF16)],
    )
    return pl.pallas_call(
        functools.partial(_moe_kernel, cap=cap, n_tb=n_tb),
        grid_spec=grid_spec,
        out_shape=jax.ShapeDtypeStruct((bsz, seq, d), F32),
        compiler_params=pltpu.CompilerParams(dimension_semantics=("parallel", "arbitrary"),
                                             vmem_limit_bytes=VMEM_LIMIT_MOE),
    )(meta, hn, pos, gate, wg, wu, wd)


def _ple_kernel(h_ref, moe_ref, p_ref, g_ref, wg_ref, wp_ref, o_ref):
    h2 = h_ref[...] + moe_ref[...]
    a = _rms(h2, g_ref[...]).astype(BF16)
    gate = jax.nn.sigmoid(jnp.dot(a, wg_ref[...], preferred_element_type=F32))
    proj = jnp.dot(p_ref[...].astype(BF16), wp_ref[...], preferred_element_type=F32)
    o_ref[...] = h2 + gate * proj


def _ple_call(h1, moe, p2d, g, wg, wp, layer):
    t_tot = h1.shape[0]
    row = lambda i: (i, 0)
    p_row = lambda i: (layer * (t_tot // TM) + i, 0)
    fixed = lambda i: (0, 0)
    return pl.pallas_call(
        _ple_kernel,
        grid=(t_tot // TM,),
        in_specs=[pl.BlockSpec((TM, D_MODEL), row), pl.BlockSpec((TM, D_MODEL), row),
                  pl.BlockSpec((TM, p2d.shape[1]), p_row), pl.BlockSpec((1, D_MODEL), fixed),
                  pl.BlockSpec(wg.shape, fixed), pl.BlockSpec(wp.shape, fixed)],
        out_specs=pl.BlockSpec((TM, D_MODEL), row),
        out_shape=jax.ShapeDtypeStruct((t_tot, D_MODEL), F32),
        compiler_params=pltpu.CompilerParams(dimension_semantics=("parallel",),
                                             vmem_limit_bytes=VMEM_LIMIT),
    )(h1, moe, p2d, g, wg, wp)


def _rope_tables(seq):
    pos = jnp.arange(seq)

    def angles(pos_f, n, theta):
        inv = theta ** (-jnp.arange(0, n, 2, dtype=F32) / n)
        return pos_f[:, None] * inv[None, :]

    ar = angles((pos // GRID_W).astype(F32), HEAD_DIM // 2, AXIAL_THETA)
    ac = angles((pos % GRID_W).astype(F32), HEAD_DIM // 2, AXIAL_THETA)
    a1 = angles(pos.astype(F32), ROPE_DIMS, ROPE_THETA)
    z16 = jnp.zeros_like(ar)
    rep = LANES // HEAD_DIM
    cos_a = jnp.tile(jnp.concatenate([jnp.cos(ar), jnp.cos(ar), jnp.cos(ac), jnp.cos(ac)], -1), (1, rep))
    sm_a = jnp.tile(jnp.concatenate([-jnp.sin(ar), z16, -jnp.sin(ac), z16], -1), (1, rep))
    sp_a = jnp.tile(jnp.concatenate([z16, jnp.sin(ar), z16, jnp.sin(ac)], -1), (1, rep))
    z8 = jnp.zeros_like(a1)
    rest = HEAD_DIM - ROPE_DIMS
    ones = jnp.ones((seq, rest), F32)
    zr = jnp.zeros((seq, rest), F32)
    cos_c = jnp.tile(jnp.concatenate([jnp.cos(a1), jnp.cos(a1), ones], -1), (1, rep))
    sm_c = jnp.tile(jnp.concatenate([-jnp.sin(a1), z8, zr], -1), (1, rep))
    sp_c = jnp.tile(jnp.concatenate([z8, jnp.sin(a1), zr], -1), (1, rep))
    return cos_a, sm_a, sp_a, cos_c, sm_c, sp_c


_A_HEAD_ORDER = (0, 2, 1, 3)
_A_PERM = np.concatenate([np.arange(h * HEAD_DIM, (h + 1) * HEAD_DIM) for h in _A_HEAD_ORDER])


def _gain_row(qg):
    scale = HEAD_DIM ** -0.5 * LOG2_E
    one = lambda n: jnp.ones((n,), F32)
    return jnp.concatenate([
        jnp.tile(qg[0, 0] * scale, A_Q_HEADS), jnp.tile(qg[0, 1], A_KV_HEADS), one(A_KV_W),
        jnp.tile(qg[1, 0] * scale, B_HEADS), jnp.tile(qg[1, 1], B_HEADS), one(B_W),
        jnp.tile(qg[2, 0] * scale, C_HEADS), jnp.tile(qg[2, 1], C_HEADS), one(C_W)])[None, :]


def kernel(x, p, g_mix, w_in, qk_gain, na_bias, g_out, w_out, g_ffn, w_router,
           w_gate, w_up, w_down, g_ple, w_ple_gate, w_ple_proj):
    bsz, seq, d = x.shape
    depth = w_in.shape[0]
    t_tot = bsz * seq
    cap = max(1, EC_CAPACITY * seq // N_EXPERTS)
    n_tb = seq // TOK_BLK

    tabs = _rope_tables(seq)
    lane = np.arange(LANES)
    col = np.arange(MXU_N)
    gsum = jnp.asarray((col[:, None] // HEAD_DIM) == (col[None, :] // HEAD_DIM), BF16)
    tri = jnp.asarray(lane[:, None] < lane[None, :], BF16)
    tok = np.arange(seq)[:, None]
    j = np.arange(LANES)[None, :]
    blk_start = (tok < j * TOK_BLK) & (j < n_tb)
    blk_count = (tok // TOK_BLK == j) & (j < n_tb)
    blk = jnp.asarray(np.concatenate([blk_start, blk_count], axis=1), BF16)

    bias = _na_bias_tiles(na_bias, seq // GRID_W)
    stack = lambda w: w.reshape((depth * N_EXPERTS,) + w.shape[2:])
    wg_all, wu_all, wd_all = stack(w_gate), stack(w_up), stack(w_down)
    p_all = p.reshape(depth * t_tot, p.shape[-1])
    h = x.reshape(t_tot, d)
    for i in range(depth):
        w_i = jnp.concatenate([w_in[i][:, _A_PERM], w_in[i][:, A_Q_W:]], axis=1).astype(BF16)
        ab, c1, c4, c16 = _proj_call(h, g_mix[i][None], w_i, _gain_row(qk_gain[i]), gsum, tabs, seq)
        ab = ab.reshape(bsz, seq, AB_COLS)
        oa = _attn_a_call(ab).reshape(t_tot, A_Q_W)
        ob = _attn_b_call(ab, bias[i]).reshape(t_tot, B_W)
        oc = _attn_c_call(c1, c4, c16, bsz, seq).reshape(t_tot, C_W)
        go = g_out[i]
        w_o = jnp.concatenate([w_out[i][_A_PERM], w_out[i][A_Q_W:]], axis=0).astype(BF16)
        wr = jnp.pad(w_router[i], ((0, 0), (0, LANES - N_EXPERTS)))
        wr_hi = wr.astype(BF16)
        wr_lo = (wr - wr_hi.astype(F32)).astype(BF16)
        h1, hn, lt = _out_call(oa, ob, oc, h, go[:A_Q_W][_A_PERM][None], go[None, A_Q_W:A_Q_W + B_W],
                               go[None, A_Q_W + B_W:], w_o, g_ffn[i][None],
                               jnp.concatenate([wr_hi, wr_lo], axis=1))
        pos, gate, meta = _route_call(lt, tri, blk, bsz, seq, cap)
        moe = _moe_call(meta[:, :, :META_W].reshape(-1),
                        hn.reshape(bsz, seq, d),
                        pos.reshape(bsz, N_EXPERTS, n_tb, 1, TOK_BLK),
                        gate.reshape(bsz, N_EXPERTS, n_tb, 1, TOK_BLK),
                        wg_all, wu_all, wd_all, cap, i)
        h = _ple_call(h1, moe.reshape(t_tot, d), p_all, g_ple[i][None],
                      w_ple_gate[i].astype(BF16), w_ple_proj[i].astype(BF16), i)
    return h.reshape(bsz, seq, d)
```

```python
import functools

import numpy as np
import jax
import jax.numpy as jnp
from jax import lax
from jax.experimental import pallas as pl
from jax.experimental.pallas import tpu as pltpu

F32 = jnp.float32
BF16 = jnp.bfloat16
I32 = jnp.int32

D_MODEL = 1024
HEAD_DIM = 64
A_Q_HEADS = 4
A_KV_HEADS = 2
B_HEADS = 6
C_HEADS = 6
A_Q_W = A_Q_HEADS * HEAD_DIM
A_KV_W = A_KV_HEADS * HEAD_DIM
B_W = B_HEADS * HEAD_DIM
C_W = C_HEADS * HEAD_DIM
IN_COLS = A_Q_W + 2 * A_KV_W + 3 * B_W + 3 * C_W
GRID_W = 64
AXIAL_THETA = 10000.0
NA_ROWS = 8
NA_COLS = 16
C_BRANCHES = ((128, 1), (512, 4), (2048, 16))
ROPE_THETA = 500000.0
ROPE_DIMS = HEAD_DIM // 4
N_EXPERTS = 16
EC_CAPACITY = 2
EPS = 1e-6
NEG_INF = -1e30
LOG2_E = 1.4426950408889634

LANES = 128
MXU_N = 256
VMEM_LIMIT = 56 * 1024 * 1024
VMEM_LIMIT_MOE = 60 * 1024 * 1024

N_CB = IN_COLS // LANES
CB_QA, CB_KA, CB_VA = 0, 2, 3
CB_QB, CB_KB, CB_VB = 4, 7, 10
CB_QC, CB_KC, CB_VC = 13, 16, 19
CB_KIND = ("A", "A", "A", "V") + ("N",) * 6 + ("V",) * 3 + ("C",) * 6 + ("V",) * 3
AB_COLS = CB_QC * LANES
C_COLS = 3 * C_W
N_PAIRS_C = C_W // LANES
DILS = tuple(d for _, d in C_BRANCHES)

TM = 512
TM_CHAIN = 256
TQ_A = 1024
TQ_A_CHAIN = 128
NA_QROWS = 4
NA_KROWS = 12
NA_UNROLL = 14
TQ_C = 128
C_UNROLL = 15
TOK_BLK = 256
MOE_UNROLL = 16
N_OUT_CHUNKS = 8
META_W = 32


def _rms(x, g):
    return x * lax.rsqrt(jnp.mean(x * x, axis=-1, keepdims=True) + EPS) * g


def _proj_kernel(h_ref, g_ref, w_ref, gain_ref, gsum_ref, ca_ref, sma_ref, spa_ref,
                 cc_ref, smc_ref, spc_ref, ab_ref, c1_ref, c4_ref, c16_ref, cs_ref):
    gsum = gsum_ref[...]
    per = MXU_N // LANES
    n_c = IN_COLS // MXU_N
    normed = [c for c in range(n_c) if any(CB_KIND[cb] != "V" for cb in range(c * per, (c + 1) * per))]
    for r0 in range(0, TM, TM_CHAIN):
        rows = slice(r0, r0 + TM_CHAIN)
        a = _rms(h_ref[rows, :], g_ref[...]).astype(BF16)
        accs = [jnp.dot(a, w_ref[:, c * MXU_N:(c + 1) * MXU_N], preferred_element_type=F32)
                for c in range(n_c)]
        sq = jnp.concatenate([accs[c] * accs[c] for c in normed], axis=0)
        hi = sq.astype(BF16)
        lo = (sq - hi.astype(F32)).astype(BF16)
        ss = jnp.dot(hi, gsum, preferred_element_type=F32) + jnp.dot(lo, gsum, preferred_element_type=F32)
        inv_all = lax.rsqrt(ss * (1.0 / HEAD_DIM) + EPS)
        for c in range(n_c):
            acc = accs[c]
            if c in normed:
                inv = inv_all[normed.index(c) * TM_CHAIN:(normed.index(c) + 1) * TM_CHAIN]
            for hf, cb in enumerate(range(c * per, (c + 1) * per)):
                t = acc[:, hf * LANES:(hf + 1) * LANES]
                kind = CB_KIND[cb]
                if kind != "V":
                    t = t * inv[:, hf * LANES:(hf + 1) * LANES] * gain_ref[:, cb * LANES:(cb + 1) * LANES]
                    if kind == "A":
                        sh = HEAD_DIM // 4
                        t = (t * ca_ref[rows, :] + pltpu.roll(t, LANES - sh, 1) * sma_ref[rows, :]
                             + pltpu.roll(t, sh, 1) * spa_ref[rows, :])
                    elif kind == "C":
                        sh = ROPE_DIMS // 2
                        t = (t * cc_ref[rows, :] + pltpu.roll(t, LANES - sh, 1) * smc_ref[rows, :]
                             + pltpu.roll(t, sh, 1) * spc_ref[rows, :])
                if cb < CB_QC:
                    ab_ref[rows, cb * LANES:(cb + 1) * LANES] = t.astype(BF16)
                else:
                    cs_ref[cb - CB_QC, rows, :] = t
        for p in range(N_PAIRS_C):
            for which in range(3):
                src = which * N_PAIRS_C + p
                for dil, ref in zip(DILS, (c1_ref, c4_ref, c16_ref)):
                    n = TM_CHAIN // dil
                    for r in range(dil):
                        dst = ((p * dil + r) * 3 + which) * LANES
                        ref[r0 // dil:r0 // dil + n, dst:dst + LANES] = (
                            cs_ref[src, pl.ds(r0 + r, n, stride=dil), :].astype(BF16))


def _proj_call(h2d, g, w, gain, gsum, tabs, seq):
    t_tot = h2d.shape[0]
    nseq = seq // TM
    row = lambda i: (i, 0)
    fixed = lambda i: (0, 0)
    tab = lambda i: (i % nseq, 0)
    return pl.pallas_call(
        _proj_kernel,
        grid=(t_tot // TM,),
        in_specs=[pl.BlockSpec((TM, D_MODEL), row),
                  pl.BlockSpec((1, D_MODEL), fixed),
                  pl.BlockSpec((D_MODEL, IN_COLS), fixed),
                  pl.BlockSpec((1, IN_COLS), fixed),
                  pl.BlockSpec((MXU_N, MXU_N), fixed)]
                 + [pl.BlockSpec((TM, LANES), tab)] * 6,
        out_specs=[pl.BlockSpec((TM, AB_COLS), row)]
                  + [pl.BlockSpec((TM // dil, dil * C_COLS), row) for dil in DILS],
        out_shape=[jax.ShapeDtypeStruct((t_tot, AB_COLS), BF16)]
                  + [jax.ShapeDtypeStruct((t_tot // dil, dil * C_COLS), BF16) for dil in DILS],
        scratch_shapes=[pltpu.VMEM((C_COLS // LANES, TM, LANES), F32)],
        compiler_params=pltpu.CompilerParams(dimension_semantics=("parallel",),
                                             vmem_limit_bytes=VMEM_LIMIT),
    )(h2d, g, w, gain, gsum, *tabs)


def _stack_heads(q):
    qf = q.astype(F32)
    lo = lax.broadcasted_iota(I32, qf.shape, 1) < HEAD_DIM
    return jnp.concatenate([jnp.where(lo, qf, 0.0), jnp.where(lo, 0.0, qf)], axis=0).astype(BF16)


def _merge_heads(o):
    n = o.shape[0] // 2
    lo = lax.broadcasted_iota(I32, (n, LANES), 1) < HEAD_DIM
    return jnp.where(lo, o[:n], o[n:])


def _scores(qs, k):
    return lax.dot_general(qs, k, (((1,), (1,)), ((), ())), preferred_element_type=F32)


def _attn_a_kernel(q_ref, k_ref, v_ref, o_ref):
    k = k_ref[0]
    v = v_ref[0]
    for rc in range(TQ_A // TQ_A_CHAIN):
        rows = slice(rc * TQ_A_CHAIN, (rc + 1) * TQ_A_CHAIN)
        for blk in range(A_Q_W // LANES):
            qs = _stack_heads(q_ref[0, rows, blk * LANES:(blk + 1) * LANES])
            s = _scores(qs, k)
            m = jnp.max(s, axis=-1, keepdims=True)
            p = jnp.exp2(s - m)
            l = jnp.sum(p, axis=-1, keepdims=True)
            o = jnp.dot(p.astype(BF16), v, preferred_element_type=F32) / l
            o_ref[0, rows, blk * LANES:(blk + 1) * LANES] = _merge_heads(o)


def _attn_a_call(qkv):
    bsz, seq, _ = qkv.shape
    return pl.pallas_call(
        _attn_a_kernel,
        grid=(bsz, seq // TQ_A),
        in_specs=[pl.BlockSpec((1, TQ_A, A_Q_W), lambda b, i: (b, i, 0)),
                  pl.BlockSpec((1, seq, LANES), lambda b, i: (b, 0, CB_KA)),
                  pl.BlockSpec((1, seq, LANES), lambda b, i: (b, 0, CB_VA))],
        out_specs=pl.BlockSpec((1, TQ_A, A_Q_W), lambda b, i: (b, i, 0)),
        out_shape=jax.ShapeDtypeStruct((bsz, seq, A_Q_W), F32),
        compiler_params=pltpu.CompilerParams(dimension_semantics=("parallel", "parallel"),
                                             vmem_limit_bytes=VMEM_LIMIT),
    )(qkv, qkv, qkv)


def _attn_b_kernel(q_ref, k_ref, v_ref, bias_ref, o_ref, *, n_rows, block_of):
    tq = NA_QROWS * GRID_W
    n_g = n_rows // NA_QROWS

    def bias_tile(cfg):
        return jnp.concatenate(
            [jnp.concatenate([bias_ref[0, t, block_of[cfg][qr][kb]] for kb in range(NA_KROWS // 2)], axis=1)
             for t in range(2) for qr in range(NA_QROWS)], axis=0)

    def group(g, cfg):
        q0 = pl.multiple_of(g * tq, tq)
        krow0 = jnp.clip(g * NA_QROWS - NA_ROWS // 2, 0, n_rows - NA_KROWS)
        t0 = pl.multiple_of(krow0 * GRID_W, GRID_W)
        kw = k_ref[0, pl.ds(t0, NA_KROWS * GRID_W), :]
        vw = v_ref[0, pl.ds(t0, NA_KROWS * GRID_W), :]
        s = _scores(_stack_heads(q_ref[0, pl.ds(q0, tq), :]), kw) + bias_tile(cfg)
        m = jnp.max(s, axis=-1, keepdims=True)
        p = jnp.exp2(s - m)
        l = jnp.sum(p, axis=-1, keepdims=True)
        o = jnp.dot(p.astype(BF16), vw, preferred_element_type=F32) / l
        o_ref[0, pl.ds(q0, tq), :] = _merge_heads(o)

    def interior(g, carry):
        group(g, 1)
        return carry

    group(0, 0)
    lax.fori_loop(1, n_g - 1, interior, 0, unroll=NA_UNROLL)
    group(n_g - 1, 2)


def _attn_b_call(qkv, bias, block_of):
    bsz, seq, _ = qkv.shape
    spec = lambda cb: pl.BlockSpec((1, seq, LANES), lambda p, b: (b, 0, cb + p))
    return pl.pallas_call(
        functools.partial(_attn_b_kernel, n_rows=seq // GRID_W, block_of=block_of),
        grid=(B_W // LANES, bsz),
        in_specs=[spec(CB_QB), spec(CB_KB), spec(CB_VB),
                  pl.BlockSpec((1,) + bias.shape[1:], lambda p, b: (p, 0, 0, 0, 0))],
        out_specs=pl.BlockSpec((1, seq, LANES), lambda p, b: (b, 0, p)),
        out_shape=jax.ShapeDtypeStruct((bsz, seq, B_W), F32),
        compiler_params=pltpu.CompilerParams(dimension_semantics=("parallel",) * 2,
                                             vmem_limit_bytes=VMEM_LIMIT),
    )(qkv, qkv, qkv, bias)


def _na_bias_blocks(rpb, n_rows):
    wr = NA_ROWS
    n_co = 2 * NA_COLS - 1
    col_sel = np.zeros((GRID_W, GRID_W, n_co), np.float32)
    for c in range(GRID_W):
        cstart = int(np.clip(c - NA_COLS // 2, 0, GRID_W - NA_COLS))
        for kc in range(cstart, cstart + NA_COLS):
            col_sel[c, kc, kc - c + NA_COLS - 1] = 1.0
    col_out = np.where(col_sel.sum(-1) > 0, 0.0, NEG_INF).astype(np.float32)
    per_off = jnp.einsum("lhab,cjb->lhacj", rpb.astype(F32) * LOG2_E, col_sel,
                         precision=lax.Precision.HIGHEST) + col_out
    masked = jnp.full(per_off.shape[:2] + (GRID_W, GRID_W), NEG_INF, F32)
    keys, block_of = [], []
    for r0 in (0, NA_QROWS, n_rows - NA_QROWS):
        start = int(np.clip(r0 - wr // 2, 0, n_rows - NA_KROWS))
        geometry = []
        for qi in range(NA_QROWS):
            r = r0 + qi
            rs = int(np.clip(r - wr // 2, 0, n_rows - wr))
            offs = [start + kj - r + NA_ROWS - 1 if rs <= start + kj < rs + wr else None
                    for kj in range(NA_KROWS)]
            row = []
            for kb in range(NA_KROWS // 2):
                key = (offs[2 * kb], offs[2 * kb + 1])
                if key not in keys:
                    keys.append(key)
                row.append(keys.index(key))
            geometry.append(tuple(row))
        block_of.append(tuple(geometry))
    half = lambda a: masked if a is None else per_off[:, :, a]
    blocks = jnp.stack([jnp.concatenate([half(a0), half(a1)], axis=-1) for a0, a1 in keys], axis=2)
    n_l = rpb.shape[0]
    blocks = blocks.reshape(n_l, B_HEADS // 2, 2, len(keys), GRID_W, 2 * GRID_W)
    return blocks, tuple(block_of)


def _band_tile(q, kw, vw, band):
    s = _scores(_stack_heads(q), kw) + band
    m = jnp.max(s, axis=-1, keepdims=True)
    p = jnp.exp2(s - m)
    l = jnp.sum(p, axis=-1, keepdims=True)
    o = jnp.dot(p.astype(BF16), vw, preferred_element_type=F32) / l
    lse = jnp.broadcast_to(m + jnp.log2(l), (2 * TQ_C, LANES))
    return _merge_heads(o), _merge_heads(lse)


def _attn_c_kernel(c1_ref, c4_ref, c16_ref, band_ref, o_ref, o2_ref, l2_ref, o3_ref, l3_ref, *, seq):
    def branch(ref, window, dil, r, emit):
        length = seq // dil
        half = window // 2 // dil
        n_q = length // TQ_C
        win = min(TQ_C + 2 * half, length)
        base = r * 3 * LANES

        def tile(qi, band):
            if isinstance(qi, int):
                q0 = qi * TQ_C
                ks = min(max(q0 - half, 0), length - win)
            else:
                q0 = pl.multiple_of(qi * TQ_C, TQ_C)
                ks = pl.multiple_of(jnp.clip(q0 - half, 0, length - win), half)
            o, lse = _band_tile(ref[0, pl.ds(q0, TQ_C), base:base + LANES],
                                ref[0, pl.ds(ks, win), base + LANES:base + 2 * LANES],
                                ref[0, pl.ds(ks, win), base + 2 * LANES:base + 3 * LANES], band)
            emit(q0, o, lse)

        def interior(qi, carry):
            tile(qi, band_ref[1])
            return carry

        tile(0, band_ref[0])
        if n_q > 2:
            lax.fori_loop(1, n_q - 1, interior, 0, unroll=min(C_UNROLL, n_q - 2))
        tile(n_q - 1, band_ref[2])

    for ref, (window, dil), o_scr, l_scr in ((c4_ref, C_BRANCHES[1], o2_ref, l2_ref),
                                            (c16_ref, C_BRANCHES[2], o3_ref, l3_ref)):
        for r in range(dil):
            def emit(q0, o, lse, r=r, dil=dil, o_scr=o_scr, l_scr=l_scr):
                rows = pl.ds(r + dil * q0, TQ_C, stride=dil)
                o_scr[rows, :] = o
                l_scr[rows, :] = lse
            branch(ref, window, dil, r, emit)

    def emit1(q0, o1, l1):
        rows = pl.ds(q0, TQ_C)
        l2, l3 = l2_ref[rows, :], l3_ref[rows, :]
        lm = jnp.maximum(jnp.maximum(l1, l2), l3)
        e1, e2, e3 = jnp.exp2(l1 - lm), jnp.exp2(l2 - lm), jnp.exp2(l3 - lm)
        o_ref[0, rows, :] = (e1 * o1 + e2 * o2_ref[rows, :] + e3 * o3_ref[rows, :]) / (e1 + e2 + e3)

    branch(c1_ref, C_BRANCHES[0][0], 1, 0, emit1)


def _band_masks():
    half = C_BRANCHES[0][0] // 2
    assert all(w // 2 // d == half for w, d in C_BRANCHES)
    i = np.arange(2 * TQ_C)[:, None] % TQ_C
    j = np.arange(TQ_C + 2 * half)[None, :]
    return jnp.asarray(np.stack([np.where(np.abs(off + i - j) <= half, 0.0, NEG_INF)
                                 for off in (0, half, 2 * half)]).astype(np.float32))


def _attn_c_call(c1, c4, c16, bsz, seq):
    views = [c.reshape(bsz, seq // dil, dil * C_COLS) for c, dil in zip((c1, c4, c16), DILS)]
    spec = lambda dil: pl.BlockSpec((1, seq // dil, dil * 3 * LANES), lambda b, p: (b, 0, p))
    band = _band_masks()
    assert seq // DILS[-1] >= band.shape[2]
    return pl.pallas_call(
        functools.partial(_attn_c_kernel, seq=seq),
        grid=(bsz, N_PAIRS_C),
        in_specs=[spec(dil) for dil in DILS] + [pl.BlockSpec(band.shape, lambda b, p: (0, 0, 0))],
        out_specs=pl.BlockSpec((1, seq, LANES), lambda b, p: (b, 0, p)),
        out_shape=jax.ShapeDtypeStruct((bsz, seq, C_W), F32),
        scratch_shapes=[pltpu.VMEM((seq, LANES), F32)] * 4,
        compiler_params=pltpu.CompilerParams(dimension_semantics=("parallel",) * 2,
                                             vmem_limit_bytes=VMEM_LIMIT),
    )(*views, band)


def _out_kernel(oa_ref, ob_ref, oc_ref, h_ref, ga_ref, gb_ref, gc_ref, w_ref, gf_ref, wr_ref,
                h1_ref, hn_ref, lt_ref):
    mixed = jnp.concatenate([_rms(oa_ref[...], ga_ref[...]),
                             _rms(ob_ref[...], gb_ref[...]),
                             _rms(oc_ref[...], gc_ref[...])], axis=-1).astype(BF16)
    h1 = h_ref[...] + jnp.dot(mixed, w_ref[...], preferred_element_type=F32)
    h1_ref[...] = h1
    hn = _rms(h1, gf_ref[...])
    hi = hn.astype(BF16)
    hn_ref[...] = hi
    lo = (hn - hi.astype(F32)).astype(BF16)
    both = jnp.dot(hi, wr_ref[...], preferred_element_type=F32)
    logits = (both[:, :LANES] + both[:, LANES:]
              + jnp.dot(lo, wr_ref[:, :LANES], preferred_element_type=F32))
    lt_ref[...] = logits.T[:N_EXPERTS, :]


def _out_call(oa, ob, oc, h2d, ga, gb, gc, w, gf, wr):
    t_tot = h2d.shape[0]
    row = lambda i: (i, 0)
    fixed = lambda i: (0, 0)
    rs = lambda w_: pl.BlockSpec((TM, w_), row)
    fs = lambda a: pl.BlockSpec(a.shape, fixed)
    return pl.pallas_call(
        _out_kernel,
        grid=(t_tot // TM,),
        in_specs=[rs(A_Q_W), rs(B_W), rs(C_W), rs(D_MODEL)]
                 + [fs(ga), fs(gb), fs(gc), fs(w), fs(gf), fs(wr)],
        out_specs=[rs(D_MODEL), rs(D_MODEL), pl.BlockSpec((N_EXPERTS, TM), lambda i: (0, i))],
        out_shape=[jax.ShapeDtypeStruct((t_tot, D_MODEL), F32),
                   jax.ShapeDtypeStruct((t_tot, D_MODEL), BF16),
                   jax.ShapeDtypeStruct((N_EXPERTS, t_tot), F32)],
        compiler_params=pltpu.CompilerParams(dimension_semantics=("parallel",),
                                             vmem_limit_bytes=VMEM_LIMIT),
    )(oa, ob, oc, h2d, ga, gb, gc, w, gf, wr)


def _route_kernel(lt_ref, tri_ref, blk_ref, pos_ref, gate_ref, meta_ref, *, cap):
    l = lt_ref[...]
    seq = l.shape[1]
    m = jnp.max(l, axis=0, keepdims=True)
    ex = jnp.exp(l - m)
    aff = ex / jnp.sum(ex, axis=0, keepdims=True)
    gate_ref[0] = aff
    bits = lax.bitcast_convert_type(aff, I32)

    def count(mask):
        return jnp.sum(mask.astype(F32), axis=1, keepdims=True)

    def search(i, t):
        cand = t | (1 << (30 - i))
        return jnp.where(count(bits >= cand) >= cap, cand, t)

    thr = lax.fori_loop(0, 31, search, jnp.zeros((N_EXPERTS, 1), I32))
    gt = bits > thr
    eq = bits == thr
    need = cap - count(gt)

    tri = tri_ref[...]

    def excl_prefix(mask):
        mb = jnp.where(mask, 1.0, 0.0).astype(BF16)
        parts = []
        run = jnp.zeros((N_EXPERTS, 1), F32)
        for c in range(seq // LANES):
            ch = mb[:, c * LANES:(c + 1) * LANES]
            parts.append(jnp.dot(ch, tri, preferred_element_type=F32) + run)
            run = run + jnp.sum(ch.astype(F32), axis=1, keepdims=True)
        return jnp.concatenate(parts, axis=1)

    sel = gt | (eq & (excl_prefix(eq) < need))
    pos = excl_prefix(sel)
    pos_ref[0] = jnp.where(sel, pos, -1.0).astype(I32)
    sb = jnp.where(sel, 1.0, 0.0).astype(BF16)
    sc = jnp.dot(sb, blk_ref[...], preferred_element_type=F32)
    start = sc[:, :LANES]
    cnt = sc[:, LANES:]
    ilo = jnp.minimum(jnp.floor(start * (1.0 / TOK_BLK)), cap // TOK_BLK - 1.0)
    cross = jnp.where(start + cnt > (ilo + 1.0) * TOK_BLK, 1.0, 0.0)
    lane = lax.broadcasted_iota(I32, cross.shape, 1)
    has = jnp.sum(cross, axis=1, keepdims=True)
    jstar = jnp.sum(cross * lane.astype(F32), axis=1, keepdims=True)
    ifix = cap // TOK_BLK - has
    n_tb = seq // TOK_BLK
    meta = jnp.where(lane < n_tb, ilo, jnp.where(lane == n_tb, jstar, jnp.where(lane == n_tb + 1, ifix, 0.0)))
    meta_ref[0] = meta.astype(I32)


def _route_call(lt, tri, blk, bsz, seq, cap):
    e = N_EXPERTS
    big = lambda dt: jax.ShapeDtypeStruct((bsz, e, seq), dt)
    small = jax.ShapeDtypeStruct((bsz, e, LANES), I32)
    bspec = pl.BlockSpec((1, e, seq), lambda b: (b, 0, 0))
    sspec = pl.BlockSpec((1, e, LANES), lambda b: (b, 0, 0))
    return pl.pallas_call(
        functools.partial(_route_kernel, cap=cap),
        grid=(bsz,),
        in_specs=[pl.BlockSpec((e, seq), lambda b: (0, b)),
                  pl.BlockSpec(tri.shape, lambda b: (0, 0)),
                  pl.BlockSpec(blk.shape, lambda b: (0, 0))],
        out_specs=[bspec, bspec, sspec],
        out_shape=[big(I32), big(F32), small],
        compiler_params=pltpu.CompilerParams(dimension_semantics=("parallel",),
                                             vmem_limit_bytes=VMEM_LIMIT),
    )(lt, tri, blk)


def _moe_kernel(meta_ref, hn_ref, pos_ref, gate_ref, wg_ref, wu_ref, wd_ref, o_ref,
                acc_ref, xg_ref, y_ref, *, cap, n_tb):
    b = pl.program_id(0)
    k = pl.program_id(1)
    out_rows = o_ref.shape[1]

    @pl.when(k == 0)
    def _():
        acc_ref[...] = jnp.zeros_like(acc_ref)
        y_ref[cap:, :] = jnp.zeros((TOK_BLK, y_ref.shape[1]), BF16)

    @pl.when(k < N_EXPERTS)
    def _():
        xg_ref[...] = jnp.zeros_like(xg_ref)
        slot = lax.broadcasted_iota(I32, (TOK_BLK, TOK_BLK), 0)
        mbase = (b * N_EXPERTS + k) * META_W
        jstar = meta_ref[mbase + n_tb]
        ifix = meta_ref[mbase + n_tb + 1]

        def gather(j, i):
            lp = pos_ref[0, 0, j]
            hb = hn_ref[0, pl.ds(pl.multiple_of(j * TOK_BLK, TOK_BLK), TOK_BLK), :]
            oh = jnp.where(slot + i * TOK_BLK == lp, 1.0, 0.0).astype(BF16)
            r0 = pl.multiple_of(i * TOK_BLK, TOK_BLK)
            xg_ref[pl.ds(r0, TOK_BLK), :] += jnp.dot(oh, hb, preferred_element_type=F32).astype(BF16)

        def gather_body(j, carry):
            gather(j, meta_ref[mbase + j])
            return carry

        lax.fori_loop(0, n_tb, gather_body, 0, unroll=MOE_UNROLL)
        gather(jstar, ifix)

        x = xg_ref[:cap, :]
        y = jnp.zeros((cap, D_MODEL), F32)
        d_ff = wg_ref.shape[2]
        for fc in range(d_ff // MXU_N):
            cs = slice(fc * MXU_N, (fc + 1) * MXU_N)
            g = jnp.dot(x, wg_ref[0, :, cs], preferred_element_type=F32)
            u = jnp.dot(x, wu_ref[0, :, cs], preferred_element_type=F32)
            hid = (g * jax.nn.sigmoid(g) * u).astype(BF16)
            y = y + jnp.dot(hid, wd_ref[0, cs, :], preferred_element_type=F32)
        y_ref[:cap, :] = y.astype(BF16)

        def scatter(j, i):
            lp = pos_ref[0, 0, j]
            gr = gate_ref[0, 0, j]
            t0 = pl.multiple_of(j * TOK_BLK, TOK_BLK)
            oh = jnp.where(slot + i * TOK_BLK == lp, gr, 0.0).astype(BF16)
            r0 = pl.multiple_of(i * TOK_BLK, TOK_BLK)
            acc_ref[pl.ds(t0, TOK_BLK), :] += lax.dot_general(
                oh, y_ref[pl.ds(r0, TOK_BLK), :], (((0,), (0,)), ((), ())),
                preferred_element_type=F32)

        def scatter_body(j, carry):
            scatter(j, meta_ref[mbase + j])
            return carry

        lax.fori_loop(0, n_tb, scatter_body, 0, unroll=MOE_UNROLL)
        scatter(jstar, ifix)

    @pl.when(k >= N_EXPERTS)
    def _():
        r0 = pl.multiple_of((k - N_EXPERTS) * out_rows, out_rows)
        o_ref[0] = acc_ref[pl.ds(r0, out_rows), :]


def _moe_call(meta, hn, pos, gate, wg, wu, wd, cap, layer):
    bsz, seq, d = hn.shape
    n_tb = seq // TOK_BLK
    e_idx = lambda k: jnp.minimum(k, N_EXPERTS - 1)
    tok_spec = pl.BlockSpec((1, 1, n_tb, 1, TOK_BLK), lambda b, k, *_: (b, e_idx(k), 0, 0, 0))
    w_spec = lambda w: pl.BlockSpec((1,) + w.shape[1:],
                                    lambda b, k, *_: (layer * N_EXPERTS + e_idx(k), 0, 0))
    out_rows = seq // N_OUT_CHUNKS
    grid_spec = pltpu.PrefetchScalarGridSpec(
        num_scalar_prefetch=1,
        grid=(bsz, N_EXPERTS + N_OUT_CHUNKS),
        in_specs=[pl.BlockSpec((1, seq, d), lambda b, k, *_: (b, 0, 0), pipeline_mode=pl.Buffered(1)),
                  tok_spec, tok_spec, w_spec(wg), w_spec(wu), w_spec(wd)],
        out_specs=pl.BlockSpec((1, out_rows, d),
                               lambda b, k, *_: (b, jnp.maximum(k - N_EXPERTS, 0), 0)),
        scratch_shapes=[pltpu.VMEM((seq, d), F32), pltpu.VMEM((cap + TOK_BLK, d), BF16),
                        pltpu.VMEM((cap + TOK_BLK, d), BF16)],
    )
    return pl.pallas_call(
        functools.partial(_moe_kernel, cap=cap, n_tb=n_tb),
        grid_spec=grid_spec,
        out_shape=jax.ShapeDtypeStruct((bsz, seq, d), F32),
        compiler_params=pltpu.CompilerParams(dimension_semantics=("parallel", "arbitrary"),
                                             vmem_limit_bytes=VMEM_LIMIT_MOE),
    )(meta, hn, pos, gate, wg, wu, wd)


def _ple_kernel(h_ref, moe_ref, p_ref, g_ref, wg_ref, wp_ref, o_ref):
    h2 = h_ref[...] + moe_ref[...]
    a = _rms(h2, g_ref[...]).astype(BF16)
    gate = jax.nn.sigmoid(jnp.dot(a, wg_ref[...], preferred_element_type=F32))
    proj = jnp.dot(p_ref[...].astype(BF16), wp_ref[...], preferred_element_type=F32)
    o_ref[...] = h2 + gate * proj


def _ple_call(h1, moe, p2d, g, wg, wp, layer):
    t_tot = h1.shape[0]
    row = lambda i: (i, 0)
    p_row = lambda i: (layer * (t_tot // TM) + i, 0)
    fixed = lambda i: (0, 0)
    return pl.pallas_call(
        _ple_kernel,
        grid=(t_tot // TM,),
        in_specs=[pl.BlockSpec((TM, D_MODEL), row), pl.BlockSpec((TM, D_MODEL), row),
                  pl.BlockSpec((TM, p2d.shape[1]), p_row), pl.BlockSpec((1, D_MODEL), fixed),
                  pl.BlockSpec(wg.shape, fixed), pl.BlockSpec(wp.shape, fixed)],
        out_specs=pl.BlockSpec((TM, D_MODEL), row),
        out_shape=jax.ShapeDtypeStruct((t_tot, D_MODEL), F32),
        compiler_params=pltpu.CompilerParams(dimension_semantics=("parallel",),
                                             vmem_limit_bytes=VMEM_LIMIT),
    )(h1, moe, p2d, g, wg, wp)


def _rope_tables(seq):
    pos = jnp.arange(seq)

    def angles(pos_f, n, theta):
        inv = theta ** (-jnp.arange(0, n, 2, dtype=F32) / n)
        return pos_f[:, None] * inv[None, :]

    ar = angles((pos // GRID_W).astype(F32), HEAD_DIM // 2, AXIAL_THETA)
    ac = angles((pos % GRID_W).astype(F32), HEAD_DIM // 2, AXIAL_THETA)
    a1 = angles(pos.astype(F32), ROPE_DIMS, ROPE_THETA)
    z16 = jnp.zeros_like(ar)
    rep = LANES // HEAD_DIM
    cos_a = jnp.tile(jnp.concatenate([jnp.cos(ar), jnp.cos(ar), jnp.cos(ac), jnp.cos(ac)], -1), (1, rep))
    sm_a = jnp.tile(jnp.concatenate([-jnp.sin(ar), z16, -jnp.sin(ac), z16], -1), (1, rep))
    sp_a = jnp.tile(jnp.concatenate([z16, jnp.sin(ar), z16, jnp.sin(ac)], -1), (1, rep))
    z8 = jnp.zeros_like(a1)
    rest = HEAD_DIM - ROPE_DIMS
    ones = jnp.ones((seq, rest), F32)
    zr = jnp.zeros((seq, rest), F32)
    cos_c = jnp.tile(jnp.concatenate([jnp.cos(a1), jnp.cos(a1), ones], -1), (1, rep))
    sm_c = jnp.tile(jnp.concatenate([-jnp.sin(a1), z8, zr], -1), (1, rep))
    sp_c = jnp.tile(jnp.concatenate([z8, jnp.sin(a1), zr], -1), (1, rep))
    return cos_a, sm_a, sp_a, cos_c, sm_c, sp_c


_A_HEAD_ORDER = (0, 2, 1, 3)
_A_PERM = np.concatenate([np.arange(h * HEAD_DIM, (h + 1) * HEAD_DIM) for h in _A_HEAD_ORDER])


def _gain_row(qg):
    scale = HEAD_DIM ** -0.5 * LOG2_E
    one = lambda n: jnp.ones((n,), F32)
    return jnp.concatenate([
        jnp.tile(qg[0, 0] * scale, A_Q_HEADS), jnp.tile(qg[0, 1], A_KV_HEADS), one(A_KV_W),
        jnp.tile(qg[1, 0] * scale, B_HEADS), jnp.tile(qg[1, 1], B_HEADS), one(B_W),
        jnp.tile(qg[2, 0] * scale, C_HEADS), jnp.tile(qg[2, 1], C_HEADS), one(C_W)])[None, :]


def kernel(x, p, g_mix, w_in, qk_gain, na_bias, g_out, w_out, g_ffn, w_router,
           w_gate, w_up, w_down, g_ple, w_ple_gate, w_ple_proj):
    bsz, seq, d = x.shape
    depth = w_in.shape[0]
    t_tot = bsz * seq
    cap = max(1, EC_CAPACITY * seq // N_EXPERTS)
    n_tb = seq // TOK_BLK

    tabs = _rope_tables(seq)
    lane = np.arange(LANES)
    col = np.arange(MXU_N)
    gsum = jnp.asarray((col[:, None] // HEAD_DIM) == (col[None, :] // HEAD_DIM), BF16)
    tri = jnp.asarray(lane[:, None] < lane[None, :], BF16)
    tok = np.arange(seq)[:, None]
    j = np.arange(LANES)[None, :]
    blk_start = (tok < j * TOK_BLK) & (j < n_tb)
    blk_count = (tok // TOK_BLK == j) & (j < n_tb)
    blk = jnp.asarray(np.concatenate([blk_start, blk_count], axis=1), BF16)

    bias, bias_block_of = _na_bias_blocks(na_bias, seq // GRID_W)
    stack = lambda w: w.reshape((depth * N_EXPERTS,) + w.shape[2:])
    wg_all, wu_all, wd_all = stack(w_gate), stack(w_up), stack(w_down)
    p_all = p.reshape(depth * t_tot, p.shape[-1])
    h = x.reshape(t_tot, d)
    for i in range(depth):
        w_i = jnp.concatenate([w_in[i][:, _A_PERM], w_in[i][:, A_Q_W:]], axis=1).astype(BF16)
        ab, c1, c4, c16 = _proj_call(h, g_mix[i][None], w_i, _gain_row(qk_gain[i]), gsum, tabs, seq)
        ab = ab.reshape(bsz, seq, AB_COLS)
        oa = _attn_a_call(ab).reshape(t_tot, A_Q_W)
        ob = _attn_b_call(ab, bias[i], bias_block_of).reshape(t_tot, B_W)
        oc = _attn_c_call(c1, c4, c16, bsz, seq).reshape(t_tot, C_W)
        go = g_out[i]
        w_o = jnp.concatenate([w_out[i][_A_PERM], w_out[i][A_Q_W:]], axis=0).astype(BF16)
        wr = jnp.pad(w_router[i], ((0, 0), (0, LANES - N_EXPERTS)))
        wr_hi = wr.astype(BF16)
        wr_lo = (wr - wr_hi.astype(F32)).astype(BF16)
        h1, hn, lt = _out_call(oa, ob, oc, h, go[:A_Q_W][_A_PERM][None], go[None, A_Q_W:A_Q_W + B_W],
                               go[None, A_Q_W + B_W:], w_o, g_ffn[i][None],
                               jnp.concatenate([wr_hi, wr_lo], axis=1))
        pos, gate, meta = _route_call(lt, tri, blk, bsz, seq, cap)
        moe = _moe_call(meta[:, :, :META_W].reshape(-1),
                        hn.reshape(bsz, seq, d),
                        pos.reshape(bsz, N_EXPERTS, n_tb, 1, TOK_BLK),
                        gate.reshape(bsz, N_EXPERTS, n_tb, 1, TOK_BLK),
                        wg_all, wu_all, wd_all, cap, i)
        h = _ple_call(h1, moe.reshape(t_tot, d), p_all, g_ple[i][None],
                      w_ple_gate[i].astype(BF16), w_ple_proj[i].astype(BF16), i)
    return h.reshape(bsz, seq, d)
```

```python
import functools

import numpy as np
import jax
import jax.numpy as jnp
from jax import lax
from jax.experimental import pallas as pl
from jax.experimental.pallas import tpu as pltpu

F32 = jnp.float32
BF16 = jnp.bfloat16
I32 = jnp.int32

D_MODEL = 1024
HEAD_DIM = 64
A_Q_HEADS = 4
A_KV_HEADS = 2
B_HEADS = 6
C_HEADS = 6
A_Q_W = A_Q_HEADS * HEAD_DIM
A_KV_W = A_KV_HEADS * HEAD_DIM
B_W = B_HEADS * HEAD_DIM
C_W = C_HEADS * HEAD_DIM
IN_COLS = A_Q_W + 2 * A_KV_W + 3 * B_W + 3 * C_W
GRID_W = 64
AXIAL_THETA = 10000.0
NA_ROWS = 8
NA_COLS = 16
C_BRANCHES = ((128, 1), (512, 4), (2048, 16))
ROPE_THETA = 500000.0
ROPE_DIMS = HEAD_DIM // 4
N_EXPERTS = 16
EC_CAPACITY = 2
EPS = 1e-6
NEG_INF = -1e30
LOG2_E = 1.4426950408889634

LANES = 128
MXU_N = 256
VMEM_LIMIT = 56 * 1024 * 1024
VMEM_LIMIT_MOE = 60 * 1024 * 1024

N_CB = IN_COLS // LANES
CB_QA, CB_KA, CB_VA = 0, 2, 3
CB_QB, CB_KB, CB_VB = 4, 7, 10
CB_QC, CB_KC, CB_VC = 13, 16, 19
CB_KIND = ("A", "A", "A", "V") + ("N",) * 6 + ("V",) * 3 + ("C",) * 6 + ("V",) * 3
AB_COLS = CB_QC * LANES
C_COLS = 3 * C_W
N_PAIRS_C = C_W // LANES
DILS = tuple(d for _, d in C_BRANCHES)

TM = 512
TM_CHAIN = 256
TQ_A = 1024
TQ_A_CHAIN = 128
NA_QROWS = 4
NA_KROWS = 12
NA_UNROLL = 14
TQ_C = 128
C_UNROLL = 15
TOK_BLK = 256
MOE_UNROLL = 16
N_OUT_CHUNKS = 8
META_W = 32


def _rms(x, g):
    return x * lax.rsqrt(jnp.mean(x * x, axis=-1, keepdims=True) + EPS) * g


def _proj_kernel(h_ref, g_ref, w_ref, gain_ref, gsum_ref, ca_ref, sma_ref, spa_ref,
                 cc_ref, smc_ref, spc_ref, ab_ref, c1_ref, c4_ref, c16_ref, cs_ref):
    gsum = gsum_ref[...]
    per = MXU_N // LANES
    n_c = IN_COLS // MXU_N
    normed = [c for c in range(n_c) if any(CB_KIND[cb] != "V" for cb in range(c * per, (c + 1) * per))]
    for r0 in range(0, TM, TM_CHAIN):
        rows = slice(r0, r0 + TM_CHAIN)
        a = _rms(h_ref[rows, :], g_ref[...]).astype(BF16)
        accs = [jnp.dot(a, w_ref[:, c * MXU_N:(c + 1) * MXU_N], preferred_element_type=F32)
                for c in range(n_c)]
        sq = jnp.concatenate([accs[c] * accs[c] for c in normed], axis=0)
        hi = sq.astype(BF16)
        lo = (sq - hi.astype(F32)).astype(BF16)
        ss = jnp.dot(hi, gsum, preferred_element_type=F32) + jnp.dot(lo, gsum, preferred_element_type=F32)
        inv_all = lax.rsqrt(ss * (1.0 / HEAD_DIM) + EPS)
        for c in range(n_c):
            acc = accs[c]
            if c in normed:
                inv = inv_all[normed.index(c) * TM_CHAIN:(normed.index(c) + 1) * TM_CHAIN]
            for hf, cb in enumerate(range(c * per, (c + 1) * per)):
                t = acc[:, hf * LANES:(hf + 1) * LANES]
                kind = CB_KIND[cb]
                if kind != "V":
                    t = t * inv[:, hf * LANES:(hf + 1) * LANES] * gain_ref[:, cb * LANES:(cb + 1) * LANES]
                    if kind == "A":
                        sh = HEAD_DIM // 4
                        t = (t * ca_ref[rows, :] + pltpu.roll(t, LANES - sh, 1) * sma_ref[rows, :]
                             + pltpu.roll(t, sh, 1) * spa_ref[rows, :])
                    elif kind == "C":
                        sh = ROPE_DIMS // 2
                        t = (t * cc_ref[rows, :] + pltpu.roll(t, LANES - sh, 1) * smc_ref[rows, :]
                             + pltpu.roll(t, sh, 1) * spc_ref[rows, :])
                if cb < CB_QC:
                    ab_ref[rows, cb * LANES:(cb + 1) * LANES] = t.astype(BF16)
                else:
                    cs_ref[cb - CB_QC, rows, :] = t
        for p in range(N_PAIRS_C):
            for which in range(3):
                src = which * N_PAIRS_C + p
                for dil, ref in zip(DILS, (c1_ref, c4_ref, c16_ref)):
                    n = TM_CHAIN // dil
                    for r in range(dil):
                        dst = ((p * dil + r) * 3 + which) * LANES
                        ref[r0 // dil:r0 // dil + n, dst:dst + LANES] = (
                            cs_ref[src, pl.ds(r0 + r, n, stride=dil), :].astype(BF16))


def _layer_spec(a, layer):
    return pl.BlockSpec((None,) + a.shape[1:], lambda *_: (layer, 0, 0))


def _proj_call(h2d, g, w, gain, gsum, tabs, seq, layer):
    t_tot = h2d.shape[0]
    nseq = seq // TM
    row = lambda i: (i, 0)
    fixed = lambda i: (0, 0)
    tab = lambda i: (i % nseq, 0)
    return pl.pallas_call(
        _proj_kernel,
        grid=(t_tot // TM,),
        in_specs=[pl.BlockSpec((TM, D_MODEL), row),
                  _layer_spec(g, layer), _layer_spec(w, layer), _layer_spec(gain, layer),
                  pl.BlockSpec((MXU_N, MXU_N), fixed)]
                 + [pl.BlockSpec((TM, LANES), tab)] * 6,
        out_specs=[pl.BlockSpec((TM, AB_COLS), row)]
                  + [pl.BlockSpec((TM // dil, dil * C_COLS), row) for dil in DILS],
        out_shape=[jax.ShapeDtypeStruct((t_tot, AB_COLS), BF16)]
                  + [jax.ShapeDtypeStruct((t_tot // dil, dil * C_COLS), BF16) for dil in DILS],
        scratch_shapes=[pltpu.VMEM((C_COLS // LANES, TM, LANES), F32)],
        compiler_params=pltpu.CompilerParams(dimension_semantics=("parallel",),
                                             vmem_limit_bytes=VMEM_LIMIT),
    )(h2d, g, w, gain, gsum, *tabs)


def _stack_heads(q):
    qf = q.astype(F32)
    lo = lax.broadcasted_iota(I32, qf.shape, 1) < HEAD_DIM
    return jnp.concatenate([jnp.where(lo, qf, 0.0), jnp.where(lo, 0.0, qf)], axis=0).astype(BF16)


def _merge_heads(o):
    n = o.shape[0] // 2
    lo = lax.broadcasted_iota(I32, (n, LANES), 1) < HEAD_DIM
    return jnp.where(lo, o[:n], o[n:])


def _scores(qs, k):
    return lax.dot_general(qs, k, (((1,), (1,)), ((), ())), preferred_element_type=F32)


def _attn_a_kernel(q_ref, k_ref, v_ref, o_ref):
    k = k_ref[0]
    v = v_ref[0]
    for rc in range(TQ_A // TQ_A_CHAIN):
        rows = slice(rc * TQ_A_CHAIN, (rc + 1) * TQ_A_CHAIN)
        for blk in range(A_Q_W // LANES):
            qs = _stack_heads(q_ref[0, rows, blk * LANES:(blk + 1) * LANES])
            s = _scores(qs, k)
            m = jnp.max(s, axis=-1, keepdims=True)
            p = jnp.exp2(s - m)
            l = jnp.sum(p, axis=-1, keepdims=True)
            o = jnp.dot(p.astype(BF16), v, preferred_element_type=F32) / l
            o_ref[0, rows, blk * LANES:(blk + 1) * LANES] = _merge_heads(o)


def _attn_a_call(qkv):
    bsz, seq, _ = qkv.shape
    return pl.pallas_call(
        _attn_a_kernel,
        grid=(bsz, seq // TQ_A),
        in_specs=[pl.BlockSpec((1, TQ_A, A_Q_W), lambda b, i: (b, i, 0)),
                  pl.BlockSpec((1, seq, LANES), lambda b, i: (b, 0, CB_KA)),
                  pl.BlockSpec((1, seq, LANES), lambda b, i: (b, 0, CB_VA))],
        out_specs=pl.BlockSpec((1, TQ_A, A_Q_W), lambda b, i: (b, i, 0)),
        out_shape=jax.ShapeDtypeStruct((bsz, seq, A_Q_W), F32),
        compiler_params=pltpu.CompilerParams(dimension_semantics=("parallel", "parallel"),
                                             vmem_limit_bytes=VMEM_LIMIT),
    )(qkv, qkv, qkv)


def _attn_b_kernel(q_ref, k_ref, v_ref, bias_ref, o_ref, *, n_rows, block_of):
    tq = NA_QROWS * GRID_W
    n_g = n_rows // NA_QROWS

    def bias_tile(cfg):
        return jnp.concatenate(
            [jnp.concatenate([bias_ref[0, t, block_of[cfg][qr][kb]] for kb in range(NA_KROWS // 2)], axis=1)
             for t in range(2) for qr in range(NA_QROWS)], axis=0)

    def group(g, cfg):
        q0 = pl.multiple_of(g * tq, tq)
        krow0 = jnp.clip(g * NA_QROWS - NA_ROWS // 2, 0, n_rows - NA_KROWS)
        t0 = pl.multiple_of(krow0 * GRID_W, GRID_W)
        kw = k_ref[0, pl.ds(t0, NA_KROWS * GRID_W), :]
        vw = v_ref[0, pl.ds(t0, NA_KROWS * GRID_W), :]
        s = _scores(_stack_heads(q_ref[0, pl.ds(q0, tq), :]), kw) + bias_tile(cfg)
        m = jnp.max(s, axis=-1, keepdims=True)
        p = jnp.exp2(s - m)
        l = jnp.sum(p, axis=-1, keepdims=True)
        o = jnp.dot(p.astype(BF16), vw, preferred_element_type=F32) / l
        o_ref[0, pl.ds(q0, tq), :] = _merge_heads(o)

    def interior(g, carry):
        group(g, 1)
        return carry

    group(0, 0)
    lax.fori_loop(1, n_g - 1, interior, 0, unroll=NA_UNROLL)
    group(n_g - 1, 2)


def _attn_b_call(qkv, bias, block_of, layer):
    bsz, seq, _ = qkv.shape
    spec = lambda cb: pl.BlockSpec((1, seq, LANES), lambda p, b: (b, 0, cb + p))
    return pl.pallas_call(
        functools.partial(_attn_b_kernel, n_rows=seq // GRID_W, block_of=block_of),
        grid=(B_W // LANES, bsz),
        in_specs=[spec(CB_QB), spec(CB_KB), spec(CB_VB),
                  pl.BlockSpec((None, 1) + bias.shape[2:], lambda p, b: (layer, p, 0, 0, 0, 0))],
        out_specs=pl.BlockSpec((1, seq, LANES), lambda p, b: (b, 0, p)),
        out_shape=jax.ShapeDtypeStruct((bsz, seq, B_W), F32),
        compiler_params=pltpu.CompilerParams(dimension_semantics=("parallel",) * 2,
                                             vmem_limit_bytes=VMEM_LIMIT),
    )(qkv, qkv, qkv, bias)


def _na_bias_blocks(rpb, n_rows):
    wr = NA_ROWS
    n_co = 2 * NA_COLS - 1
    col_sel = np.zeros((GRID_W, GRID_W, n_co), np.float32)
    for c in range(GRID_W):
        cstart = int(np.clip(c - NA_COLS // 2, 0, GRID_W - NA_COLS))
        for kc in range(cstart, cstart + NA_COLS):
            col_sel[c, kc, kc - c + NA_COLS - 1] = 1.0
    col_out = np.where(col_sel.sum(-1) > 0, 0.0, NEG_INF).astype(np.float32)
    per_off = jnp.einsum("lhab,cjb->lhacj", rpb.astype(F32) * LOG2_E, col_sel,
                         precision=lax.Precision.HIGHEST) + col_out
    masked = jnp.full(per_off.shape[:2] + (GRID_W, GRID_W), NEG_INF, F32)
    keys, block_of = [], []
    for r0 in (0, NA_QROWS, n_rows - NA_QROWS):
        start = int(np.clip(r0 - wr // 2, 0, n_rows - NA_KROWS))
        geometry = []
        for qi in range(NA_QROWS):
            r = r0 + qi
            rs = int(np.clip(r - wr // 2, 0, n_rows - wr))
            offs = [start + kj - r + NA_ROWS - 1 if rs <= start + kj < rs + wr else None
                    for kj in range(NA_KROWS)]
            row = []
            for kb in range(NA_KROWS // 2):
                key = (offs[2 * kb], offs[2 * kb + 1])
                if key not in keys:
                    keys.append(key)
                row.append(keys.index(key))
            geometry.append(tuple(row))
        block_of.append(tuple(geometry))
    half = lambda a: masked if a is None else per_off[:, :, a]
    blocks = jnp.stack([jnp.concatenate([half(a0), half(a1)], axis=-1) for a0, a1 in keys], axis=2)
    n_l = rpb.shape[0]
    blocks = blocks.reshape(n_l, B_HEADS // 2, 2, len(keys), GRID_W, 2 * GRID_W)
    return blocks, tuple(block_of)


def _band_tile(q, kw, vw, band):
    s = _scores(_stack_heads(q), kw) + band
    m = jnp.max(s, axis=-1, keepdims=True)
    p = jnp.exp2(s - m)
    l = jnp.sum(p, axis=-1, keepdims=True)
    o = jnp.dot(p.astype(BF16), vw, preferred_element_type=F32) / l
    lse = jnp.broadcast_to(m + jnp.log2(l), (2 * TQ_C, LANES))
    return _merge_heads(o), _merge_heads(lse)


def _attn_c_kernel(c1_ref, c4_ref, c16_ref, band_ref, o_ref, o2_ref, l2_ref, o3_ref, l3_ref, *, seq):
    def branch(ref, window, dil, r, emit):
        length = seq // dil
        half = window // 2 // dil
        n_q = length // TQ_C
        win = min(TQ_C + 2 * half, length)
        base = r * 3 * LANES

        def tile(qi, band):
            if isinstance(qi, int):
                q0 = qi * TQ_C
                ks = min(max(q0 - half, 0), length - win)
            else:
                q0 = pl.multiple_of(qi * TQ_C, TQ_C)
                ks = pl.multiple_of(jnp.clip(q0 - half, 0, length - win), half)
            o, lse = _band_tile(ref[0, pl.ds(q0, TQ_C), base:base + LANES],
                                ref[0, pl.ds(ks, win), base + LANES:base + 2 * LANES],
                                ref[0, pl.ds(ks, win), base + 2 * LANES:base + 3 * LANES], band)
            emit(q0, o, lse)

        def interior(qi, carry):
            tile(qi, band_ref[1])
            return carry

        tile(0, band_ref[0])
        if n_q > 2:
            lax.fori_loop(1, n_q - 1, interior, 0, unroll=min(C_UNROLL, n_q - 2))
        tile(n_q - 1, band_ref[2])

    for ref, (window, dil), o_scr, l_scr in ((c4_ref, C_BRANCHES[1], o2_ref, l2_ref),
                                            (c16_ref, C_BRANCHES[2], o3_ref, l3_ref)):
        for r in range(dil):
            def emit(q0, o, lse, r=r, dil=dil, o_scr=o_scr, l_scr=l_scr):
                rows = pl.ds(r + dil * q0, TQ_C, stride=dil)
                o_scr[rows, :] = o
                l_scr[rows, :] = lse
            branch(ref, window, dil, r, emit)

    def emit1(q0, o1, l1):
        rows = pl.ds(q0, TQ_C)
        l2, l3 = l2_ref[rows, :], l3_ref[rows, :]
        lm = jnp.maximum(jnp.maximum(l1, l2), l3)
        e1, e2, e3 = jnp.exp2(l1 - lm), jnp.exp2(l2 - lm), jnp.exp2(l3 - lm)
        o_ref[0, rows, :] = (e1 * o1 + e2 * o2_ref[rows, :] + e3 * o3_ref[rows, :]) / (e1 + e2 + e3)

    branch(c1_ref, C_BRANCHES[0][0], 1, 0, emit1)


def _band_masks():
    half = C_BRANCHES[0][0] // 2
    assert all(w // 2 // d == half for w, d in C_BRANCHES)
    i = np.arange(2 * TQ_C)[:, None] % TQ_C
    j = np.arange(TQ_C + 2 * half)[None, :]
    return jnp.asarray(np.stack([np.where(np.abs(off + i - j) <= half, 0.0, NEG_INF)
                                 for off in (0, half, 2 * half)]).astype(np.float32))


def _attn_c_call(c1, c4, c16, bsz, seq):
    views = [c.reshape(bsz, seq // dil, dil * C_COLS) for c, dil in zip((c1, c4, c16), DILS)]
    spec = lambda dil: pl.BlockSpec((1, seq // dil, dil * 3 * LANES), lambda b, p: (b, 0, p))
    band = _band_masks()
    assert seq // DILS[-1] >= band.shape[2]
    return pl.pallas_call(
        functools.partial(_attn_c_kernel, seq=seq),
        grid=(bsz, N_PAIRS_C),
        in_specs=[spec(dil) for dil in DILS] + [pl.BlockSpec(band.shape, lambda b, p: (0, 0, 0))],
        out_specs=pl.BlockSpec((1, seq, LANES), lambda b, p: (b, 0, p)),
        out_shape=jax.ShapeDtypeStruct((bsz, seq, C_W), F32),
        scratch_shapes=[pltpu.VMEM((seq, LANES), F32)] * 4,
        compiler_params=pltpu.CompilerParams(dimension_semantics=("parallel",) * 2,
                                             vmem_limit_bytes=VMEM_LIMIT),
    )(*views, band)


def _out_kernel(oa_ref, ob_ref, oc_ref, h_ref, ga_ref, gb_ref, gc_ref, w_ref, gf_ref, wr_ref,
                h1_ref, hn_ref, lt_ref):
    mixed = jnp.concatenate([_rms(oa_ref[...], ga_ref[...]),
                             _rms(ob_ref[...], gb_ref[...]),
                             _rms(oc_ref[...], gc_ref[...])], axis=-1).astype(BF16)
    h1 = h_ref[...] + jnp.dot(mixed, w_ref[...], preferred_element_type=F32)
    h1_ref[...] = h1
    hn = _rms(h1, gf_ref[...])
    hi = hn.astype(BF16)
    hn_ref[...] = hi
    lo = (hn - hi.astype(F32)).astype(BF16)
    both = jnp.dot(hi, wr_ref[...], preferred_element_type=F32)
    logits = (both[:, :LANES] + both[:, LANES:]
              + jnp.dot(lo, wr_ref[:, :LANES], preferred_element_type=F32))
    lt_ref[...] = logits.T[:N_EXPERTS, :]


def _out_call(oa, ob, oc, h2d, ga, gb, gc, w, gf, wr, layer):
    t_tot = h2d.shape[0]
    row = lambda i: (i, 0)
    rs = lambda w_: pl.BlockSpec((TM, w_), row)
    fs = lambda a: _layer_spec(a, layer)
    return pl.pallas_call(
        _out_kernel,
        grid=(t_tot // TM,),
        in_specs=[rs(A_Q_W), rs(B_W), rs(C_W), rs(D_MODEL)]
                 + [fs(ga), fs(gb), fs(gc), fs(w), fs(gf), fs(wr)],
        out_specs=[rs(D_MODEL), rs(D_MODEL), pl.BlockSpec((N_EXPERTS, TM), lambda i: (0, i))],
        out_shape=[jax.ShapeDtypeStruct((t_tot, D_MODEL), F32),
                   jax.ShapeDtypeStruct((t_tot, D_MODEL), BF16),
                   jax.ShapeDtypeStruct((N_EXPERTS, t_tot), F32)],
        compiler_params=pltpu.CompilerParams(dimension_semantics=("parallel",),
                                             vmem_limit_bytes=VMEM_LIMIT),
    )(oa, ob, oc, h2d, ga, gb, gc, w, gf, wr)


def _route_kernel(lt_ref, tri_ref, blk_ref, pos_ref, gate_ref, meta_ref, *, cap):
    l = lt_ref[...]
    seq = l.shape[1]
    m = jnp.max(l, axis=0, keepdims=True)
    ex = jnp.exp(l - m)
    aff = ex / jnp.sum(ex, axis=0, keepdims=True)
    gate_ref[0] = aff
    bits = lax.bitcast_convert_type(aff, I32)

    def count(mask):
        return jnp.sum(mask.astype(F32), axis=1, keepdims=True)

    def search(i, t):
        cand = t | (1 << (30 - i))
        return jnp.where(count(bits >= cand) >= cap, cand, t)

    thr = lax.fori_loop(0, 31, search, jnp.zeros((N_EXPERTS, 1), I32))
    gt = bits > thr
    eq = bits == thr
    need = cap - count(gt)

    tri = tri_ref[...]

    def excl_prefix(mask):
        mb = jnp.where(mask, 1.0, 0.0).astype(BF16)
        parts = []
        run = jnp.zeros((N_EXPERTS, 1), F32)
        for c in range(seq // LANES):
            ch = mb[:, c * LANES:(c + 1) * LANES]
            parts.append(jnp.dot(ch, tri, preferred_element_type=F32) + run)
            run = run + jnp.sum(ch.astype(F32), axis=1, keepdims=True)
        return jnp.concatenate(parts, axis=1)

    sel = gt | (eq & (excl_prefix(eq) < need))
    pos = excl_prefix(sel)
    pos_ref[0] = jnp.where(sel, pos, -1.0).astype(I32)
    sb = jnp.where(sel, 1.0, 0.0).astype(BF16)
    sc = jnp.dot(sb, blk_ref[...], preferred_element_type=F32)
    start = sc[:, :LANES]
    cnt = sc[:, LANES:]
    ilo = jnp.minimum(jnp.floor(start * (1.0 / TOK_BLK)), cap // TOK_BLK - 1.0)
    cross = jnp.where(start + cnt > (ilo + 1.0) * TOK_BLK, 1.0, 0.0)
    lane = lax.broadcasted_iota(I32, cross.shape, 1)
    has = jnp.sum(cross, axis=1, keepdims=True)
    jstar = jnp.sum(cross * lane.astype(F32), axis=1, keepdims=True)
    ifix = cap // TOK_BLK - has
    n_tb = seq // TOK_BLK
    meta = jnp.where(lane < n_tb, ilo, jnp.where(lane == n_tb, jstar, jnp.where(lane == n_tb + 1, ifix, 0.0)))
    meta_ref[0] = meta.astype(I32)


def _route_call(lt, tri, blk, bsz, seq, cap):
    e = N_EXPERTS
    big = lambda dt: jax.ShapeDtypeStruct((bsz, e, seq), dt)
    small = jax.ShapeDtypeStruct((bsz, e, LANES), I32)
    bspec = pl.BlockSpec((1, e, seq), lambda b: (b, 0, 0))
    sspec = pl.BlockSpec((1, e, LANES), lambda b: (b, 0, 0))
    return pl.pallas_call(
        functools.partial(_route_kernel, cap=cap),
        grid=(bsz,),
        in_specs=[pl.BlockSpec((e, seq), lambda b: (0, b)),
                  pl.BlockSpec(tri.shape, lambda b: (0, 0)),
                  pl.BlockSpec(blk.shape, lambda b: (0, 0))],
        out_specs=[bspec, bspec, sspec],
        out_shape=[big(I32), big(F32), small],
        compiler_params=pltpu.CompilerParams(dimension_semantics=("parallel",),
                                             vmem_limit_bytes=VMEM_LIMIT),
    )(lt, tri, blk)


def _moe_kernel(meta_ref, hn_ref, pos_ref, gate_ref, wg_ref, wu_ref, wd_ref, o_ref,
                acc_ref, xg_ref, y_ref, *, cap, n_tb):
    b = pl.program_id(0)
    k = pl.program_id(1)
    out_rows = o_ref.shape[1]

    @pl.when(k == 0)
    def _():
        acc_ref[...] = jnp.zeros_like(acc_ref)
        y_ref[cap:, :] = jnp.zeros((TOK_BLK, y_ref.shape[1]), BF16)

    @pl.when(k < N_EXPERTS)
    def _():
        xg_ref[...] = jnp.zeros_like(xg_ref)
        slot = lax.broadcasted_iota(I32, (TOK_BLK, TOK_BLK), 0)
        mbase = (b * N_EXPERTS + k) * META_W
        jstar = meta_ref[mbase + n_tb]
        ifix = meta_ref[mbase + n_tb + 1]

        def gather(j, i):
            lp = pos_ref[0, 0, j]
            hb = hn_ref[0, pl.ds(pl.multiple_of(j * TOK_BLK, TOK_BLK), TOK_BLK), :]
            oh = jnp.where(slot + i * TOK_BLK == lp, 1.0, 0.0).astype(BF16)
            r0 = pl.multiple_of(i * TOK_BLK, TOK_BLK)
            xg_ref[pl.ds(r0, TOK_BLK), :] += jnp.dot(oh, hb, preferred_element_type=F32).astype(BF16)

        def gather_body(j, carry):
            gather(j, meta_ref[mbase + j])
            return carry

        lax.fori_loop(0, n_tb, gather_body, 0, unroll=MOE_UNROLL)
        gather(jstar, ifix)

        x = xg_ref[:cap, :]
        y = jnp.zeros((cap, D_MODEL), F32)
        d_ff = wg_ref.shape[2]
        for fc in range(d_ff // MXU_N):
            cs = slice(fc * MXU_N, (fc + 1) * MXU_N)
            g = jnp.dot(x, wg_ref[0, :, cs], preferred_element_type=F32)
            u = jnp.dot(x, wu_ref[0, :, cs], preferred_element_type=F32)
            hid = (g * jax.nn.sigmoid(g) * u).astype(BF16)
            y = y + jnp.dot(hid, wd_ref[0, cs, :], preferred_element_type=F32)
        y_ref[:cap, :] = y.astype(BF16)

        def scatter(j, i):
            lp = pos_ref[0, 0, j]
            gr = gate_ref[0, 0, j]
            t0 = pl.multiple_of(j * TOK_BLK, TOK_BLK)
            oh = jnp.where(slot + i * TOK_BLK == lp, gr, 0.0).astype(BF16)
            r0 = pl.multiple_of(i * TOK_BLK, TOK_BLK)
            acc_ref[pl.ds(t0, TOK_BLK), :] += lax.dot_general(
                oh, y_ref[pl.ds(r0, TOK_BLK), :], (((0,), (0,)), ((), ())),
                preferred_element_type=F32)

        def scatter_body(j, carry):
            scatter(j, meta_ref[mbase + j])
            return carry

        lax.fori_loop(0, n_tb, scatter_body, 0, unroll=MOE_UNROLL)
        scatter(jstar, ifix)

    @pl.when(k >= N_EXPERTS)
    def _():
        r0 = pl.multiple_of((k - N_EXPERTS) * out_rows, out_rows)
        o_ref[0] = acc_ref[pl.ds(r0, out_rows), :]


def _moe_call(meta, hn, pos, gate, wg, wu, wd, cap, layer):
    bsz, seq, d = hn.shape
    n_tb = seq // TOK_BLK
    e_idx = lambda k: jnp.minimum(k, N_EXPERTS - 1)
    tok_spec = pl.BlockSpec((1, 1, n_tb, 1, TOK_BLK), lambda b, k, *_: (b, e_idx(k), 0, 0, 0))
    w_spec = lambda w: pl.BlockSpec((1,) + w.shape[1:],
                                    lambda b, k, *_: (layer * N_EXPERTS + e_idx(k), 0, 0))
    out_rows = seq // N_OUT_CHUNKS
    grid_spec = pltpu.PrefetchScalarGridSpec(
        num_scalar_prefetch=1,
        grid=(bsz, N_EXPERTS + N_OUT_CHUNKS),
        in_specs=[pl.BlockSpec((1, seq, d), lambda b, k, *_: (b, 0, 0), pipeline_mode=pl.Buffered(1)),
                  tok_spec, tok_spec, w_spec(wg), w_spec(wu), w_spec(wd)],
        out_specs=pl.BlockSpec((1, out_rows, d),
                               lambda b, k, *_: (b, jnp.maximum(k - N_EXPERTS, 0), 0)),
        scratch_shapes=[pltpu.VMEM((seq, d), F32), pltpu.VMEM((cap + TOK_BLK, d), BF16),
                        pltpu.VMEM((cap + TOK_BLK, d), BF16)],
    )
    return pl.pallas_call(
        functools.partial(_moe_kernel, cap=cap, n_tb=n_tb),
        grid_spec=grid_spec,
        out_shape=jax.ShapeDtypeStruct((bsz, seq, d), F32),
        compiler_params=pltpu.CompilerParams(dimension_semantics=("parallel", "arbitrary"),
                                             vmem_limit_bytes=VMEM_LIMIT_MOE),
    )(meta, hn, pos, gate, wg, wu, wd)


def _ple_kernel(h_ref, moe_ref, p_ref, g_ref, wg_ref, wp_ref, o_ref):
    h2 = h_ref[...] + moe_ref[...]
    a = _rms(h2, g_ref[...]).astype(BF16)
    gate = jax.nn.sigmoid(jnp.dot(a, wg_ref[...], preferred_element_type=F32))
    proj = jnp.dot(p_ref[...].astype(BF16), wp_ref[...], preferred_element_type=F32)
    o_ref[...] = h2 + gate * proj


def _ple_call(h1, moe, p2d, g, wg, wp, layer):
    t_tot = h1.shape[0]
    row = lambda i: (i, 0)
    p_row = lambda i: (layer * (t_tot // TM) + i, 0)
    return pl.pallas_call(
        _ple_kernel,
        grid=(t_tot // TM,),
        in_specs=[pl.BlockSpec((TM, D_MODEL), row), pl.BlockSpec((TM, D_MODEL), row),
                  pl.BlockSpec((TM, p2d.shape[1]), p_row), _layer_spec(g, layer),
                  _layer_spec(wg, layer), _layer_spec(wp, layer)],
        out_specs=pl.BlockSpec((TM, D_MODEL), row),
        out_shape=jax.ShapeDtypeStruct((t_tot, D_MODEL), F32),
        compiler_params=pltpu.CompilerParams(dimension_semantics=("parallel",),
                                             vmem_limit_bytes=VMEM_LIMIT),
    )(h1, moe, p2d, g, wg, wp)


def _rope_tables(seq):
    pos = jnp.arange(seq)

    def angles(pos_f, n, theta):
        inv = theta ** (-jnp.arange(0, n, 2, dtype=F32) / n)
        return pos_f[:, None] * inv[None, :]

    ar = angles((pos // GRID_W).astype(F32), HEAD_DIM // 2, AXIAL_THETA)
    ac = angles((pos % GRID_W).astype(F32), HEAD_DIM // 2, AXIAL_THETA)
    a1 = angles(pos.astype(F32), ROPE_DIMS, ROPE_THETA)
    z16 = jnp.zeros_like(ar)
    rep = LANES // HEAD_DIM
    cos_a = jnp.tile(jnp.concatenate([jnp.cos(ar), jnp.cos(ar), jnp.cos(ac), jnp.cos(ac)], -1), (1, rep))
    sm_a = jnp.tile(jnp.concatenate([-jnp.sin(ar), z16, -jnp.sin(ac), z16], -1), (1, rep))
    sp_a = jnp.tile(jnp.concatenate([z16, jnp.sin(ar), z16, jnp.sin(ac)], -1), (1, rep))
    z8 = jnp.zeros_like(a1)
    rest = HEAD_DIM - ROPE_DIMS
    ones = jnp.ones((seq, rest), F32)
    zr = jnp.zeros((seq, rest), F32)
    cos_c = jnp.tile(jnp.concatenate([jnp.cos(a1), jnp.cos(a1), ones], -1), (1, rep))
    sm_c = jnp.tile(jnp.concatenate([-jnp.sin(a1), z8, zr], -1), (1, rep))
    sp_c = jnp.tile(jnp.concatenate([z8, jnp.sin(a1), zr], -1), (1, rep))
    return cos_a, sm_a, sp_a, cos_c, sm_c, sp_c


_A_HEAD_ORDER = (0, 2, 1, 3)
_A_PERM = np.concatenate([np.arange(h * HEAD_DIM, (h + 1) * HEAD_DIM) for h in _A_HEAD_ORDER])


def _gain_rows(qg):
    scale = HEAD_DIM ** -0.5 * LOG2_E
    n_l = qg.shape[0]
    one = lambda n: jnp.ones((n_l, n), F32)
    rep = lambda v, n: jnp.tile(v, (1, n))
    return jnp.concatenate([
        rep(qg[:, 0, 0] * scale, A_Q_HEADS), rep(qg[:, 0, 1], A_KV_HEADS), one(A_KV_W),
        rep(qg[:, 1, 0] * scale, B_HEADS), rep(qg[:, 1, 1], B_HEADS), one(B_W),
        rep(qg[:, 2, 0] * scale, C_HEADS), rep(qg[:, 2, 1], C_HEADS), one(C_W)], axis=1)[:, None, :]


def kernel(x, p, g_mix, w_in, qk_gain, na_bias, g_out, w_out, g_ffn, w_router,
           w_gate, w_up, w_down, g_ple, w_ple_gate, w_ple_proj):
    bsz, seq, d = x.shape
    depth = w_in.shape[0]
    t_tot = bsz * seq
    cap = max(1, EC_CAPACITY * seq // N_EXPERTS)
    n_tb = seq // TOK_BLK

    tabs = _rope_tables(seq)
    lane = np.arange(LANES)
    col = np.arange(MXU_N)
    gsum = jnp.asarray((col[:, None] // HEAD_DIM) == (col[None, :] // HEAD_DIM), BF16)
    tri = jnp.asarray(lane[:, None] < lane[None, :], BF16)
    tok = np.arange(seq)[:, None]
    j = np.arange(LANES)[None, :]
    blk_start = (tok < j * TOK_BLK) & (j < n_tb)
    blk_count = (tok // TOK_BLK == j) & (j < n_tb)
    blk = jnp.asarray(np.concatenate([blk_start, blk_count], axis=1), BF16)

    bias, bias_block_of = _na_bias_blocks(na_bias, seq // GRID_W)
    stack = lambda w: w.reshape((depth * N_EXPERTS,) + w.shape[2:])
    wg_all, wu_all, wd_all = stack(w_gate), stack(w_up), stack(w_down)
    p_all = p.reshape(depth * t_tot, p.shape[-1])
    vec = lambda g: g[:, None, :]
    w_in_all = jnp.concatenate([w_in[:, :, _A_PERM], w_in[:, :, A_Q_W:]], axis=2).astype(BF16)
    gains = _gain_rows(qk_gain)
    w_out_all = jnp.concatenate([w_out[:, _A_PERM], w_out[:, A_Q_W:]], axis=1).astype(BF16)
    ga_all, gb_all, gc_all = (vec(g_out[:, :A_Q_W][:, _A_PERM]), vec(g_out[:, A_Q_W:A_Q_W + B_W]),
                              vec(g_out[:, A_Q_W + B_W:]))
    wr = jnp.pad(w_router, ((0, 0), (0, 0), (0, LANES - N_EXPERTS)))
    wr_hi = wr.astype(BF16)
    wr_all = jnp.concatenate([wr_hi, (wr - wr_hi.astype(F32)).astype(BF16)], axis=2)
    wpg_all, wpp_all = w_ple_gate.astype(BF16), w_ple_proj.astype(BF16)
    h = x.reshape(t_tot, d)
    for i in range(depth):
        ab, c1, c4, c16 = _proj_call(h, vec(g_mix), w_in_all, gains, gsum, tabs, seq, i)
        ab = ab.reshape(bsz, seq, AB_COLS)
        oa = _attn_a_call(ab).reshape(t_tot, A_Q_W)
        ob = _attn_b_call(ab, bias, bias_block_of, i).reshape(t_tot, B_W)
        oc = _attn_c_call(c1, c4, c16, bsz, seq).reshape(t_tot, C_W)
        h1, hn, lt = _out_call(oa, ob, oc, h, ga_all, gb_all, gc_all, w_out_all, vec(g_ffn), wr_all, i)
        pos, gate, meta = _route_call(lt, tri, blk, bsz, seq, cap)
        moe = _moe_call(meta[:, :, :META_W].reshape(-1),
                        hn.reshape(bsz, seq, d),
                        pos.reshape(bsz, N_EXPERTS, n_tb, 1, TOK_BLK),
                        gate.reshape(bsz, N_EXPERTS, n_tb, 1, TOK_BLK),
                        wg_all, wu_all, wd_all, cap, i)
        h = _ple_call(h1, moe.reshape(t_tot, d), p_all, vec(g_ple), wpg_all, wpp_all, i)
    return h.reshape(bsz, seq, d)
```

```python
import functools

import numpy as np
import jax
import jax.numpy as jnp
from jax import lax
from jax.experimental import pallas as pl
from jax.experimental.pallas import tpu as pltpu

F32 = jnp.float32
BF16 = jnp.bfloat16
I32 = jnp.int32

D_MODEL = 1024
HEAD_DIM = 64
A_Q_HEADS = 4
A_KV_HEADS = 2
B_HEADS = 6
C_HEADS = 6
A_Q_W = A_Q_HEADS * HEAD_DIM
A_KV_W = A_KV_HEADS * HEAD_DIM
B_W = B_HEADS * HEAD_DIM
C_W = C_HEADS * HEAD_DIM
IN_COLS = A_Q_W + 2 * A_KV_W + 3 * B_W + 3 * C_W
GRID_W = 64
AXIAL_THETA = 10000.0
NA_ROWS = 8
NA_COLS = 16
C_BRANCHES = ((128, 1), (512, 4), (2048, 16))
ROPE_THETA = 500000.0
ROPE_DIMS = HEAD_DIM // 4
N_EXPERTS = 16
EC_CAPACITY = 2
EPS = 1e-6
NEG_INF = -1e30
LOG2_E = 1.4426950408889634

LANES = 128
MXU_N = 256
VMEM_LIMIT = 56 * 1024 * 1024
VMEM_LIMIT_MOE = 60 * 1024 * 1024

N_CB = IN_COLS // LANES
CB_QA, CB_KA, CB_VA = 0, 2, 3
CB_QB, CB_KB, CB_VB = 4, 7, 10
CB_QC, CB_KC, CB_VC = 13, 16, 19
CB_KIND = ("A", "A", "A", "V") + ("N",) * 6 + ("V",) * 3 + ("C",) * 6 + ("V",) * 3
AB_COLS = CB_QC * LANES
C_COLS = 3 * C_W
N_PAIRS_C = C_W // LANES
DILS = tuple(d for _, d in C_BRANCHES)

TM = 512
TM_CHAIN = 256
TQ_A = 1024
TQ_A_CHAIN = 128
NA_QROWS = 4
NA_KROWS = 12
NA_UNROLL = 14
TQ_C = 128
C_UNROLL = 15
TOK_BLK = 256
MOE_UNROLL = 16
N_OUT_CHUNKS = 8
META_W = 32


def _rms(x, g):
    return x * lax.rsqrt(jnp.mean(x * x, axis=-1, keepdims=True) + EPS) * g


def _proj_kernel(h_ref, g_ref, w_ref, gain_ref, gsum_ref, ca_ref, sma_ref, spa_ref,
                 cc_ref, smc_ref, spc_ref, ab_ref, c1_ref, c4_ref, c16_ref, cs_ref):
    gsum = gsum_ref[...]
    per = MXU_N // LANES
    n_c = IN_COLS // MXU_N
    normed = [c for c in range(n_c) if any(CB_KIND[cb] != "V" for cb in range(c * per, (c + 1) * per))]
    for r0 in range(0, TM, TM_CHAIN):
        rows = slice(r0, r0 + TM_CHAIN)
        a = _rms(h_ref[rows, :], g_ref[...]).astype(BF16)
        accs = [jnp.dot(a, w_ref[:, c * MXU_N:(c + 1) * MXU_N], preferred_element_type=F32)
                for c in range(n_c)]
        sq = jnp.concatenate([accs[c] * accs[c] for c in normed], axis=0)
        hi = sq.astype(BF16)
        lo = (sq - hi.astype(F32)).astype(BF16)
        ss = jnp.dot(hi, gsum, preferred_element_type=F32) + jnp.dot(lo, gsum, preferred_element_type=F32)
        inv_all = lax.rsqrt(ss * (1.0 / HEAD_DIM) + EPS)
        for c in range(n_c):
            acc = accs[c]
            if c in normed:
                inv = inv_all[normed.index(c) * TM_CHAIN:(normed.index(c) + 1) * TM_CHAIN]
            for hf, cb in enumerate(range(c * per, (c + 1) * per)):
                t = acc[:, hf * LANES:(hf + 1) * LANES]
                kind = CB_KIND[cb]
                if kind != "V":
                    t = t * inv[:, hf * LANES:(hf + 1) * LANES] * gain_ref[:, cb * LANES:(cb + 1) * LANES]
                    if kind == "A":
                        sh = HEAD_DIM // 4
                        t = (t * ca_ref[rows, :] + pltpu.roll(t, LANES - sh, 1) * sma_ref[rows, :]
                             + pltpu.roll(t, sh, 1) * spa_ref[rows, :])
                    elif kind == "C":
                        sh = ROPE_DIMS // 2
                        t = (t * cc_ref[rows, :] + pltpu.roll(t, LANES - sh, 1) * smc_ref[rows, :]
                             + pltpu.roll(t, sh, 1) * spc_ref[rows, :])
                if cb < CB_QC:
                    ab_ref[rows, cb * LANES:(cb + 1) * LANES] = t.astype(BF16)
                else:
                    cs_ref[cb - CB_QC, rows, :] = t
        for p in range(N_PAIRS_C):
            for which in range(3):
                src = which * N_PAIRS_C + p
                for dil, ref in zip(DILS, (c1_ref, c4_ref, c16_ref)):
                    n = TM_CHAIN // dil
                    for r in range(dil):
                        dst = ((p * dil + r) * 3 + which) * LANES
                        ref[r0 // dil:r0 // dil + n, dst:dst + LANES] = (
                            cs_ref[src, pl.ds(r0 + r, n, stride=dil), :].astype(BF16))


def _layer_spec(a, layer):
    return pl.BlockSpec((None,) + a.shape[1:], lambda *_: (layer, 0, 0))


def _proj_call(h2d, g, w, gain, gsum, tabs, seq, layer):
    t_tot = h2d.shape[0]
    nseq = seq // TM
    row = lambda i: (i, 0)
    fixed = lambda i: (0, 0)
    tab = lambda i: (i % nseq, 0)
    return pl.pallas_call(
        _proj_kernel,
        grid=(t_tot // TM,),
        in_specs=[pl.BlockSpec((TM, D_MODEL), row),
                  _layer_spec(g, layer), _layer_spec(w, layer), _layer_spec(gain, layer),
                  pl.BlockSpec((MXU_N, MXU_N), fixed)]
                 + [pl.BlockSpec((TM, LANES), tab)] * 6,
        out_specs=[pl.BlockSpec((TM, AB_COLS), row)]
                  + [pl.BlockSpec((TM // dil, dil * C_COLS), row) for dil in DILS],
        out_shape=[jax.ShapeDtypeStruct((t_tot, AB_COLS), BF16)]
                  + [jax.ShapeDtypeStruct((t_tot // dil, dil * C_COLS), BF16) for dil in DILS],
        scratch_shapes=[pltpu.VMEM((C_COLS // LANES, TM, LANES), F32)],
        compiler_params=pltpu.CompilerParams(dimension_semantics=("parallel",),
                                             vmem_limit_bytes=VMEM_LIMIT),
    )(h2d, g, w, gain, gsum, *tabs)


def _stack_heads(q):
    qf = q.astype(F32)
    lo = lax.broadcasted_iota(I32, qf.shape, 1) < HEAD_DIM
    return jnp.concatenate([jnp.where(lo, qf, 0.0), jnp.where(lo, 0.0, qf)], axis=0).astype(BF16)


def _merge_heads(o):
    n = o.shape[0] // 2
    lo = lax.broadcasted_iota(I32, (n, LANES), 1) < HEAD_DIM
    return jnp.where(lo, o[:n], o[n:])


def _scores(qs, k):
    return lax.dot_general(qs, k, (((1,), (1,)), ((), ())), preferred_element_type=F32)


def _attn_a_kernel(q_ref, k_ref, v_ref, o_ref):
    k = k_ref[0]
    v = v_ref[0]
    for rc in range(TQ_A // TQ_A_CHAIN):
        rows = slice(rc * TQ_A_CHAIN, (rc + 1) * TQ_A_CHAIN)
        for blk in range(A_Q_W // LANES):
            qs = _stack_heads(q_ref[0, rows, blk * LANES:(blk + 1) * LANES])
            s = _scores(qs, k)
            m = jnp.max(s, axis=-1, keepdims=True)
            p = jnp.exp2(s - m)
            l = jnp.sum(p, axis=-1, keepdims=True)
            o = jnp.dot(p.astype(BF16), v, preferred_element_type=F32) / l
            o_ref[0, rows, blk * LANES:(blk + 1) * LANES] = _merge_heads(o)


def _attn_a_call(qkv):
    bsz, seq, _ = qkv.shape
    return pl.pallas_call(
        _attn_a_kernel,
        grid=(bsz, seq // TQ_A),
        in_specs=[pl.BlockSpec((1, TQ_A, A_Q_W), lambda b, i: (b, i, 0)),
                  pl.BlockSpec((1, seq, LANES), lambda b, i: (b, 0, CB_KA)),
                  pl.BlockSpec((1, seq, LANES), lambda b, i: (b, 0, CB_VA))],
        out_specs=pl.BlockSpec((1, TQ_A, A_Q_W), lambda b, i: (b, i, 0)),
        out_shape=jax.ShapeDtypeStruct((bsz, seq, A_Q_W), F32),
        compiler_params=pltpu.CompilerParams(dimension_semantics=("parallel", "parallel"),
                                             vmem_limit_bytes=VMEM_LIMIT),
    )(qkv, qkv, qkv)


def _attn_b_kernel(q_ref, k_ref, v_ref, bias_ref, o_ref, *, n_rows, block_of):
    tq = NA_QROWS * GRID_W
    n_g = n_rows // NA_QROWS

    def bias_tile(cfg):
        return jnp.concatenate(
            [jnp.concatenate([bias_ref[0, t, block_of[cfg][qr][kb]] for kb in range(NA_KROWS // 2)], axis=1)
             for t in range(2) for qr in range(NA_QROWS)], axis=0)

    def group(g, cfg):
        q0 = pl.multiple_of(g * tq, tq)
        krow0 = jnp.clip(g * NA_QROWS - NA_ROWS // 2, 0, n_rows - NA_KROWS)
        t0 = pl.multiple_of(krow0 * GRID_W, GRID_W)
        kw = k_ref[0, pl.ds(t0, NA_KROWS * GRID_W), :]
        vw = v_ref[0, pl.ds(t0, NA_KROWS * GRID_W), :]
        s = _scores(_stack_heads(q_ref[0, pl.ds(q0, tq), :]), kw) + bias_tile(cfg)
        m = jnp.max(s, axis=-1, keepdims=True)
        p = jnp.exp2(s - m)
        l = jnp.sum(p, axis=-1, keepdims=True)
        o = jnp.dot(p.astype(BF16), vw, preferred_element_type=F32) / l
        o_ref[0, pl.ds(q0, tq), :] = _merge_heads(o)

    def interior(g, carry):
        group(g, 1)
        return carry

    group(0, 0)
    lax.fori_loop(1, n_g - 1, interior, 0, unroll=NA_UNROLL)
    group(n_g - 1, 2)


def _attn_b_call(qkv, bias, block_of, layer):
    bsz, seq, _ = qkv.shape
    spec = lambda cb: pl.BlockSpec((1, seq, LANES), lambda p, b: (b, 0, cb + p))
    return pl.pallas_call(
        functools.partial(_attn_b_kernel, n_rows=seq // GRID_W, block_of=block_of),
        grid=(B_W // LANES, bsz),
        in_specs=[spec(CB_QB), spec(CB_KB), spec(CB_VB),
                  pl.BlockSpec((None, 1) + bias.shape[2:], lambda p, b: (layer, p, 0, 0, 0, 0))],
        out_specs=pl.BlockSpec((1, seq, LANES), lambda p, b: (b, 0, p)),
        out_shape=jax.ShapeDtypeStruct((bsz, seq, B_W), F32),
        compiler_params=pltpu.CompilerParams(dimension_semantics=("parallel",) * 2,
                                             vmem_limit_bytes=VMEM_LIMIT),
    )(qkv, qkv, qkv, bias)


def _na_bias_blocks(rpb, n_rows):
    wr = NA_ROWS
    n_co = 2 * NA_COLS - 1
    col_sel = np.zeros((GRID_W, GRID_W, n_co), np.float32)
    for c in range(GRID_W):
        cstart = int(np.clip(c - NA_COLS // 2, 0, GRID_W - NA_COLS))
        for kc in range(cstart, cstart + NA_COLS):
            col_sel[c, kc, kc - c + NA_COLS - 1] = 1.0
    col_out = np.where(col_sel.sum(-1) > 0, 0.0, NEG_INF).astype(np.float32)
    per_off = jnp.einsum("lhab,cjb->lhacj", rpb.astype(F32) * LOG2_E, col_sel,
                         precision=lax.Precision.HIGHEST) + col_out
    masked = jnp.full(per_off.shape[:2] + (GRID_W, GRID_W), NEG_INF, F32)
    keys, block_of = [], []
    for r0 in (0, NA_QROWS, n_rows - NA_QROWS):
        start = int(np.clip(r0 - wr // 2, 0, n_rows - NA_KROWS))
        geometry = []
        for qi in range(NA_QROWS):
            r = r0 + qi
            rs = int(np.clip(r - wr // 2, 0, n_rows - wr))
            offs = [start + kj - r + NA_ROWS - 1 if rs <= start + kj < rs + wr else None
                    for kj in range(NA_KROWS)]
            row = []
            for kb in range(NA_KROWS // 2):
                key = (offs[2 * kb], offs[2 * kb + 1])
                if key not in keys:
                    keys.append(key)
                row.append(keys.index(key))
            geometry.append(tuple(row))
        block_of.append(tuple(geometry))
    half = lambda a: masked if a is None else per_off[:, :, a]
    blocks = jnp.stack([jnp.concatenate([half(a0), half(a1)], axis=-1) for a0, a1 in keys], axis=2)
    n_l = rpb.shape[0]
    blocks = blocks.reshape(n_l, B_HEADS // 2, 2, len(keys), GRID_W, 2 * GRID_W)
    return blocks, tuple(block_of)


def _band_tile(q, kw, vw, band):
    s = _scores(_stack_heads(q), kw) + band
    m = jnp.max(s, axis=-1, keepdims=True)
    p = jnp.exp2(s - m)
    l = jnp.sum(p, axis=-1, keepdims=True)
    o = jnp.dot(p.astype(BF16), vw, preferred_element_type=F32) / l
    lse = jnp.broadcast_to(m + jnp.log2(l), (2 * TQ_C, LANES))
    return _merge_heads(o), _merge_heads(lse)


def _attn_c_kernel(c1_ref, c4_ref, c16_ref, band_ref, o_ref, o2_ref, l2_ref, o3_ref, l3_ref, *, seq):
    def branch(ref, window, dil, r, emit):
        length = seq // dil
        half = window // 2 // dil
        n_q = length // TQ_C
        win = min(TQ_C + 2 * half, length)
        base = r * 3 * LANES

        def tile(qi, band):
            if isinstance(qi, int):
                q0 = qi * TQ_C
                ks = min(max(q0 - half, 0), length - win)
            else:
                q0 = pl.multiple_of(qi * TQ_C, TQ_C)
                ks = pl.multiple_of(jnp.clip(q0 - half, 0, length - win), half)
            o, lse = _band_tile(ref[0, pl.ds(q0, TQ_C), base:base + LANES],
                                ref[0, pl.ds(ks, win), base + LANES:base + 2 * LANES],
                                ref[0, pl.ds(ks, win), base + 2 * LANES:base + 3 * LANES], band)
            emit(q0, o, lse)

        def interior(qi, carry):
            tile(qi, band_ref[1])
            return carry

        tile(0, band_ref[0])
        if n_q > 2:
            lax.fori_loop(1, n_q - 1, interior, 0, unroll=min(C_UNROLL, n_q - 2))
        tile(n_q - 1, band_ref[2])

    for ref, (window, dil), o_scr, l_scr in ((c4_ref, C_BRANCHES[1], o2_ref, l2_ref),
                                            (c16_ref, C_BRANCHES[2], o3_ref, l3_ref)):
        for r in range(dil):
            def emit(q0, o, lse, r=r, dil=dil, o_scr=o_scr, l_scr=l_scr):
                rows = pl.ds(r + dil * q0, TQ_C, stride=dil)
                o_scr[rows, :] = o
                l_scr[rows, :] = lse
            branch(ref, window, dil, r, emit)

    def emit1(q0, o1, l1):
        rows = pl.ds(q0, TQ_C)
        l2, l3 = l2_ref[rows, :], l3_ref[rows, :]
        lm = jnp.maximum(jnp.maximum(l1, l2), l3)
        e1, e2, e3 = jnp.exp2(l1 - lm), jnp.exp2(l2 - lm), jnp.exp2(l3 - lm)
        o_ref[0, rows, :] = (e1 * o1 + e2 * o2_ref[rows, :] + e3 * o3_ref[rows, :]) / (e1 + e2 + e3)

    branch(c1_ref, C_BRANCHES[0][0], 1, 0, emit1)


def _band_masks():
    half = C_BRANCHES[0][0] // 2
    assert all(w // 2 // d == half for w, d in C_BRANCHES)
    i = np.arange(2 * TQ_C)[:, None] % TQ_C
    j = np.arange(TQ_C + 2 * half)[None, :]
    return jnp.asarray(np.stack([np.where(np.abs(off + i - j) <= half, 0.0, NEG_INF)
                                 for off in (0, half, 2 * half)]).astype(np.float32))


def _attn_c_call(c1, c4, c16, bsz, seq):
    views = [c.reshape(bsz, seq // dil, dil * C_COLS) for c, dil in zip((c1, c4, c16), DILS)]
    spec = lambda dil: pl.BlockSpec((1, seq // dil, dil * 3 * LANES), lambda b, p: (b, 0, p))
    band = _band_masks()
    assert seq // DILS[-1] >= band.shape[2]
    return pl.pallas_call(
        functools.partial(_attn_c_kernel, seq=seq),
        grid=(bsz, N_PAIRS_C),
        in_specs=[spec(dil) for dil in DILS] + [pl.BlockSpec(band.shape, lambda b, p: (0, 0, 0))],
        out_specs=pl.BlockSpec((1, seq, LANES), lambda b, p: (b, 0, p)),
        out_shape=jax.ShapeDtypeStruct((bsz, seq, C_W), F32),
        scratch_shapes=[pltpu.VMEM((seq, LANES), F32)] * 4,
        compiler_params=pltpu.CompilerParams(dimension_semantics=("parallel",) * 2,
                                             vmem_limit_bytes=VMEM_LIMIT),
    )(*views, band)


def _out_kernel(oa_ref, ob_ref, oc_ref, h_ref, ga_ref, gb_ref, gc_ref, w_ref, gf_ref, wr_ref,
                h1_ref, hn_ref, lt_ref):
    mixed = jnp.concatenate([_rms(oa_ref[...], ga_ref[...]),
                             _rms(ob_ref[...], gb_ref[...]),
                             _rms(oc_ref[...], gc_ref[...])], axis=-1).astype(BF16)
    h1 = h_ref[...] + jnp.dot(mixed, w_ref[...], preferred_element_type=F32)
    h1_ref[...] = h1
    hn = _rms(h1, gf_ref[...])
    hi = hn.astype(BF16)
    hn_ref[...] = hi
    lo = (hn - hi.astype(F32)).astype(BF16)
    both = jnp.dot(hi, wr_ref[...], preferred_element_type=F32)
    logits = (both[:, :LANES] + both[:, LANES:]
              + jnp.dot(lo, wr_ref[:, :LANES], preferred_element_type=F32))
    lt_ref[...] = logits.T[:N_EXPERTS, :]


def _out_call(oa, ob, oc, h2d, ga, gb, gc, w, gf, wr, layer):
    t_tot = h2d.shape[0]
    row = lambda i: (i, 0)
    rs = lambda w_: pl.BlockSpec((TM, w_), row)
    fs = lambda a: _layer_spec(a, layer)
    return pl.pallas_call(
        _out_kernel,
        grid=(t_tot // TM,),
        in_specs=[rs(A_Q_W), rs(B_W), rs(C_W), rs(D_MODEL)]
                 + [fs(ga), fs(gb), fs(gc), fs(w), fs(gf), fs(wr)],
        out_specs=[rs(D_MODEL), rs(D_MODEL), pl.BlockSpec((N_EXPERTS, TM), lambda i: (0, i))],
        out_shape=[jax.ShapeDtypeStruct((t_tot, D_MODEL), F32),
                   jax.ShapeDtypeStruct((t_tot, D_MODEL), BF16),
                   jax.ShapeDtypeStruct((N_EXPERTS, t_tot), F32)],
        compiler_params=pltpu.CompilerParams(dimension_semantics=("parallel",),
                                             vmem_limit_bytes=VMEM_LIMIT),
    )(oa, ob, oc, h2d, ga, gb, gc, w, gf, wr)


def _route_kernel(lt_ref, tri_ref, blk_ref, pos_ref, gate_ref, meta_ref, *, cap):
    l = lt_ref[...]
    seq = l.shape[1]
    m = jnp.max(l, axis=0, keepdims=True)
    ex = jnp.exp(l - m)
    aff = ex / jnp.sum(ex, axis=0, keepdims=True)
    for jb in range(seq // TOK_BLK):
        gate_ref[0, :, jb, 0, :] = aff[:, jb * TOK_BLK:(jb + 1) * TOK_BLK]
    bits = lax.bitcast_convert_type(aff, I32)

    def count(mask):
        return jnp.sum(mask.astype(F32), axis=1, keepdims=True)

    def search(i, t):
        cand = t | (1 << (30 - i))
        return jnp.where(count(bits >= cand) >= cap, cand, t)

    thr = lax.fori_loop(0, 31, search, jnp.zeros((N_EXPERTS, 1), I32))
    gt = bits > thr
    eq = bits == thr
    need = cap - count(gt)

    tri = tri_ref[...]

    def excl_prefix(mask):
        mb = jnp.where(mask, 1.0, 0.0).astype(BF16)
        parts = []
        run = jnp.zeros((N_EXPERTS, 1), F32)
        for c in range(seq // LANES):
            ch = mb[:, c * LANES:(c + 1) * LANES]
            parts.append(jnp.dot(ch, tri, preferred_element_type=F32) + run)
            run = run + jnp.sum(ch.astype(F32), axis=1, keepdims=True)
        return jnp.concatenate(parts, axis=1)

    sel = gt | (eq & (excl_prefix(eq) < need))
    pos = excl_prefix(sel)
    posi = jnp.where(sel, pos, -1.0).astype(I32)
    for jb in range(seq // TOK_BLK):
        pos_ref[0, :, jb, 0, :] = posi[:, jb * TOK_BLK:(jb + 1) * TOK_BLK]
    sb = jnp.where(sel, 1.0, 0.0).astype(BF16)
    sc = jnp.dot(sb, blk_ref[...], preferred_element_type=F32)
    start = sc[:, :LANES]
    cnt = sc[:, LANES:]
    ilo = jnp.minimum(jnp.floor(start * (1.0 / TOK_BLK)), cap // TOK_BLK - 1.0)
    cross = jnp.where(start + cnt > (ilo + 1.0) * TOK_BLK, 1.0, 0.0)
    lane = lax.broadcasted_iota(I32, cross.shape, 1)
    has = jnp.sum(cross, axis=1, keepdims=True)
    jstar = jnp.sum(cross * lane.astype(F32), axis=1, keepdims=True)
    ifix = cap // TOK_BLK - has
    n_tb = seq // TOK_BLK
    meta = jnp.where(lane < n_tb, ilo, jnp.where(lane == n_tb, jstar, jnp.where(lane == n_tb + 1, ifix, 0.0)))
    meta_ref[0] = meta.astype(I32)


def _route_call(lt, tri, blk, bsz, seq, cap):
    e = N_EXPERTS
    big = lambda dt: jax.ShapeDtypeStruct((bsz, e, seq // TOK_BLK, 1, TOK_BLK), dt)
    small = jax.ShapeDtypeStruct((bsz, e, LANES), I32)
    bspec = pl.BlockSpec((1, e, seq // TOK_BLK, 1, TOK_BLK), lambda b: (b, 0, 0, 0, 0))
    sspec = pl.BlockSpec((1, e, LANES), lambda b: (b, 0, 0))
    return pl.pallas_call(
        functools.partial(_route_kernel, cap=cap),
        grid=(bsz,),
        in_specs=[pl.BlockSpec((e, seq), lambda b: (0, b)),
                  pl.BlockSpec(tri.shape, lambda b: (0, 0)),
                  pl.BlockSpec(blk.shape, lambda b: (0, 0))],
        out_specs=[bspec, bspec, sspec],
        out_shape=[big(I32), big(F32), small],
        compiler_params=pltpu.CompilerParams(dimension_semantics=("parallel",),
                                             vmem_limit_bytes=VMEM_LIMIT),
    )(lt, tri, blk)


def _moe_kernel(meta_ref, hn_ref, pos_ref, gate_ref, wg_ref, wu_ref, wd_ref, o_ref,
                acc_ref, xg_ref, y_ref, *, cap, n_tb):
    b = pl.program_id(0)
    k = pl.program_id(1)
    out_rows = o_ref.shape[1]

    @pl.when(k == 0)
    def _():
        acc_ref[...] = jnp.zeros_like(acc_ref)
        y_ref[cap:, :] = jnp.zeros((TOK_BLK, y_ref.shape[1]), BF16)

    @pl.when(k < N_EXPERTS)
    def _():
        xg_ref[...] = jnp.zeros_like(xg_ref)
        slot = lax.broadcasted_iota(I32, (TOK_BLK, TOK_BLK), 0)
        mbase = (b * N_EXPERTS + k) * META_W
        jstar = meta_ref[mbase + n_tb]
        ifix = meta_ref[mbase + n_tb + 1]

        def gather(j, i):
            lp = pos_ref[0, 0, j]
            hb = hn_ref[0, pl.ds(pl.multiple_of(j * TOK_BLK, TOK_BLK), TOK_BLK), :]
            oh = jnp.where(slot + i * TOK_BLK == lp, 1.0, 0.0).astype(BF16)
            r0 = pl.multiple_of(i * TOK_BLK, TOK_BLK)
            xg_ref[pl.ds(r0, TOK_BLK), :] += jnp.dot(oh, hb, preferred_element_type=F32).astype(BF16)

        def gather_body(j, carry):
            gather(j, meta_ref[mbase + j])
            return carry

        lax.fori_loop(0, n_tb, gather_body, 0, unroll=MOE_UNROLL)
        gather(jstar, ifix)

        x = xg_ref[:cap, :]
        y = jnp.zeros((cap, D_MODEL), F32)
        d_ff = wg_ref.shape[2]
        for fc in range(d_ff // MXU_N):
            cs = slice(fc * MXU_N, (fc + 1) * MXU_N)
            g = jnp.dot(x, wg_ref[0, :, cs], preferred_element_type=F32)
            u = jnp.dot(x, wu_ref[0, :, cs], preferred_element_type=F32)
            hid = (g * jax.nn.sigmoid(g) * u).astype(BF16)
            y = y + jnp.dot(hid, wd_ref[0, cs, :], preferred_element_type=F32)
        y_ref[:cap, :] = y.astype(BF16)

        def scatter(j, i):
            lp = pos_ref[0, 0, j]
            gr = gate_ref[0, 0, j]
            t0 = pl.multiple_of(j * TOK_BLK, TOK_BLK)
            oh = jnp.where(slot + i * TOK_BLK == lp, gr, 0.0).astype(BF16)
            r0 = pl.multiple_of(i * TOK_BLK, TOK_BLK)
            acc_ref[pl.ds(t0, TOK_BLK), :] += lax.dot_general(
                oh, y_ref[pl.ds(r0, TOK_BLK), :], (((0,), (0,)), ((), ())),
                preferred_element_type=F32)

        def scatter_body(j, carry):
            scatter(j, meta_ref[mbase + j])
            return carry

        lax.fori_loop(0, n_tb, scatter_body, 0, unroll=MOE_UNROLL)
        scatter(jstar, ifix)

    @pl.when(k >= N_EXPERTS)
    def _():
        r0 = pl.multiple_of((k - N_EXPERTS) * out_rows, out_rows)
        o_ref[0] = acc_ref[pl.ds(r0, out_rows), :]


def _moe_call(meta, hn, pos, gate, wg, wu, wd, cap, layer):
    bsz, seq, d = hn.shape
    n_tb = seq // TOK_BLK
    e_idx = lambda k: jnp.minimum(k, N_EXPERTS - 1)
    tok_spec = pl.BlockSpec((1, 1, n_tb, 1, TOK_BLK), lambda b, k, *_: (b, e_idx(k), 0, 0, 0))
    w_spec = lambda w: pl.BlockSpec((1,) + w.shape[1:],
                                    lambda b, k, *_: (layer * N_EXPERTS + e_idx(k), 0, 0))
    out_rows = seq // N_OUT_CHUNKS
    grid_spec = pltpu.PrefetchScalarGridSpec(
        num_scalar_prefetch=1,
        grid=(bsz, N_EXPERTS + N_OUT_CHUNKS),
        in_specs=[pl.BlockSpec((1, seq, d), lambda b, k, *_: (b, 0, 0), pipeline_mode=pl.Buffered(1)),
                  tok_spec, tok_spec, w_spec(wg), w_spec(wu), w_spec(wd)],
        out_specs=pl.BlockSpec((1, out_rows, d),
                               lambda b, k, *_: (b, jnp.maximum(k - N_EXPERTS, 0), 0)),
        scratch_shapes=[pltpu.VMEM((seq, d), F32), pltpu.VMEM((cap + TOK_BLK, d), BF16),
                        pltpu.VMEM((cap + TOK_BLK, d), BF16)],
    )
    return pl.pallas_call(
        functools.partial(_moe_kernel, cap=cap, n_tb=n_tb),
        grid_spec=grid_spec,
        out_shape=jax.ShapeDtypeStruct((bsz, seq, d), F32),
        compiler_params=pltpu.CompilerParams(dimension_semantics=("parallel", "arbitrary"),
                                             vmem_limit_bytes=VMEM_LIMIT_MOE),
    )(meta, hn, pos, gate, wg, wu, wd)


def _ple_kernel(h_ref, moe_ref, p_ref, g_ref, wg_ref, wp_ref, o_ref):
    h2 = h_ref[...] + moe_ref[...]
    a = _rms(h2, g_ref[...]).astype(BF16)
    gate = jax.nn.sigmoid(jnp.dot(a, wg_ref[...], preferred_element_type=F32))
    proj = jnp.dot(p_ref[...].astype(BF16), wp_ref[...], preferred_element_type=F32)
    o_ref[...] = h2 + gate * proj


def _ple_call(h1, moe, p2d, g, wg, wp, layer):
    t_tot = h1.shape[0]
    row = lambda i: (i, 0)
    p_row = lambda i: (layer * (t_tot // TM) + i, 0)
    return pl.pallas_call(
        _ple_kernel,
        grid=(t_tot // TM,),
        in_specs=[pl.BlockSpec((TM, D_MODEL), row), pl.BlockSpec((TM, D_MODEL), row),
                  pl.BlockSpec((TM, p2d.shape[1]), p_row), _layer_spec(g, layer),
                  _layer_spec(wg, layer), _layer_spec(wp, layer)],
        out_specs=pl.BlockSpec((TM, D_MODEL), row),
        out_shape=jax.ShapeDtypeStruct((t_tot, D_MODEL), F32),
        compiler_params=pltpu.CompilerParams(dimension_semantics=("parallel",),
                                             vmem_limit_bytes=VMEM_LIMIT),
    )(h1, moe, p2d, g, wg, wp)


def _rope_tables(seq):
    pos = jnp.arange(seq)

    def angles(pos_f, n, theta):
        inv = theta ** (-jnp.arange(0, n, 2, dtype=F32) / n)
        return pos_f[:, None] * inv[None, :]

    ar = angles((pos // GRID_W).astype(F32), HEAD_DIM // 2, AXIAL_THETA)
    ac = angles((pos % GRID_W).astype(F32), HEAD_DIM // 2, AXIAL_THETA)
    a1 = angles(pos.astype(F32), ROPE_DIMS, ROPE_THETA)
    z16 = jnp.zeros_like(ar)
    rep = LANES // HEAD_DIM
    cos_a = jnp.tile(jnp.concatenate([jnp.cos(ar), jnp.cos(ar), jnp.cos(ac), jnp.cos(ac)], -1), (1, rep))
    sm_a = jnp.tile(jnp.concatenate([-jnp.sin(ar), z16, -jnp.sin(ac), z16], -1), (1, rep))
    sp_a = jnp.tile(jnp.concatenate([z16, jnp.sin(ar), z16, jnp.sin(ac)], -1), (1, rep))
    z8 = jnp.zeros_like(a1)
    rest = HEAD_DIM - ROPE_DIMS
    ones = jnp.ones((seq, rest), F32)
    zr = jnp.zeros((seq, rest), F32)
    cos_c = jnp.tile(jnp.concatenate([jnp.cos(a1), jnp.cos(a1), ones], -1), (1, rep))
    sm_c = jnp.tile(jnp.concatenate([-jnp.sin(a1), z8, zr], -1), (1, rep))
    sp_c = jnp.tile(jnp.concatenate([z8, jnp.sin(a1), zr], -1), (1, rep))
    return cos_a, sm_a, sp_a, cos_c, sm_c, sp_c


_A_HEAD_ORDER = (0, 2, 1, 3)
_A_PERM = np.concatenate([np.arange(h * HEAD_DIM, (h + 1) * HEAD_DIM) for h in _A_HEAD_ORDER])


def _gain_rows(qg):
    scale = HEAD_DIM ** -0.5 * LOG2_E
    n_l = qg.shape[0]
    one = lambda n: jnp.ones((n_l, n), F32)
    rep = lambda v, n: jnp.tile(v, (1, n))
    return jnp.concatenate([
        rep(qg[:, 0, 0] * scale, A_Q_HEADS), rep(qg[:, 0, 1], A_KV_HEADS), one(A_KV_W),
        rep(qg[:, 1, 0] * scale, B_HEADS), rep(qg[:, 1, 1], B_HEADS), one(B_W),
        rep(qg[:, 2, 0] * scale, C_HEADS), rep(qg[:, 2, 1], C_HEADS), one(C_W)], axis=1)[:, None, :]


def kernel(x, p, g_mix, w_in, qk_gain, na_bias, g_out, w_out, g_ffn, w_router,
           w_gate, w_up, w_down, g_ple, w_ple_gate, w_ple_proj):
    bsz, seq, d = x.shape
    depth = w_in.shape[0]
    t_tot = bsz * seq
    cap = max(1, EC_CAPACITY * seq // N_EXPERTS)
    n_tb = seq // TOK_BLK

    tabs = _rope_tables(seq)
    lane = np.arange(LANES)
    col = np.arange(MXU_N)
    gsum = jnp.asarray((col[:, None] // HEAD_DIM) == (col[None, :] // HEAD_DIM), BF16)
    tri = jnp.asarray(lane[:, None] < lane[None, :], BF16)
    tok = np.arange(seq)[:, None]
    j = np.arange(LANES)[None, :]
    blk_start = (tok < j * TOK_BLK) & (j < n_tb)
    blk_count = (tok // TOK_BLK == j) & (j < n_tb)
    blk = jnp.asarray(np.concatenate([blk_start, blk_count], axis=1), BF16)

    bias, bias_block_of = _na_bias_blocks(na_bias, seq // GRID_W)
    stack = lambda w: w.reshape((depth * N_EXPERTS,) + w.shape[2:])
    wg_all, wu_all, wd_all = stack(w_gate), stack(w_up), stack(w_down)
    p_all = p.reshape(depth * t_tot, p.shape[-1])
    vec = lambda g: g[:, None, :]
    w_in_all = jnp.concatenate([w_in[:, :, _A_PERM], w_in[:, :, A_Q_W:]], axis=2).astype(BF16)
    gains = _gain_rows(qk_gain)
    w_out_all = jnp.concatenate([w_out[:, _A_PERM], w_out[:, A_Q_W:]], axis=1).astype(BF16)
    ga_all, gb_all, gc_all = (vec(g_out[:, :A_Q_W][:, _A_PERM]), vec(g_out[:, A_Q_W:A_Q_W + B_W]),
                              vec(g_out[:, A_Q_W + B_W:]))
    wr = jnp.pad(w_router, ((0, 0), (0, 0), (0, LANES - N_EXPERTS)))
    wr_hi = wr.astype(BF16)
    wr_all = jnp.concatenate([wr_hi, (wr - wr_hi.astype(F32)).astype(BF16)], axis=2)
    wpg_all, wpp_all = w_ple_gate.astype(BF16), w_ple_proj.astype(BF16)
    h = x.reshape(t_tot, d)
    for i in range(depth):
        ab, c1, c4, c16 = _proj_call(h, vec(g_mix), w_in_all, gains, gsum, tabs, seq, i)
        ab = ab.reshape(bsz, seq, AB_COLS)
        oa = _attn_a_call(ab).reshape(t_tot, A_Q_W)
        ob = _attn_b_call(ab, bias, bias_block_of, i).reshape(t_tot, B_W)
        oc = _attn_c_call(c1, c4, c16, bsz, seq).reshape(t_tot, C_W)
        h1, hn, lt = _out_call(oa, ob, oc, h, ga_all, gb_all, gc_all, w_out_all, vec(g_ffn), wr_all, i)
        pos, gate, meta = _route_call(lt, tri, blk, bsz, seq, cap)
        moe = _moe_call(meta[:, :, :META_W].reshape(-1),
                        hn.reshape(bsz, seq, d),
                        pos, gate,
                        wg_all, wu_all, wd_all, cap, i)
        h = _ple_call(h1, moe.reshape(t_tot, d), p_all, vec(g_ple), wpg_all, wpp_all, i)
    return h.reshape(bsz, seq, d)
```

```python
import functools

import numpy as np
import jax
import jax.numpy as jnp
from jax import lax
from jax.experimental import pallas as pl
from jax.experimental.pallas import tpu as pltpu

F32 = jnp.float32
BF16 = jnp.bfloat16
I32 = jnp.int32

D_MODEL = 1024
HEAD_DIM = 64
A_Q_HEADS = 4
A_KV_HEADS = 2
B_HEADS = 6
C_HEADS = 6
A_Q_W = A_Q_HEADS * HEAD_DIM
A_KV_W = A_KV_HEADS * HEAD_DIM
B_W = B_HEADS * HEAD_DIM
C_W = C_HEADS * HEAD_DIM
IN_COLS = A_Q_W + 2 * A_KV_W + 3 * B_W + 3 * C_W
GRID_W = 64
AXIAL_THETA = 10000.0
NA_ROWS = 8
NA_COLS = 16
C_BRANCHES = ((128, 1), (512, 4), (2048, 16))
ROPE_THETA = 500000.0
ROPE_DIMS = HEAD_DIM // 4
N_EXPERTS = 16
EC_CAPACITY = 2
EPS = 1e-6
NEG_INF = -1e30
LOG2_E = 1.4426950408889634

LANES = 128
MXU_N = 256
VMEM_LIMIT = 56 * 1024 * 1024
VMEM_LIMIT_MOE = 60 * 1024 * 1024

N_CB = IN_COLS // LANES
CB_QA, CB_KA, CB_VA = 0, 2, 3
CB_QB, CB_KB, CB_VB = 4, 7, 10
CB_QC, CB_KC, CB_VC = 13, 16, 19
CB_KIND = ("A", "A", "A", "V") + ("N",) * 6 + ("V",) * 3 + ("C",) * 6 + ("V",) * 3
AB_COLS = CB_QC * LANES
C_COLS = 3 * C_W
N_PAIRS_C = C_W // LANES
DILS = tuple(d for _, d in C_BRANCHES)

TM = 512
TM_CHAIN = 256
TQ_A = 1024
TQ_A_CHAIN = 128
NA_QROWS = 4
NA_KROWS = 12
NA_UNROLL = 14
TQ_C = 128
C_UNROLL = 30
TOK_BLK = 256
MOE_UNROLL = 16
N_OUT_CHUNKS = 8
META_W = 32


def _rms(x, g):
    return x * lax.rsqrt(jnp.mean(x * x, axis=-1, keepdims=True) + EPS) * g


def _proj_kernel(h_ref, g_ref, w_ref, gain_ref, gsum_ref, ca_ref, sma_ref, spa_ref,
                 cc_ref, smc_ref, spc_ref, ab_ref, c1_ref, c4_ref, c16_ref, cs_ref):
    gsum = gsum_ref[...]
    per = MXU_N // LANES
    n_c = IN_COLS // MXU_N
    normed = [c for c in range(n_c) if any(CB_KIND[cb] != "V" for cb in range(c * per, (c + 1) * per))]
    for r0 in range(0, TM, TM_CHAIN):
        rows = slice(r0, r0 + TM_CHAIN)
        a = _rms(h_ref[rows, :], g_ref[...]).astype(BF16)
        accs = [jnp.dot(a, w_ref[:, c * MXU_N:(c + 1) * MXU_N], preferred_element_type=F32)
                for c in range(n_c)]
        sq = jnp.concatenate([accs[c] * accs[c] for c in normed], axis=0)
        hi = sq.astype(BF16)
        lo = (sq - hi.astype(F32)).astype(BF16)
        ss = jnp.dot(hi, gsum, preferred_element_type=F32) + jnp.dot(lo, gsum, preferred_element_type=F32)
        inv_all = lax.rsqrt(ss * (1.0 / HEAD_DIM) + EPS)
        for c in range(n_c):
            acc = accs[c]
            if c in normed:
                inv = inv_all[normed.index(c) * TM_CHAIN:(normed.index(c) + 1) * TM_CHAIN]
            for hf, cb in enumerate(range(c * per, (c + 1) * per)):
                t = acc[:, hf * LANES:(hf + 1) * LANES]
                kind = CB_KIND[cb]
                if kind != "V":
                    t = t * inv[:, hf * LANES:(hf + 1) * LANES] * gain_ref[:, cb * LANES:(cb + 1) * LANES]
                    if kind == "A":
                        sh = HEAD_DIM // 4
                        t = (t * ca_ref[rows, :] + pltpu.roll(t, LANES - sh, 1) * sma_ref[rows, :]
                             + pltpu.roll(t, sh, 1) * spa_ref[rows, :])
                    elif kind == "C":
                        sh = ROPE_DIMS // 2
                        t = (t * cc_ref[rows, :] + pltpu.roll(t, LANES - sh, 1) * smc_ref[rows, :]
                             + pltpu.roll(t, sh, 1) * spc_ref[rows, :])
                if cb < CB_QC:
                    ab_ref[rows, cb * LANES:(cb + 1) * LANES] = t.astype(BF16)
                else:
                    cs_ref[cb - CB_QC, rows, :] = t
        for p in range(N_PAIRS_C):
            for which in range(3):
                src = which * N_PAIRS_C + p
                for dil, ref in zip(DILS, (c1_ref, c4_ref, c16_ref)):
                    n = TM_CHAIN // dil
                    for r in range(dil):
                        dst = ((p * dil + r) * 3 + which) * LANES
                        ref[r0 // dil:r0 // dil + n, dst:dst + LANES] = (
                            cs_ref[src, pl.ds(r0 + r, n, stride=dil), :].astype(BF16))


def _layer_spec(a, layer):
    return pl.BlockSpec((None,) + a.shape[1:], lambda *_: (layer, 0, 0))


def _proj_call(h2d, g, w, gain, gsum, tabs, seq, layer):
    t_tot = h2d.shape[0]
    nseq = seq // TM
    row = lambda i: (i, 0)
    fixed = lambda i: (0, 0)
    tab = lambda i: (i % nseq, 0)
    return pl.pallas_call(
        _proj_kernel,
        grid=(t_tot // TM,),
        in_specs=[pl.BlockSpec((TM, D_MODEL), row),
                  _layer_spec(g, layer), _layer_spec(w, layer), _layer_spec(gain, layer),
                  pl.BlockSpec((MXU_N, MXU_N), fixed)]
                 + [pl.BlockSpec((TM, LANES), tab)] * 6,
        out_specs=[pl.BlockSpec((TM, AB_COLS), row)]
                  + [pl.BlockSpec((TM // dil, dil * C_COLS), row) for dil in DILS],
        out_shape=[jax.ShapeDtypeStruct((t_tot, AB_COLS), BF16)]
                  + [jax.ShapeDtypeStruct((t_tot // dil, dil * C_COLS), BF16) for dil in DILS],
        scratch_shapes=[pltpu.VMEM((C_COLS // LANES, TM, LANES), F32)],
        compiler_params=pltpu.CompilerParams(dimension_semantics=("parallel",),
                                             vmem_limit_bytes=VMEM_LIMIT),
    )(h2d, g, w, gain, gsum, *tabs)


def _stack_heads(q):
    qf = q.astype(F32)
    lo = lax.broadcasted_iota(I32, qf.shape, 1) < HEAD_DIM
    return jnp.concatenate([jnp.where(lo, qf, 0.0), jnp.where(lo, 0.0, qf)], axis=0).astype(BF16)


def _merge_heads(o):
    n = o.shape[0] // 2
    lo = lax.broadcasted_iota(I32, (n, LANES), 1) < HEAD_DIM
    return jnp.where(lo, o[:n], o[n:])


def _scores(qs, k):
    return lax.dot_general(qs, k, (((1,), (1,)), ((), ())), preferred_element_type=F32)


def _attn_a_kernel(q_ref, k_ref, v_ref, o_ref):
    k = k_ref[0]
    v = v_ref[0]
    for rc in range(TQ_A // TQ_A_CHAIN):
        rows = slice(rc * TQ_A_CHAIN, (rc + 1) * TQ_A_CHAIN)
        for blk in range(A_Q_W // LANES):
            qs = _stack_heads(q_ref[0, rows, blk * LANES:(blk + 1) * LANES])
            s = _scores(qs, k)
            m = jnp.max(s, axis=-1, keepdims=True)
            p = jnp.exp2(s - m)
            l = jnp.sum(p, axis=-1, keepdims=True)
            o = jnp.dot(p.astype(BF16), v, preferred_element_type=F32) / l
            o_ref[0, rows, blk * LANES:(blk + 1) * LANES] = _merge_heads(o)


def _attn_a_call(qkv):
    bsz, seq, _ = qkv.shape
    return pl.pallas_call(
        _attn_a_kernel,
        grid=(bsz, seq // TQ_A),
        in_specs=[pl.BlockSpec((1, TQ_A, A_Q_W), lambda b, i: (b, i, 0)),
                  pl.BlockSpec((1, seq, LANES), lambda b, i: (b, 0, CB_KA)),
                  pl.BlockSpec((1, seq, LANES), lambda b, i: (b, 0, CB_VA))],
        out_specs=pl.BlockSpec((1, TQ_A, A_Q_W), lambda b, i: (b, i, 0)),
        out_shape=jax.ShapeDtypeStruct((bsz, seq, A_Q_W), F32),
        compiler_params=pltpu.CompilerParams(dimension_semantics=("parallel", "parallel"),
                                             vmem_limit_bytes=VMEM_LIMIT),
    )(qkv, qkv, qkv)


def _attn_b_kernel(q_ref, k_ref, v_ref, bias_ref, o_ref, *, n_rows, block_of):
    tq = NA_QROWS * GRID_W
    n_g = n_rows // NA_QROWS

    def bias_tile(cfg):
        return jnp.concatenate(
            [jnp.concatenate([bias_ref[0, t, block_of[cfg][qr][kb]] for kb in range(NA_KROWS // 2)], axis=1)
             for t in range(2) for qr in range(NA_QROWS)], axis=0)

    def group(g, cfg):
        q0 = pl.multiple_of(g * tq, tq)
        krow0 = jnp.clip(g * NA_QROWS - NA_ROWS // 2, 0, n_rows - NA_KROWS)
        t0 = pl.multiple_of(krow0 * GRID_W, GRID_W)
        kw = k_ref[0, pl.ds(t0, NA_KROWS * GRID_W), :]
        vw = v_ref[0, pl.ds(t0, NA_KROWS * GRID_W), :]
        s = _scores(_stack_heads(q_ref[0, pl.ds(q0, tq), :]), kw) + bias_tile(cfg)
        m = jnp.max(s, axis=-1, keepdims=True)
        p = jnp.exp2(s - m)
        l = jnp.sum(p, axis=-1, keepdims=True)
        o = jnp.dot(p.astype(BF16), vw, preferred_element_type=F32) / l
        o_ref[0, pl.ds(q0, tq), :] = _merge_heads(o)

    def interior(g, carry):
        group(g, 1)
        return carry

    group(0, 0)
    lax.fori_loop(1, n_g - 1, interior, 0, unroll=NA_UNROLL)
    group(n_g - 1, 2)


def _attn_b_call(qkv, bias, block_of, layer):
    bsz, seq, _ = qkv.shape
    spec = lambda cb: pl.BlockSpec((1, seq, LANES), lambda p, b: (b, 0, cb + p))
    return pl.pallas_call(
        functools.partial(_attn_b_kernel, n_rows=seq // GRID_W, block_of=block_of),
        grid=(B_W // LANES, bsz),
        in_specs=[spec(CB_QB), spec(CB_KB), spec(CB_VB),
                  pl.BlockSpec((None, 1) + bias.shape[2:], lambda p, b: (layer, p, 0, 0, 0, 0))],
        out_specs=pl.BlockSpec((1, seq, LANES), lambda p, b: (b, 0, p)),
        out_shape=jax.ShapeDtypeStruct((bsz, seq, B_W), F32),
        compiler_params=pltpu.CompilerParams(dimension_semantics=("parallel",) * 2,
                                             vmem_limit_bytes=VMEM_LIMIT),
    )(qkv, qkv, qkv, bias)


def _na_bias_blocks(rpb, n_rows):
    wr = NA_ROWS
    n_co = 2 * NA_COLS - 1
    col_sel = np.zeros((GRID_W, GRID_W, n_co), np.float32)
    for c in range(GRID_W):
        cstart = int(np.clip(c - NA_COLS // 2, 0, GRID_W - NA_COLS))
        for kc in range(cstart, cstart + NA_COLS):
            col_sel[c, kc, kc - c + NA_COLS - 1] = 1.0
    col_out = np.where(col_sel.sum(-1) > 0, 0.0, NEG_INF).astype(np.float32)
    per_off = jnp.einsum("lhab,cjb->lhacj", rpb.astype(F32) * LOG2_E, col_sel,
                         precision=lax.Precision.HIGHEST) + col_out
    masked = jnp.full(per_off.shape[:2] + (GRID_W, GRID_W), NEG_INF, F32)
    keys, block_of = [], []
    for r0 in (0, NA_QROWS, n_rows - NA_QROWS):
        start = int(np.clip(r0 - wr // 2, 0, n_rows - NA_KROWS))
        geometry = []
        for qi in range(NA_QROWS):
            r = r0 + qi
            rs = int(np.clip(r - wr // 2, 0, n_rows - wr))
            offs = [start + kj - r + NA_ROWS - 1 if rs <= start + kj < rs + wr else None
                    for kj in range(NA_KROWS)]
            row = []
            for kb in range(NA_KROWS // 2):
                key = (offs[2 * kb], offs[2 * kb + 1])
                if key not in keys:
                    keys.append(key)
                row.append(keys.index(key))
            geometry.append(tuple(row))
        block_of.append(tuple(geometry))
    half = lambda a: masked if a is None else per_off[:, :, a]
    blocks = jnp.stack([jnp.concatenate([half(a0), half(a1)], axis=-1) for a0, a1 in keys], axis=2)
    n_l = rpb.shape[0]
    blocks = blocks.reshape(n_l, B_HEADS // 2, 2, len(keys), GRID_W, 2 * GRID_W)
    return blocks, tuple(block_of)


def _band_tile(q, kw, vw, band):
    s = _scores(_stack_heads(q), kw) + band
    m = jnp.max(s, axis=-1, keepdims=True)
    p = jnp.exp2(s - m)
    l = jnp.sum(p, axis=-1, keepdims=True)
    o = jnp.dot(p.astype(BF16), vw, preferred_element_type=F32) / l
    lse = jnp.broadcast_to(m + jnp.log2(l), (2 * TQ_C, LANES))
    return _merge_heads(o), _merge_heads(lse)


def _attn_c_kernel(c1_ref, c4_ref, c16_ref, band_ref, o_ref, o2_ref, l2_ref, o3_ref, l3_ref, *, seq):
    def branch(ref, window, dil, r, emit):
        length = seq // dil
        half = window // 2 // dil
        n_q = length // TQ_C
        win = min(TQ_C + 2 * half, length)
        base = r * 3 * LANES

        def tile(qi, band):
            if isinstance(qi, int):
                q0 = qi * TQ_C
                ks = min(max(q0 - half, 0), length - win)
            else:
                q0 = pl.multiple_of(qi * TQ_C, TQ_C)
                ks = pl.multiple_of(jnp.clip(q0 - half, 0, length - win), half)
            o, lse = _band_tile(ref[0, pl.ds(q0, TQ_C), base:base + LANES],
                                ref[0, pl.ds(ks, win), base + LANES:base + 2 * LANES],
                                ref[0, pl.ds(ks, win), base + 2 * LANES:base + 3 * LANES], band)
            emit(q0, o, lse)

        def interior(qi, carry):
            tile(qi, band_ref[1])
            return carry

        tile(0, band_ref[0])
        if n_q > 2:
            lax.fori_loop(1, n_q - 1, interior, 0, unroll=min(C_UNROLL, n_q - 2))
        tile(n_q - 1, band_ref[2])

    for ref, (window, dil), o_scr, l_scr in ((c4_ref, C_BRANCHES[1], o2_ref, l2_ref),
                                            (c16_ref, C_BRANCHES[2], o3_ref, l3_ref)):
        for r in range(dil):
            def emit(q0, o, lse, r=r, dil=dil, o_scr=o_scr, l_scr=l_scr):
                rows = pl.ds(r + dil * q0, TQ_C, stride=dil)
                o_scr[rows, :] = o
                l_scr[rows, :] = lse
            branch(ref, window, dil, r, emit)

    def emit1(q0, o1, l1):
        rows = pl.ds(q0, TQ_C)
        l2, l3 = l2_ref[rows, :], l3_ref[rows, :]
        lm = jnp.maximum(jnp.maximum(l1, l2), l3)
        e1, e2, e3 = jnp.exp2(l1 - lm), jnp.exp2(l2 - lm), jnp.exp2(l3 - lm)
        o_ref[0, rows, :] = (e1 * o1 + e2 * o2_ref[rows, :] + e3 * o3_ref[rows, :]) / (e1 + e2 + e3)

    branch(c1_ref, C_BRANCHES[0][0], 1, 0, emit1)


def _band_masks():
    half = C_BRANCHES[0][0] // 2
    assert all(w // 2 // d == half for w, d in C_BRANCHES)
    i = np.arange(2 * TQ_C)[:, None] % TQ_C
    j = np.arange(TQ_C + 2 * half)[None, :]
    return jnp.asarray(np.stack([np.where(np.abs(off + i - j) <= half, 0.0, NEG_INF)
                                 for off in (0, half, 2 * half)]).astype(np.float32))


def _attn_c_call(c1, c4, c16, bsz, seq):
    views = [c.reshape(bsz, seq // dil, dil * C_COLS) for c, dil in zip((c1, c4, c16), DILS)]
    spec = lambda dil: pl.BlockSpec((1, seq // dil, dil * 3 * LANES), lambda b, p: (b, 0, p))
    band = _band_masks()
    assert seq // DILS[-1] >= band.shape[2]
    return pl.pallas_call(
        functools.partial(_attn_c_kernel, seq=seq),
        grid=(bsz, N_PAIRS_C),
        in_specs=[spec(dil) for dil in DILS] + [pl.BlockSpec(band.shape, lambda b, p: (0, 0, 0))],
        out_specs=pl.BlockSpec((1, seq, LANES), lambda b, p: (b, 0, p)),
        out_shape=jax.ShapeDtypeStruct((bsz, seq, C_W), F32),
        scratch_shapes=[pltpu.VMEM((seq, LANES), F32)] * 4,
        compiler_params=pltpu.CompilerParams(dimension_semantics=("parallel",) * 2,
                                             vmem_limit_bytes=VMEM_LIMIT),
    )(*views, band)


def _out_kernel(oa_ref, ob_ref, oc_ref, h_ref, ga_ref, gb_ref, gc_ref, w_ref, gf_ref, wr_ref,
                h1_ref, hn_ref, lt_ref):
    mixed = jnp.concatenate([_rms(oa_ref[...], ga_ref[...]),
                             _rms(ob_ref[...], gb_ref[...]),
                             _rms(oc_ref[...], gc_ref[...])], axis=-1).astype(BF16)
    h1 = h_ref[...] + jnp.dot(mixed, w_ref[...], preferred_element_type=F32)
    h1_ref[...] = h1
    hn = _rms(h1, gf_ref[...])
    hi = hn.astype(BF16)
    hn_ref[...] = hi
    lo = (hn - hi.astype(F32)).astype(BF16)
    both = jnp.dot(hi, wr_ref[...], preferred_element_type=F32)
    logits = (both[:, :LANES] + both[:, LANES:]
              + jnp.dot(lo, wr_ref[:, :LANES], preferred_element_type=F32))
    lt_ref[...] = logits.T[:N_EXPERTS, :]


def _out_call(oa, ob, oc, h2d, ga, gb, gc, w, gf, wr, layer):
    t_tot = h2d.shape[0]
    row = lambda i: (i, 0)
    rs = lambda w_: pl.BlockSpec((TM, w_), row)
    fs = lambda a: _layer_spec(a, layer)
    return pl.pallas_call(
        _out_kernel,
        grid=(t_tot // TM,),
        in_specs=[rs(A_Q_W), rs(B_W), rs(C_W), rs(D_MODEL)]
                 + [fs(ga), fs(gb), fs(gc), fs(w), fs(gf), fs(wr)],
        out_specs=[rs(D_MODEL), rs(D_MODEL), pl.BlockSpec((N_EXPERTS, TM), lambda i: (0, i))],
        out_shape=[jax.ShapeDtypeStruct((t_tot, D_MODEL), F32),
                   jax.ShapeDtypeStruct((t_tot, D_MODEL), BF16),
                   jax.ShapeDtypeStruct((N_EXPERTS, t_tot), F32)],
        compiler_params=pltpu.CompilerParams(dimension_semantics=("parallel",),
                                             vmem_limit_bytes=VMEM_LIMIT),
    )(oa, ob, oc, h2d, ga, gb, gc, w, gf, wr)


def _route_kernel(lt_ref, tri_ref, blk_ref, pos_ref, gate_ref, meta_ref, *, cap):
    l = lt_ref[...]
    seq = l.shape[1]
    m = jnp.max(l, axis=0, keepdims=True)
    ex = jnp.exp(l - m)
    aff = ex / jnp.sum(ex, axis=0, keepdims=True)
    for jb in range(seq // TOK_BLK):
        gate_ref[0, :, jb, 0, :] = aff[:, jb * TOK_BLK:(jb + 1) * TOK_BLK]
    bits = lax.bitcast_convert_type(aff, I32)

    def count(mask):
        return jnp.sum(mask.astype(F32), axis=1, keepdims=True)

    def search(i, t):
        cand = t | (1 << (30 - i))
        return jnp.where(count(bits >= cand) >= cap, cand, t)

    thr = lax.fori_loop(0, 31, search, jnp.zeros((N_EXPERTS, 1), I32))
    gt = bits > thr
    eq = bits == thr
    need = cap - count(gt)

    tri = tri_ref[...]

    def excl_prefix(mask):
        mb = jnp.where(mask, 1.0, 0.0).astype(BF16)
        parts = []
        run = jnp.zeros((N_EXPERTS, 1), F32)
        for c in range(seq // LANES):
            ch = mb[:, c * LANES:(c + 1) * LANES]
            parts.append(jnp.dot(ch, tri, preferred_element_type=F32) + run)
            run = run + jnp.sum(ch.astype(F32), axis=1, keepdims=True)
        return jnp.concatenate(parts, axis=1)

    sel = gt | (eq & (excl_prefix(eq) < need))
    pos = excl_prefix(sel)
    posi = jnp.where(sel, pos, -1.0).astype(I32)
    for jb in range(seq // TOK_BLK):
        pos_ref[0, :, jb, 0, :] = posi[:, jb * TOK_BLK:(jb + 1) * TOK_BLK]
    sb = jnp.where(sel, 1.0, 0.0).astype(BF16)
    sc = jnp.dot(sb, blk_ref[...], preferred_element_type=F32)
    start = sc[:, :LANES]
    cnt = sc[:, LANES:]
    ilo = jnp.minimum(jnp.floor(start * (1.0 / TOK_BLK)), cap // TOK_BLK - 1.0)
    cross = jnp.where(start + cnt > (ilo + 1.0) * TOK_BLK, 1.0, 0.0)
    lane = lax.broadcasted_iota(I32, cross.shape, 1)
    has = jnp.sum(cross, axis=1, keepdims=True)
    jstar = jnp.sum(cross * lane.astype(F32), axis=1, keepdims=True)
    ifix = cap // TOK_BLK - has
    n_tb = seq // TOK_BLK
    meta = jnp.where(lane < n_tb, ilo, jnp.where(lane == n_tb, jstar, jnp.where(lane == n_tb + 1, ifix, 0.0)))
    meta_ref[0] = meta.astype(I32)


def _route_call(lt, tri, blk, bsz, seq, cap):
    e = N_EXPERTS
    big = lambda dt: jax.ShapeDtypeStruct((bsz, e, seq // TOK_BLK, 1, TOK_BLK), dt)
    small = jax.ShapeDtypeStruct((bsz, e, LANES), I32)
    bspec = pl.BlockSpec((1, e, seq // TOK_BLK, 1, TOK_BLK), lambda b: (b, 0, 0, 0, 0))
    sspec = pl.BlockSpec((1, e, LANES), lambda b: (b, 0, 0))
    return pl.pallas_call(
        functools.partial(_route_kernel, cap=cap),
        grid=(bsz,),
        in_specs=[pl.BlockSpec((e, seq), lambda b: (0, b)),
                  pl.BlockSpec(tri.shape, lambda b: (0, 0)),
                  pl.BlockSpec(blk.shape, lambda b: (0, 0))],
        out_specs=[bspec, bspec, sspec],
        out_shape=[big(I32), big(F32), small],
        compiler_params=pltpu.CompilerParams(dimension_semantics=("parallel",),
                                             vmem_limit_bytes=VMEM_LIMIT),
    )(lt, tri, blk)


def _moe_kernel(meta_ref, hn_ref, pos_ref, gate_ref, wg_ref, wu_ref, wd_ref, o_ref,
                acc_ref, xg_ref, y_ref, *, cap, n_tb):
    b = pl.program_id(0)
    k = pl.program_id(1)
    out_rows = o_ref.shape[1]

    @pl.when(k == 0)
    def _():
        acc_ref[...] = jnp.zeros_like(acc_ref)
        y_ref[cap:, :] = jnp.zeros((TOK_BLK, y_ref.shape[1]), BF16)

    @pl.when(k < N_EXPERTS)
    def _():
        xg_ref[...] = jnp.zeros_like(xg_ref)
        slot = lax.broadcasted_iota(I32, (TOK_BLK, TOK_BLK), 0)
        mbase = (b * N_EXPERTS + k) * META_W
        jstar = meta_ref[mbase + n_tb]
        ifix = meta_ref[mbase + n_tb + 1]

        def gather(j, i):
            lp = pos_ref[0, 0, j]
            hb = hn_ref[0, pl.ds(pl.multiple_of(j * TOK_BLK, TOK_BLK), TOK_BLK), :]
            oh = jnp.where(slot + i * TOK_BLK == lp, 1.0, 0.0).astype(BF16)
            r0 = pl.multiple_of(i * TOK_BLK, TOK_BLK)
            xg_ref[pl.ds(r0, TOK_BLK), :] += jnp.dot(oh, hb, preferred_element_type=F32).astype(BF16)

        def gather_body(j, carry):
            gather(j, meta_ref[mbase + j])
            return carry

        lax.fori_loop(0, n_tb, gather_body, 0, unroll=MOE_UNROLL)
        gather(jstar, ifix)

        x = xg_ref[:cap, :]
        y = jnp.zeros((cap, D_MODEL), F32)
        d_ff = wg_ref.shape[2]
        for fc in range(d_ff // MXU_N):
            cs = slice(fc * MXU_N, (fc + 1) * MXU_N)
            g = jnp.dot(x, wg_ref[0, :, cs], preferred_element_type=F32)
            u = jnp.dot(x, wu_ref[0, :, cs], preferred_element_type=F32)
            hid = (g * jax.nn.sigmoid(g) * u).astype(BF16)
            y = y + jnp.dot(hid, wd_ref[0, cs, :], preferred_element_type=F32)
        y_ref[:cap, :] = y.astype(BF16)

        def scatter(j, i):
            lp = pos_ref[0, 0, j]
            gr = gate_ref[0, 0, j]
            t0 = pl.multiple_of(j * TOK_BLK, TOK_BLK)
            oh = jnp.where(slot + i * TOK_BLK == lp, gr, 0.0).astype(BF16)
            r0 = pl.multiple_of(i * TOK_BLK, TOK_BLK)
            acc_ref[pl.ds(t0, TOK_BLK), :] += lax.dot_general(
                oh, y_ref[pl.ds(r0, TOK_BLK), :], (((0,), (0,)), ((), ())),
                preferred_element_type=F32)

        def scatter_body(j, carry):
            scatter(j, meta_ref[mbase + j])
            return carry

        lax.fori_loop(0, n_tb, scatter_body, 0, unroll=MOE_UNROLL)
        scatter(jstar, ifix)

    @pl.when(k >= N_EXPERTS)
    def _():
        r0 = pl.multiple_of((k - N_EXPERTS) * out_rows, out_rows)
        o_ref[0] = acc_ref[pl.ds(r0, out_rows), :]


def _moe_call(meta, hn, pos, gate, wg, wu, wd, cap, layer):
    bsz, seq, d = hn.shape
    n_tb = seq // TOK_BLK
    e_idx = lambda k: jnp.minimum(k, N_EXPERTS - 1)
    tok_spec = pl.BlockSpec((1, 1, n_tb, 1, TOK_BLK), lambda b, k, *_: (b, e_idx(k), 0, 0, 0))
    w_spec = lambda w: pl.BlockSpec((1,) + w.shape[1:],
                                    lambda b, k, *_: (layer * N_EXPERTS + e_idx(k), 0, 0))
    out_rows = seq // N_OUT_CHUNKS
    grid_spec = pltpu.PrefetchScalarGridSpec(
        num_scalar_prefetch=1,
        grid=(bsz, N_EXPERTS + N_OUT_CHUNKS),
        in_specs=[pl.BlockSpec((1, seq, d), lambda b, k, *_: (b, 0, 0), pipeline_mode=pl.Buffered(1)),
                  tok_spec, tok_spec, w_spec(wg), w_spec(wu), w_spec(wd)],
        out_specs=pl.BlockSpec((1, out_rows, d),
                               lambda b, k, *_: (b, jnp.maximum(k - N_EXPERTS, 0), 0)),
        scratch_shapes=[pltpu.VMEM((seq, d), F32), pltpu.VMEM((cap + TOK_BLK, d), BF16),
                        pltpu.VMEM((cap + TOK_BLK, d), BF16)],
    )
    return pl.pallas_call(
        functools.partial(_moe_kernel, cap=cap, n_tb=n_tb),
        grid_spec=grid_spec,
        out_shape=jax.ShapeDtypeStruct((bsz, seq, d), F32),
        compiler_params=pltpu.CompilerParams(dimension_semantics=("parallel", "arbitrary"),
                                             vmem_limit_bytes=VMEM_LIMIT_MOE),
    )(meta, hn, pos, gate, wg, wu, wd)


def _ple_kernel(h_ref, moe_ref, p_ref, g_ref, wg_ref, wp_ref, o_ref):
    h2 = h_ref[...] + moe_ref[...]
    a = _rms(h2, g_ref[...]).astype(BF16)
    gate = jax.nn.sigmoid(jnp.dot(a, wg_ref[...], preferred_element_type=F32))
    proj = jnp.dot(p_ref[...].astype(BF16), wp_ref[...], preferred_element_type=F32)
    o_ref[...] = h2 + gate * proj


def _ple_call(h1, moe, p2d, g, wg, wp, layer):
    t_tot = h1.shape[0]
    row = lambda i: (i, 0)
    p_row = lambda i: (layer * (t_tot // TM) + i, 0)
    return pl.pallas_call(
        _ple_kernel,
        grid=(t_tot // TM,),
        in_specs=[pl.BlockSpec((TM, D_MODEL), row), pl.BlockSpec((TM, D_MODEL), row),
                  pl.BlockSpec((TM, p2d.shape[1]), p_row), _layer_spec(g, layer),
                  _layer_spec(wg, layer), _layer_spec(wp, layer)],
        out_specs=pl.BlockSpec((TM, D_MODEL), row),
        out_shape=jax.ShapeDtypeStruct((t_tot, D_MODEL), F32),
        compiler_params=pltpu.CompilerParams(dimension_semantics=("parallel",),
                                             vmem_limit_bytes=VMEM_LIMIT),
    )(h1, moe, p2d, g, wg, wp)


def _rope_tables(seq):
    pos = jnp.arange(seq)

    def angles(pos_f, n, theta):
        inv = theta ** (-jnp.arange(0, n, 2, dtype=F32) / n)
        return pos_f[:, None] * inv[None, :]

    ar = angles((pos // GRID_W).astype(F32), HEAD_DIM // 2, AXIAL_THETA)
    ac = angles((pos % GRID_W).astype(F32), HEAD_DIM // 2, AXIAL_THETA)
    a1 = angles(pos.astype(F32), ROPE_DIMS, ROPE_THETA)
    z16 = jnp.zeros_like(ar)
    rep = LANES // HEAD_DIM
    cos_a = jnp.tile(jnp.concatenate([jnp.cos(ar), jnp.cos(ar), jnp.cos(ac), jnp.cos(ac)], -1), (1, rep))
    sm_a = jnp.tile(jnp.concatenate([-jnp.sin(ar), z16, -jnp.sin(ac), z16], -1), (1, rep))
    sp_a = jnp.tile(jnp.concatenate([z16, jnp.sin(ar), z16, jnp.sin(ac)], -1), (1, rep))
    z8 = jnp.zeros_like(a1)
    rest = HEAD_DIM - ROPE_DIMS
    ones = jnp.ones((seq, rest), F32)
    zr = jnp.zeros((seq, rest), F32)
    cos_c = jnp.tile(jnp.concatenate([jnp.cos(a1), jnp.cos(a1), ones], -1), (1, rep))
    sm_c = jnp.tile(jnp.concatenate([-jnp.sin(a1), z8, zr], -1), (1, rep))
    sp_c = jnp.tile(jnp.concatenate([z8, jnp.sin(a1), zr], -1), (1, rep))
    return cos_a, sm_a, sp_a, cos_c, sm_c, sp_c


_A_HEAD_ORDER = (0, 2, 1, 3)
_A_PERM = np.concatenate([np.arange(h * HEAD_DIM, (h + 1) * HEAD_DIM) for h in _A_HEAD_ORDER])


def _gain_rows(qg):
    scale = HEAD_DIM ** -0.5 * LOG2_E
    n_l = qg.shape[0]
    one = lambda n: jnp.ones((n_l, n), F32)
    rep = lambda v, n: jnp.tile(v, (1, n))
    return jnp.concatenate([
        rep(qg[:, 0, 0] * scale, A_Q_HEADS), rep(qg[:, 0, 1], A_KV_HEADS), one(A_KV_W),
        rep(qg[:, 1, 0] * scale, B_HEADS), rep(qg[:, 1, 1], B_HEADS), one(B_W),
        rep(qg[:, 2, 0] * scale, C_HEADS), rep(qg[:, 2, 1], C_HEADS), one(C_W)], axis=1)[:, None, :]


def kernel(x, p, g_mix, w_in, qk_gain, na_bias, g_out, w_out, g_ffn, w_router,
           w_gate, w_up, w_down, g_ple, w_ple_gate, w_ple_proj):
    bsz, seq, d = x.shape
    depth = w_in.shape[0]
    t_tot = bsz * seq
    cap = max(1, EC_CAPACITY * seq // N_EXPERTS)
    n_tb = seq // TOK_BLK

    tabs = _rope_tables(seq)
    lane = np.arange(LANES)
    col = np.arange(MXU_N)
    gsum = jnp.asarray((col[:, None] // HEAD_DIM) == (col[None, :] // HEAD_DIM), BF16)
    tri = jnp.asarray(lane[:, None] < lane[None, :], BF16)
    tok = np.arange(seq)[:, None]
    j = np.arange(LANES)[None, :]
    blk_start = (tok < j * TOK_BLK) & (j < n_tb)
    blk_count = (tok // TOK_BLK == j) & (j < n_tb)
    blk = jnp.asarray(np.concatenate([blk_start, blk_count], axis=1), BF16)

    bias, bias_block_of = _na_bias_blocks(na_bias, seq // GRID_W)
    stack = lambda w: w.reshape((depth * N_EXPERTS,) + w.shape[2:])
    wg_all, wu_all, wd_all = stack(w_gate), stack(w_up), stack(w_down)
    p_all = p.reshape(depth * t_tot, p.shape[-1])
    vec = lambda g: g[:, None, :]
    w_in_all = jnp.concatenate([w_in[:, :, _A_PERM], w_in[:, :, A_Q_W:]], axis=2).astype(BF16)
    gains = _gain_rows(qk_gain)
    w_out_all = jnp.concatenate([w_out[:, _A_PERM], w_out[:, A_Q_W:]], axis=1).astype(BF16)
    ga_all, gb_all, gc_all = (vec(g_out[:, :A_Q_W][:, _A_PERM]), vec(g_out[:, A_Q_W:A_Q_W + B_W]),
                              vec(g_out[:, A_Q_W + B_W:]))
    wr = jnp.pad(w_router, ((0, 0), (0, 0), (0, LANES - N_EXPERTS)))
    wr_hi = wr.astype(BF16)
    wr_all = jnp.concatenate([wr_hi, (wr - wr_hi.astype(F32)).astype(BF16)], axis=2)
    wpg_all, wpp_all = w_ple_gate.astype(BF16), w_ple_proj.astype(BF16)
    h = x.reshape(t_tot, d)
    for i in range(depth):
        ab, c1, c4, c16 = _proj_call(h, vec(g_mix), w_in_all, gains, gsum, tabs, seq, i)
        ab = ab.reshape(bsz, seq, AB_COLS)
        oa = _attn_a_call(ab).reshape(t_tot, A_Q_W)
        ob = _attn_b_call(ab, bias, bias_block_of, i).reshape(t_tot, B_W)
        oc = _attn_c_call(c1, c4, c16, bsz, seq).reshape(t_tot, C_W)
        h1, hn, lt = _out_call(oa, ob, oc, h, ga_all, gb_all, gc_all, w_out_all, vec(g_ffn), wr_all, i)
        pos, gate, meta = _route_call(lt, tri, blk, bsz, seq, cap)
        moe = _moe_call(meta[:, :, :META_W].reshape(-1),
                        hn.reshape(bsz, seq, d),
                        pos, gate,
                        wg_all, wu_all, wd_all, cap, i)
        h = _ple_call(h1, moe.reshape(t_tot, d), p_all, vec(g_ple), wpg_all, wpp_all, i)
    return h.reshape(bsz, seq, d)
```
